```python
import jax, jax.numpy as jnp
from jax import lax
import numpy as np

D_MODEL = 1024
BATCH = 4
SEQ = 4096
DEPTH = 1

CTX_LEN = 256
GRID_W = 64
EPS = 1e-6
N_MOD = 9
D_FF = 2816
DN_HEADS = 4
DN_DK = 128
DN_DV = 128
DN_WIDTH = DN_HEADS * DN_DV
DN_CONV = 5
DN_CHUNK = 64
AT_HEADS = 4
AT_KV_HEADS = 2
AT_GROUP = AT_HEADS // AT_KV_HEADS
AT_HD = 128
AT_WIDTH = AT_HEADS * AT_HD
ATT_SCALE = AT_HD ** -0.5
Q_BLOCK = 128
ROPE_AXIS_DIM = AT_HD // 2
ROPE_THETA = 10000.0
D_MIX = DN_WIDTH + AT_WIDTH
LEN_DN_QKV = 3 * DN_WIDTH
LEN_DN_Z = DN_WIDTH
LEN_DN_B = 2 * DN_HEADS
LEN_DN_A = 2 * DN_HEADS
LEN_AT_Q = AT_WIDTH
LEN_AT_K = AT_KV_HEADS * AT_HD
LEN_AT_V = AT_KV_HEADS * AT_HD
OFF_DN_QKV = 0
OFF_DN_Z = OFF_DN_QKV + LEN_DN_QKV
OFF_DN_B = OFF_DN_Z + LEN_DN_Z
OFF_DN_A = OFF_DN_B + LEN_DN_B
OFF_AT_Q = OFF_DN_A + LEN_DN_A
OFF_AT_K = OFF_AT_Q + LEN_AT_Q
OFF_AT_V = OFF_AT_K + LEN_AT_K
P_IN = OFF_AT_V + LEN_AT_V

kernel_name = "hybrid_deltanet_gqa_macaron_prefix_block"


def _rmsnorm(x, gain):
    x32 = x.astype(jnp.float32)
    y = x32 * lax.rsqrt(jnp.mean(x32 * x32, axis=-1, keepdims=True) + EPS)
    return (y * gain.astype(jnp.float32)).astype(x.dtype)


def _l2norm(x):
    x32 = x.astype(jnp.float32)
    return x32 * lax.rsqrt(jnp.sum(x32 * x32, axis=-1, keepdims=True) + EPS)


def _modulate(x, gain, shift, scale):
    return _rmsnorm(x, gain) * (1 + scale) + shift


def _swiglu(h, w1, w3, w2):
    return (jax.nn.silu(h @ w1) * (h @ w3)) @ w2


def _centred_dwconv(x, w):
    k, ch = w.shape
    pad = (k - 1) // 2
    return lax.conv_general_dilated(
        x, w[:, None, :].astype(x.dtype), window_strides=(1,),
        padding=[(pad, k - 1 - pad)], dimension_numbers=('NWC', 'WIO', 'NWC'),
        feature_group_count=ch)


def _axial_rope_angles(n):
    rows = n // GRID_W
    row = jnp.repeat(jnp.arange(rows, dtype=jnp.int32), GRID_W).astype(jnp.float32)
    col = jnp.tile(jnp.arange(GRID_W, dtype=jnp.int32), rows).astype(jnp.float32)
    freqs = 1.0 / (ROPE_THETA ** (jnp.arange(0, ROPE_AXIS_DIM, 2, dtype=jnp.float32) / ROPE_AXIS_DIM))
    ang = jnp.concatenate([row[:, None] * freqs, col[:, None] * freqs], axis=-1)
    return jnp.cos(ang), jnp.sin(ang)


def _apply_rope(x, cos, sin):
    x32 = x.astype(jnp.float32)
    xp = x32.reshape(*x.shape[:-1], -1, 2)
    x0, x1 = xp[..., 0], xp[..., 1]
    cc = cos[None, :, None, :]
    ss = sin[None, :, None, :]
    out = jnp.stack([x0 * cc - x1 * ss, x0 * ss + x1 * cc], axis=-1).reshape(x.shape)
    return out.astype(x.dtype)


def _gated_delta_chunked(q, k, v, g, beta, s0):
    bsz, t, h, dk = q.shape
    dv = v.shape[-1]
    n = t // DN_CHUNK

    def chunks(a):
        a = jnp.moveaxis(a, 2, 1)
        return a.reshape(bsz, h, n, DN_CHUNK, *a.shape[3:])

    q = chunks(q) * (dk ** -0.5)
    k = chunks(k)
    v = chunks(v)
    g = chunks(g)
    beta = chunks(beta)
    gcum = jnp.cumsum(g, axis=-1)
    idx = jnp.arange(DN_CHUNK)
    lower = idx[:, None] >= idx[None, :]
    strict = idx[:, None] > idx[None, :]
    decay = jnp.exp(jnp.where(lower, gcum[..., :, None] - gcum[..., None, :], -jnp.inf))
    kb = k * beta[..., None]
    a_mat = jnp.where(strict, jnp.einsum('bhnid,bhnjd->bhnij', kb, k) * decay, 0.0)
    l_mat = a_mat + jnp.eye(DN_CHUNK, dtype=jnp.float32)
    u = lax.linalg.triangular_solve(l_mat, v * beta[..., None], left_side=True, lower=True, unit_diagonal=True)
    w = lax.linalg.triangular_solve(l_mat, kb * jnp.exp(gcum)[..., None], left_side=True, lower=True, unit_diagonal=True)
    attn = jnp.einsum('bhnid,bhnjd->bhnij', q, k) * decay
    g_last = gcum[..., -1]
    k_tail = k * jnp.exp(g_last[..., None] - gcum)[..., None]
    q_head = q * jnp.exp(gcum)[..., None]
    xs = tuple(jnp.moveaxis(a, 2, 0) for a in (q_head, k_tail, u, w, attn, g_last))

    def step(s, inp):
        qh, kt, ui, wi, ai, gl = inp
        v_new = ui - jnp.einsum('bhcd,bhde->bhce', wi, s)
        o = jnp.einsum('bhcd,bhde->bhce', qh, s) + jnp.einsum('bhcj,bhje->bhce', ai, v_new)
        s = s * jnp.exp(gl)[..., None, None] + jnp.einsum('bhcd,bhce->bhde', kt, v_new)
        return s, o

    s_fin, o = lax.scan(step, s0, xs)
    o = jnp.moveaxis(o, 0, 2).reshape(bsz, h, t, dv)
    return jnp.moveaxis(o, 1, 2), s_fin


def _delta_inputs(p, conv_w, a_log, dt_bias):
    bsz, t, _ = p.shape
    qkv = jax.nn.silu(_centred_dwconv(p[..., OFF_DN_QKV:OFF_DN_QKV + LEN_DN_QKV], conv_w))
    q, k, v = jnp.split(qkv, 3, axis=-1)
    q = _l2norm(q.reshape(bsz, t, DN_HEADS, DN_DK))
    k = _l2norm(k.reshape(bsz, t, DN_HEADS, DN_DK))
    v = v.reshape(bsz, t, DN_HEADS, DN_DV).astype(jnp.float32)
    beta = jax.nn.sigmoid(p[..., OFF_DN_B:OFF_DN_B + LEN_DN_B].astype(jnp.float32)).reshape(bsz, t, 2, DN_HEADS)
    a_raw = p[..., OFF_DN_A:OFF_DN_A + LEN_DN_A].astype(jnp.float32).reshape(bsz, t, 2, DN_HEADS)
    g = -jnp.exp(a_log.astype(jnp.float32)) * jax.nn.softplus(a_raw + dt_bias.astype(jnp.float32))
    z = p[..., OFF_DN_Z:OFF_DN_Z + LEN_DN_Z].reshape(bsz, t, DN_HEADS, DN_DV)
    return q, k, v, g, beta, z


def _flip_t(a, d):
    return jnp.flip(a, axis=1) if d == 1 else a


def _gated_out(o, z, gain, dtype):
    y = _rmsnorm(o, gain) * jax.nn.silu(z.astype(jnp.float32))
    return y.reshape(o.shape[0], o.shape[1], DN_WIDTH).astype(dtype)


def _attend(q, k, v):
    bsz, t = q.shape[:2]
    nb = t // Q_BLOCK
    qb = q.reshape(bsz, nb, Q_BLOCK, AT_KV_HEADS, AT_GROUP, AT_HD).transpose(1, 0, 2, 3, 4, 5)

    def one(qi):
        s = jnp.einsum('bqhgd,bkhd->bhgqk', qi, k, preferred_element_type=jnp.float32) * ATT_SCALE
        pr = jax.nn.softmax(s, axis=-1).astype(v.dtype)
        return jnp.einsum('bhgqk,bkhd->bqhgd', pr, v)

    o = lax.map(one, qb)
    return o.transpose(1, 0, 2, 3, 4, 5).reshape(bsz, t, AT_WIDTH)


def _mixer(h_lat, h_ctx, w_in, dn_conv, dn_a_log, dn_dt_bias, dn_norm, q_norm, k_norm, w_out,
           cos, sin, need_ctx):
    bsz = h_lat.shape[0]
    dtype = h_lat.dtype
    p_lat = h_lat @ w_in
    p_ctx = h_ctx @ w_in

    ql, kl, vl, gl, bl, zl = _delta_inputs(p_lat, dn_conv, dn_a_log, dn_dt_bias)
    qc, kc, vc, gc, bc, zc = _delta_inputs(p_ctx, dn_conv, dn_a_log, dn_dt_bias)
    o_lat = jnp.zeros(vl.shape, jnp.float32)
    o_ctx = jnp.zeros(vc.shape, jnp.float32)
    for d in range(2):
        s0 = jnp.zeros((bsz, DN_HEADS, DN_DK, DN_DV), jnp.float32)
        oc, s_ctx = _gated_delta_chunked(_flip_t(qc, d), _flip_t(kc, d), _flip_t(vc, d),
                                         _flip_t(gc[:, :, d], d), _flip_t(bc[:, :, d], d), s0)
        ol, _ = _gated_delta_chunked(_flip_t(ql, d), _flip_t(kl, d), _flip_t(vl, d),
                                     _flip_t(gl[:, :, d], d), _flip_t(bl[:, :, d], d), s_ctx)
        o_lat = o_lat + _flip_t(ol, d)
        o_ctx = o_ctx + _flip_t(oc, d)
    dn_lat = _gated_out(o_lat, zl, dn_norm, dtype)

    def qkv_at(p):
        b_, t_ = p.shape[:2]
        q = _rmsnorm(p[..., OFF_AT_Q:OFF_AT_Q + LEN_AT_Q].reshape(b_, t_, AT_HEADS, AT_HD), q_norm)
        k = _rmsnorm(p[..., OFF_AT_K:OFF_AT_K + LEN_AT_K].reshape(b_, t_, AT_KV_HEADS, AT_HD), k_norm)
        v = p[..., OFF_AT_V:OFF_AT_V + LEN_AT_V].reshape(b_, t_, AT_KV_HEADS, AT_HD)
        return q, k, v

    aq_l, ak_l, av_l = qkv_at(p_lat)
    aq_c, ak_c, av_c = qkv_at(p_ctx)
    aq_l = _apply_rope(aq_l, cos, sin)
    ak_l = _apply_rope(ak_l, cos, sin)
    k_all = jnp.concatenate([ak_l, ak_c], axis=1)
    v_all = jnp.concatenate([av_l, av_c], axis=1)
    at_lat = _attend(aq_l, k_all, v_all)

    out_lat = jnp.concatenate([dn_lat, at_lat], axis=-1) @ w_out
    if not need_ctx:
        return out_lat, None
    dn_ctx = _gated_out(o_ctx, zc, dn_norm, dtype)
    at_ctx = _attend(aq_c, ak_c, av_c)
    out_ctx = jnp.concatenate([dn_ctx, at_ctx], axis=-1) @ w_out
    return out_lat, out_ctx


def setup_inputs(seed: int = 0) -> dict:
    key = jax.random.key(seed)
    ks = jax.random.split(key, 24)
    f32 = jnp.float32

    def nrm(k, shape, scale):
        return jax.random.normal(k, shape, f32) * scale

    def gain(k, shape):
        return 1.0 + 0.02 * jax.random.normal(k, shape, f32)

    dt = jnp.exp(jax.random.uniform(ks[14], (DEPTH, 2, DN_HEADS), f32, np.log(1e-3), np.log(1e-1)))
    return {
        "x": nrm(ks[0], (BATCH, SEQ, D_MODEL), 1.0),
        "c": nrm(ks[1], (BATCH, D_MODEL), 1.0),
        "ctx": nrm(ks[2], (BATCH, CTX_LEN, D_MODEL), 1.0),
        "c_ctx": nrm(ks[3], (D_MODEL,), 1.0),
        "w_mod": nrm(ks[4], (DEPTH, D_MODEL, N_MOD * D_MODEL), D_MODEL ** -0.5),
        "b_mod": nrm(ks[5], (DEPTH, N_MOD * D_MODEL), 0.01),
        "g_ffn1": gain(ks[6], (DEPTH, D_MODEL)),
        "ffn1_w1": nrm(ks[7], (DEPTH, D_MODEL, D_FF), D_MODEL ** -0.5),
        "ffn1_w3": nrm(ks[8], (DEPTH, D_MODEL, D_FF), D_MODEL ** -0.5),
        "ffn1_w2": nrm(ks[9], (DEPTH, D_FF, D_MODEL), D_FF ** -0.5),
        "g_mix": gain(ks[10], (DEPTH, D_MODEL)),
        "w_in": nrm(ks[11], (DEPTH, D_MODEL, P_IN), D_MODEL ** -0.5),
        "dn_conv": nrm(ks[12], (DEPTH, DN_CONV, LEN_DN_QKV), DN_CONV ** -0.5),
        "dn_a_log": jnp.log(jax.random.uniform(ks[13], (DEPTH, 2, DN_HEADS), f32, 1.0, 16.0)),
        "dn_dt_bias": dt + jnp.log(-jnp.expm1(-dt)),
        "dn_norm": gain(ks[15], (DEPTH, DN_DV)),
        "q_norm": gain(ks[16], (DEPTH, AT_HD)),
        "k_norm": gain(ks[17], (DEPTH, AT_HD)),
        "w_out": nrm(ks[18], (DEPTH, D_MIX, D_MODEL), D_MIX ** -0.5),
        "g_ffn2": gain(ks[19], (DEPTH, D_MODEL)),
        "ffn2_w1": nrm(ks[20], (DEPTH, D_MODEL, D_FF), D_MODEL ** -0.5),
        "ffn2_w3": nrm(ks[21], (DEPTH, D_MODEL, D_FF), D_MODEL ** -0.5),
        "ffn2_w2": nrm(ks[22], (DEPTH, D_FF, D_MODEL), D_FF ** -0.5),
        "g_final": gain(ks[23], (D_MODEL,)),
    }


def reference(x, c, ctx, c_ctx, w_mod, b_mod, g_ffn1, ffn1_w1, ffn1_w3, ffn1_w2, g_mix, w_in,
              dn_conv, dn_a_log, dn_dt_bias, dn_norm, q_norm, k_norm, w_out, g_ffn2,
              ffn2_w1, ffn2_w3, ffn2_w2, g_final):
    cos, sin = _axial_rope_angles(x.shape[1])
    h_ctx = ctx
    for i in range(DEPTH):
        last = i == DEPTH - 1
        ml = jnp.split((jax.nn.silu(c) @ w_mod[i] + b_mod[i])[:, None, :], N_MOD, axis=-1)
        mc = jnp.split((jax.nn.silu(c_ctx) @ w_mod[i] + b_mod[i])[None, None, :], N_MOD, axis=-1)
        x = x + 0.5 * ml[2] * _swiglu(_modulate(x, g_ffn1[i], ml[0], ml[1]), ffn1_w1[i], ffn1_w3[i], ffn1_w2[i])
        h_ctx = h_ctx + 0.5 * mc[2] * _swiglu(_modulate(h_ctx, g_ffn1[i], mc[0], mc[1]), ffn1_w1[i], ffn1_w3[i], ffn1_w2[i])
        mix_l, mix_c = _mixer(_modulate(x, g_mix[i], ml[3], ml[4]), _modulate(h_ctx, g_mix[i], mc[3], mc[4]),
                              w_in[i], dn_conv[i], dn_a_log[i], dn_dt_bias[i], dn_norm[i], q_norm[i], k_norm[i],
                              w_out[i], cos, sin, not last)
        x = x + ml[5] * mix_l
        x = x + 0.5 * ml[8] * _swiglu(_modulate(x, g_ffn2[i], ml[6], ml[7]), ffn2_w1[i], ffn2_w3[i], ffn2_w2[i])
        if not last:
            h_ctx = h_ctx + mc[5] * mix_c
            h_ctx = h_ctx + 0.5 * mc[8] * _swiglu(_modulate(h_ctx, g_ffn2[i], mc[6], mc[7]), ffn2_w1[i], ffn2_w3[i], ffn2_w2[i])
    return _rmsnorm(x, g_final)
```

```python
import functools

import jax
import jax.numpy as jnp
from jax import lax
from jax.experimental import pallas as pl
from jax.experimental.pallas import tpu as pltpu

F32 = jnp.float32
BF16 = jnp.bfloat16

D_MODEL = 1024
CTX_LEN = 256
GRID_W = 64
EPS = 1e-6
N_MOD = 9
D_FF = 2816
DN_HEADS = 4
DN_DK = 128
DN_DV = 128
DN_WIDTH = DN_HEADS * DN_DV
DN_CONV = 5
AT_HEADS = 4
AT_KV_HEADS = 2
AT_HD = 128
AT_WIDTH = AT_HEADS * AT_HD
ATT_SCALE = AT_HD ** -0.5
ROPE_AXIS_DIM = AT_HD // 2
ROPE_THETA = 10000.0
LEN_DN_QKV = 3 * DN_WIDTH
OFF_DN_Z = LEN_DN_QKV
OFF_DN_B = OFF_DN_Z + DN_WIDTH
OFF_DN_A = OFF_DN_B + 2 * DN_HEADS
OFF_AT_Q = OFF_DN_A + 2 * DN_HEADS
OFF_AT_K = OFF_AT_Q + AT_WIDTH
OFF_AT_V = OFF_AT_K + AT_KV_HEADS * AT_HD
P_IN = OFF_AT_V + AT_KV_HEADS * AT_HD

LANES = 128
CHUNK = 128
FF_TILE = 256
P_PAD = LEN_DN_QKV + DN_WIDTH + AT_WIDTH + 2 * AT_KV_HEADS * AT_HD + LANES
VMEM_LIMIT = 56 * 1024 * 1024
NEG_BIG = -1e30

_NT = (((1,), (1,)), ((), ()))


def _sigmoid(x):
    return 1.0 / (1.0 + jnp.exp(-x))


def _silu(x):
    return x * _sigmoid(x)


def _rms(x, gain):
    ms = jnp.mean(x * x, axis=-1, keepdims=True)
    return x * lax.rsqrt(ms + EPS) * gain


def _dot(a, b):
    return jnp.dot(a, b, preferred_element_type=F32)


def _const_spec(shape):
    nd = len(shape)
    return pl.BlockSpec(shape, lambda *_: (0,) * nd, pipeline_mode=pl.Buffered(1))


def _params(n_axes):
    return pltpu.CompilerParams(dimension_semantics=("arbitrary",) * n_axes,
                                vmem_limit_bytes=VMEM_LIMIT)


def _mod_kernel(c_ref, w_ref, b_ref, o_ref):
    s = _silu(c_ref[...]).astype(BF16)
    o_ref[...] = _dot(s, w_ref[...].astype(BF16)) + b_ref[...]


def _mod_call(cc, w_mod, b_mod):
    d = cc.shape[1]
    n = w_mod.shape[1]
    return pl.pallas_call(
        _mod_kernel,
        out_shape=jax.ShapeDtypeStruct((cc.shape[0], n), F32),
        grid=(n // d,),
        in_specs=[pl.BlockSpec(cc.shape, lambda j: (0, 0)),
                  pl.BlockSpec((d, d), lambda j: (0, j)),
                  pl.BlockSpec((1, d), lambda j: (0, j))],
        out_specs=pl.BlockSpec((cc.shape[0], d), lambda j: (0, j)),
        compiler_params=_params(1),
        name="mod",
    )(cc, w_mod, b_mod)


def _swiglu_update(x, mod_ref, j0, g_ref, w1_ref, w3_ref, w2_ref, act_ref):
    h = (_rms(x, g_ref[...]) * (1.0 + mod_ref[j0 + 1:j0 + 2, :]) + mod_ref[j0:j0 + 1, :]).astype(BF16)
    for c in range(D_FF // FF_TILE):
        sl = slice(c * FF_TILE, (c + 1) * FF_TILE)
        a = _dot(h, w1_ref[:, sl])
        b = _dot(h, w3_ref[:, sl])
        act_ref[:, sl] = (_silu(a) * b).astype(BF16)
    y = _dot(act_ref[...], w2_ref[...])
    return x + (0.5 * mod_ref[j0 + 2:j0 + 3, :]) * y


def _ffn_kernel(x_ref, mod_ref, g_ref, w1_ref, w3_ref, w2_ref, o_ref, act_ref):
    o_ref[...] = _swiglu_update(x_ref[...], mod_ref, 0, g_ref, w1_ref, w3_ref, w2_ref, act_ref)


def _ffn_call(x2d, mod3, mod_row, g, w1, w3, w2, tm, name):
    n, d = x2d.shape
    return pl.pallas_call(
        _ffn_kernel,
        out_shape=jax.ShapeDtypeStruct((n, d), F32),
        grid=(n // tm,),
        in_specs=[pl.BlockSpec((tm, d), lambda i: (i, 0)),
                  pl.BlockSpec((None, N_MOD, d), lambda i: (mod_row(i), 0, 0)),
                  _const_spec((1, d)),
                  _const_spec(w1.shape), _const_spec(w3.shape), _const_spec(w2.shape)],
        out_specs=pl.BlockSpec((tm, d), lambda i: (i, 0)),
        scratch_shapes=[pltpu.VMEM((tm, D_FF), BF16)],
        compiler_params=_params(1),
        name=name,
    )(x2d, mod3, g, w1, w3, w2)


def _rope(x, cos, sin_signed, even):
    nxt = pltpu.roll(x, LANES - 1, axis=1)
    prv = pltpu.roll(x, 1, axis=1)
    return x * cos + jnp.where(even, nxt, prv) * sin_signed


def _inproj_kernel(x_ref, mod_ref, g_ref, w_ref, alog_ref, dtb_ref, qn_ref, kn_ref, cos_ref, sin_ref,
                   qkv_ref, z_ref, bg_ref, aq_ref, ak_ref, av_ref, *, rope):
    tm = x_ref.shape[0]
    x = x_ref[...]
    h = (_rms(x, g_ref[...]) * (1.0 + mod_ref[4:5, :]) + mod_ref[3:4, :]).astype(BF16)

    for part in range(3):
        p = _dot(h, w_ref[:, part * DN_WIDTH:(part + 1) * DN_WIDTH])
        for hd in range(DN_HEADS):
            qkv_ref[part * DN_HEADS + hd] = p[:, hd * LANES:(hd + 1) * LANES].astype(BF16)
    p = _dot(h, w_ref[:, LEN_DN_QKV:LEN_DN_QKV + DN_WIDTH])
    for hd in range(DN_HEADS):
        z_ref[hd] = p[:, hd * LANES:(hd + 1) * LANES].astype(BF16)

    lane = lax.broadcasted_iota(jnp.int32, (tm, LANES), 1)
    even = (lane % 2) == 0
    if rope:
        cos = cos_ref[...]
        sin = sin_ref[...]

    off = LEN_DN_QKV + DN_WIDTH
    p = _dot(h, w_ref[:, off:off + AT_WIDTH])
    for hd in range(AT_HEADS):
        q = _rms(p[:, hd * LANES:(hd + 1) * LANES], qn_ref[...])
        if rope:
            q = _rope(q, cos, sin, even)
        aq_ref[hd // 2, :, (hd % 2) * LANES:(hd % 2 + 1) * LANES] = (q * ATT_SCALE).astype(BF16)
    off += AT_WIDTH
    p = _dot(h, w_ref[:, off:off + 2 * AT_KV_HEADS * AT_HD])
    for hd in range(AT_KV_HEADS):
        k = _rms(p[:, hd * LANES:(hd + 1) * LANES], kn_ref[...])
        if rope:
            k = _rope(k, cos, sin, even)
        ak_ref[hd] = k.astype(BF16)
        av_ref[hd] = p[:, (AT_KV_HEADS + hd) * LANES:(AT_KV_HEADS + hd + 1) * LANES].astype(BF16)

    off += 2 * AT_KV_HEADS * AT_HD
    p = _dot(h, w_ref[:, off:off + LANES])
    beta = _sigmoid(p)
    t = p + dtb_ref[...]
    softplus = jnp.maximum(t, 0.0) + jnp.log(1.0 + jnp.exp(-jnp.abs(t)))
    g = -jnp.exp(alog_ref[...]) * softplus
    ri = lax.broadcasted_iota(jnp.int32, (CHUNK, CHUNK), 0)
    ci = lax.broadcasted_iota(jnp.int32, (CHUNK, CHUNK), 1)
    ltri = jnp.where(ri >= ci, 1.0, 0.0).astype(BF16)
    lane_c = lax.broadcasted_iota(jnp.int32, (CHUNK, LANES), 1)
    for c in range(tm // CHUNK):
        rows = slice(c * CHUNK, (c + 1) * CHUNK)
        gc = g[rows]
        g1 = gc.astype(BF16)
        r1 = gc - g1.astype(F32)
        g2 = r1.astype(BF16)
        g3 = (r1 - g2.astype(F32)).astype(BF16)
        pre = _dot(ltri, g1) + _dot(ltri, g2) + _dot(ltri, g3)
        suf = pre[CHUNK - 1:CHUNK, :] - pre + gc
        gcs = jnp.where(lane_c < 8 + DN_HEADS, pre, suf)
        bg_ref[rows, :] = jnp.where(lane_c < 2 * DN_HEADS, beta[rows], gcs)


def _inproj_call(x2d, mod3, mod_row, g, w_r, alog_row, dtb_row, qn, kn, cos, sin, bsz, t, tm, rope, name):
    n, d = x2d.shape
    tpb = t // tm
    bt = lambda i: (i // tpb, 0, i % tpb, 0)
    out_shape = (
        jax.ShapeDtypeStruct((bsz, 3 * DN_HEADS, t, LANES), BF16),
        jax.ShapeDtypeStruct((bsz, DN_HEADS, t, LANES), BF16),
        jax.ShapeDtypeStruct((n, LANES), F32),
        jax.ShapeDtypeStruct((bsz, AT_KV_HEADS, t, 2 * AT_HD), BF16),
        jax.ShapeDtypeStruct((bsz, AT_KV_HEADS, t, AT_HD), BF16),
        jax.ShapeDtypeStruct((bsz, AT_KV_HEADS, t, AT_HD), BF16),
    )
    out_specs = (
        pl.BlockSpec((None, 3 * DN_HEADS, tm, LANES), bt),
        pl.BlockSpec((None, DN_HEADS, tm, LANES), bt),
        pl.BlockSpec((tm, LANES), lambda i: (i, 0)),
        pl.BlockSpec((None, AT_KV_HEADS, tm, 2 * AT_HD), bt),
        pl.BlockSpec((None, AT_KV_HEADS, tm, AT_HD), bt),
        pl.BlockSpec((None, AT_KV_HEADS, tm, AT_HD), bt),
    )
    return pl.pallas_call(
        functools.partial(_inproj_kernel, rope=rope),
        out_shape=out_shape,
        grid=(n // tm,),
        in_specs=[pl.BlockSpec((tm, d), lambda i: (i, 0)),
                  pl.BlockSpec((None, N_MOD, d), lambda i: (mod_row(i), 0, 0)),
                  _const_spec((1, d)),
                  _const_spec(w_r.shape),
                  _const_spec((1, LANES)), _const_spec((1, LANES)),
                  _const_spec((1, LANES)), _const_spec((1, LANES)),
                  pl.BlockSpec((tm, LANES), lambda i: (i % tpb, 0)),
                  pl.BlockSpec((tm, LANES), lambda i: (i % tpb, 0))],
        out_specs=out_specs,
        compiler_params=_params(1),
        name=name,
    )(x2d, mod3, g, w_r, alog_row, dtb_row, qn, kn, cos, sin)


def _dn_conv_fill(src_ref, w_ref, xpad_ref, t, emit):
    nblk = t // CHUNK
    xpad_ref[0:8, :] = jnp.zeros((8, LANES), F32)
    xpad_ref[8 + t:16 + t, :] = jnp.zeros((8, LANES), F32)

    def fill(r, carry):
        r0 = pl.multiple_of(r * CHUNK, CHUNK)
        xpad_ref[pl.ds(r0 + 8, CHUNK), :] = src_ref[pl.ds(r0, CHUNK), :].astype(F32)
        return carry

    lax.fori_loop(0, nblk, fill, 0)
    w = w_ref[...]
    pad = (DN_CONV - 1) // 2
    win = CHUNK + 16

    def conv(r, carry):
        r0 = pl.multiple_of(r * CHUNK, CHUNK)
        val = xpad_ref[pl.ds(r0, win), :]
        acc = val * w[pad:pad + 1, :]
        for j in range(DN_CONV):
            if j != pad:
                acc = acc + pltpu.roll(val, (pad - j) % win, axis=0) * w[j:j + 1, :]
        y = acc[8:8 + CHUNK]
        emit(r0, _silu(y))
        return carry

    lax.fori_loop(0, nblk, conv, 0)


def _dn_chunk(row, d, s_ref, qs_ref, ks_ref, vs_ref, bg_ref):
    k = ks_ref[pl.ds(row, CHUNK), :]
    q = qs_ref[pl.ds(row, CHUNK), :]
    v = vs_ref[pl.ds(row, CHUNK), :]
    bg = bg_ref[pl.ds(row, CHUNK), :]
    beta = bg[:, d:d + 1]
    g_c = jnp.broadcast_to(bg[:, 2 + d:3 + d], (CHUNK, CHUNK))
    g_r = g_c.T
    ri = lax.broadcasted_iota(jnp.int32, (CHUNK, CHUNK), 0)
    ci = lax.broadcasted_iota(jnp.int32, (CHUNK, CHUNK), 1)
    incl = (ri >= ci) if d == 0 else (ri <= ci)
    strict = (ri > ci) if d == 0 else (ri < ci)
    decay = jnp.exp(jnp.where(incl, g_c - g_r, NEG_BIG))
    kf = k.astype(F32)
    kb = kf * beta
    aq = lax.dot_general(jnp.concatenate([kb.astype(BF16), q], axis=0), k, _NT,
                         preferred_element_type=F32)
    a = jnp.where(strict, aq[:CHUNK] * decay, 0.0)
    attn = aq[CHUNK:] * decay

    eye = jnp.where(ri == ci, 1.0, 0.0)
    lo, hi = (ci, ri) if d == 0 else (ri, ci)
    t_inv = eye - jnp.where((hi >> 1) == (lo >> 1), a, 0.0)
    lvl = 1
    while (1 << lvl) < CHUNK:
        hb = hi >> lvl
        lb = lo >> lvl
        sub = jnp.logical_and(hb == lb + 1, (hb & 1) == 1)
        a_sub = jnp.where(sub, a, 0.0).astype(BF16)
        tb = t_inv.astype(BF16)
        t_inv = t_inv - _dot(_dot(tb, a_sub).astype(BF16), tb)
        lvl += 1
    toff = t_inv - eye

    eg = jnp.exp(g_c)
    rhs = jnp.concatenate([v * beta, kb * eg], axis=1)
    uw = rhs + _dot(toff.astype(BF16), rhs.astype(BF16))
    u = uw[:, :DN_DV]
    w = uw[:, DN_DV:]
    g_last = g_c[CHUNK - 1:CHUNK, :] if d == 0 else g_c[0:1, :]
    k_tail = kf * jnp.exp(g_last - g_c)
    q_head = q.astype(F32) * eg
    s = s_ref[d]
    wq = _dot(jnp.concatenate([w.astype(BF16), q_head.astype(BF16)], axis=0), s.astype(BF16))
    v_new = (u - wq[:CHUNK]).astype(BF16)
    o = wq[CHUNK:] + _dot(attn.astype(BF16), v_new)
    s_ref[d] = s * jnp.exp(g_last) + _dot(k_tail.T.astype(BF16), v_new)
    return o


def _dn_kernel(ql_ref, kl_ref, vl_ref, qc_ref, kc_ref, vc_ref, wq_ref, wk_ref, wv_ref, bg_ref, z_ref,
               gn_ref, o_ref, qs_ref, ks_ref, vs_ref, o0_ref, o1_ref, xpad_ref, s_ref):
    t = ql_ref.shape[0]
    tc = qc_ref.shape[0]

    def emit_qk(dst_ref, base, scale):
        def emit(r0, y):
            yn = y * lax.rsqrt(jnp.sum(y * y, axis=-1, keepdims=True) + EPS)
            dst_ref[pl.ds(base + r0, CHUNK), :] = (yn * scale).astype(BF16)
        return emit

    def emit_v(base):
        def emit(r0, y):
            vs_ref[pl.ds(base + r0, CHUNK), :] = y
        return emit

    _dn_conv_fill(qc_ref, wq_ref, xpad_ref, tc, emit_qk(qs_ref, 0, DN_DK ** -0.5))
    _dn_conv_fill(kc_ref, wk_ref, xpad_ref, tc, emit_qk(ks_ref, 0, 1.0))
    _dn_conv_fill(vc_ref, wv_ref, xpad_ref, tc, emit_v(0))
    _dn_conv_fill(ql_ref, wq_ref, xpad_ref, t, emit_qk(qs_ref, tc, DN_DK ** -0.5))
    _dn_conv_fill(kl_ref, wk_ref, xpad_ref, t, emit_qk(ks_ref, tc, 1.0))
    _dn_conv_fill(vl_ref, wv_ref, xpad_ref, t, emit_v(tc))

    s_ref[...] = jnp.zeros(s_ref.shape, F32)
    n_ctx = tc // CHUNK
    n_lat = t // CHUNK
    for i in range(n_ctx):
        _dn_chunk(i * CHUNK, 0, s_ref, qs_ref, ks_ref, vs_ref, bg_ref)
        _dn_chunk((n_ctx - 1 - i) * CHUNK, 1, s_ref, qs_ref, ks_ref, vs_ref, bg_ref)

    def step(i, carry):
        f0 = pl.multiple_of(i * CHUNK, CHUNK)
        b0 = pl.multiple_of((n_lat - 1 - i) * CHUNK, CHUNK)
        o0_ref[pl.ds(f0, CHUNK), :] = _dn_chunk(tc + f0, 0, s_ref, qs_ref, ks_ref, vs_ref, bg_ref)
        o1_ref[pl.ds(b0, CHUNK), :] = _dn_chunk(tc + b0, 1, s_ref, qs_ref, ks_ref, vs_ref, bg_ref)
        return carry

    lax.fori_loop(0, n_lat, step, 0)

    def gate(r, carry):
        r0 = pl.multiple_of(r * CHUNK, CHUNK)
        o = o0_ref[pl.ds(r0, CHUNK), :] + o1_ref[pl.ds(r0, CHUNK), :]
        zz = z_ref[pl.ds(r0, CHUNK), :].astype(F32)
        o_ref[pl.ds(r0, CHUNK), :] = (_rms(o, gn_ref[...]) * _silu(zz)).astype(BF16)
        return carry

    lax.fori_loop(0, n_lat, gate, 0)


def _dn_call(qkv_l, qkv_c, conv_w, bg_all, z_l, gn):
    bsz, _, t, _ = qkv_l.shape
    tc = qkv_c.shape[2]
    ta = t + tc

    def head(part, tt):
        return pl.BlockSpec((None, None, tt, LANES), lambda b, h: (b, part * DN_HEADS + h, 0, 0))

    def wspec(part):
        return pl.BlockSpec((None, DN_CONV, LANES), lambda b, h: (part * DN_HEADS + h, 0, 0))

    return pl.pallas_call(
        _dn_kernel,
        out_shape=jax.ShapeDtypeStruct((bsz, t, DN_WIDTH), BF16),
        grid=(bsz, DN_HEADS),
        in_specs=[head(0, t), head(1, t), head(2, t), head(0, tc), head(1, tc), head(2, tc),
                  wspec(0), wspec(1), wspec(2),
                  pl.BlockSpec((None, None, ta, 4), lambda b, h: (b, h, 0, 0)),
                  pl.BlockSpec((None, None, t, LANES), lambda b, h: (b, h, 0, 0)),
                  pl.BlockSpec((1, LANES), lambda b, h: (0, 0))],
        out_specs=pl.BlockSpec((None, t, LANES), lambda b, h: (b, 0, h)),
        scratch_shapes=[pltpu.VMEM((ta, LANES), BF16), pltpu.VMEM((ta, LANES), BF16),
                        pltpu.VMEM((ta, LANES), F32),
                        pltpu.VMEM((t, LANES), F32), pltpu.VMEM((t, LANES), F32),
                        pltpu.VMEM((t + 16, LANES), F32),
                        pltpu.VMEM((2, DN_DK, DN_DV), F32)],
        compiler_params=_params(2),
        name="deltanet",
    )(qkv_l, qkv_l, qkv_l, qkv_c, qkv_c, qkv_c, conv_w, conv_w, conv_w, bg_all, z_l, gn)


def _attn_kernel(q_ref, kl_ref, vl_ref, kc_ref, vc_ref, o_ref):
    tq = q_ref.shape[0]
    q = q_ref[...]
    q2 = jnp.concatenate([q[:, :AT_HD], q[:, AT_HD:]], axis=0)
    s_l = lax.dot_general(q2, kl_ref[...], _NT, preferred_element_type=F32)
    s_c = lax.dot_general(q2, kc_ref[...], _NT, preferred_element_type=F32)
    m = jnp.maximum(jnp.max(s_l, axis=-1, keepdims=True), jnp.max(s_c, axis=-1, keepdims=True))
    p_l = jnp.exp(s_l - m)
    p_c = jnp.exp(s_c - m)
    den = jnp.sum(p_l, axis=-1, keepdims=True) + jnp.sum(p_c, axis=-1, keepdims=True)
    o = (_dot(p_l.astype(BF16), vl_ref[...]) + _dot(p_c.astype(BF16), vc_ref[...])) / den
    o_ref[:, :AT_HD] = o[:tq].astype(BF16)
    o_ref[:, AT_HD:] = o[tq:].astype(BF16)


def _attn_call(aq, ak_l, av_l, ak_c, av_c, tq):
    bsz, hkv, t, _ = aq.shape
    tc = ak_c.shape[2]
    kv = lambda tt: pl.BlockSpec((None, None, tt, AT_HD), lambda b, j, i: (b, j, 0, 0))
    return pl.pallas_call(
        _attn_kernel,
        out_shape=jax.ShapeDtypeStruct((bsz, t, AT_WIDTH), BF16),
        grid=(bsz, hkv, t // tq),
        in_specs=[pl.BlockSpec((None, None, tq, 2 * AT_HD), lambda b, j, i: (b, j, i, 0)),
                  kv(t), kv(t), kv(tc), kv(tc)],
        out_specs=pl.BlockSpec((None, tq, 2 * AT_HD), lambda b, j, i: (b, i, j)),
        compiler_params=_params(3),
        name="attn",
    )(aq, ak_l, av_l, ak_c, av_c)


def _tail_kernel(x_ref, dn_ref, at_ref, mod_ref, wo_ref, g_ref, w1_ref, w3_ref, w2_ref, gf_ref, o_ref, act_ref):
    mix = _dot(dn_ref[...], wo_ref[:DN_WIDTH, :]) + _dot(at_ref[...], wo_ref[DN_WIDTH:, :])
    x = x_ref[...] + mod_ref[5:6, :] * mix
    x = _swiglu_update(x, mod_ref, 6, g_ref, w1_ref, w3_ref, w2_ref, act_ref)
    o_ref[...] = _rms(x, gf_ref[...])


def _tail_call(x2d, dn2d, at2d, mod3, mod_row, wo, g, w1, w3, w2, gf, tm):
    n, d = x2d.shape
    return pl.pallas_call(
        _tail_kernel,
        out_shape=jax.ShapeDtypeStruct((n, d), F32),
        grid=(n // tm,),
        in_specs=[pl.BlockSpec((tm, d), lambda i: (i, 0)),
                  pl.BlockSpec((tm, DN_WIDTH), lambda i: (i, 0)),
                  pl.BlockSpec((tm, AT_WIDTH), lambda i: (i, 0)),
                  pl.BlockSpec((None, N_MOD, d), lambda i: (mod_row(i), 0, 0)),
                  _const_spec(wo.shape), _const_spec((1, d)),
                  _const_spec(w1.shape), _const_spec(w3.shape), _const_spec(w2.shape),
                  _const_spec((1, d))],
        out_specs=pl.BlockSpec((tm, d), lambda i: (i, 0)),
        scratch_shapes=[pltpu.VMEM((tm, D_FF), BF16)],
        compiler_params=_params(1),
        name="tail",
    )(x2d, dn2d, at2d, mod3, wo, g, w1, w3, w2, gf)


def _rope_tables(n):
    rows = n // GRID_W
    row = jnp.repeat(jnp.arange(rows, dtype=jnp.int32), GRID_W).astype(F32)
    col = jnp.tile(jnp.arange(GRID_W, dtype=jnp.int32), rows).astype(F32)
    freqs = 1.0 / (ROPE_THETA ** (jnp.arange(0, ROPE_AXIS_DIM, 2, dtype=F32) / ROPE_AXIS_DIM))
    ang = jnp.concatenate([row[:, None] * freqs, col[:, None] * freqs], axis=-1)
    cos = jnp.repeat(jnp.cos(ang), 2, axis=-1)
    sin = jnp.repeat(jnp.sin(ang), 2, axis=-1)
    sign = jnp.tile(jnp.array([-1.0, 1.0], F32), AT_HD // 2)
    return cos, sin * sign


def _reorder_w_in(w_in):
    ba = w_in[:, OFF_DN_B:OFF_AT_Q]
    ba = jnp.pad(ba, ((0, 0), (0, LANES - ba.shape[1])))
    return jnp.concatenate([w_in[:, :OFF_DN_B], w_in[:, OFF_AT_Q:], ba], axis=1).astype(BF16)


def _lane_row(vec8, offset):
    return jnp.zeros((1, LANES), F32).at[0, offset:offset + vec8.shape[0]].set(vec8)


def _bg_heads(bg, bsz, t):
    return bg[:, :4 * DN_HEADS].reshape(bsz, t, 4, DN_HEADS).transpose(0, 3, 1, 2)


def kernel(x, c, ctx, c_ctx, w_mod, b_mod, g_ffn1, ffn1_w1, ffn1_w3, ffn1_w2, g_mix, w_in, dn_conv, dn_a_log,
           dn_dt_bias, dn_norm, q_norm, k_norm, w_out, g_ffn2, ffn2_w1, ffn2_w3, ffn2_w2, g_final):
    bsz, t, d = x.shape
    tc = ctx.shape[1]
    assert w_mod.shape[0] == 1, "single-layer block"
    tm = 512
    ctx_row = bsz

    cc = jnp.zeros((8, d), F32).at[:bsz].set(c).at[ctx_row].set(c_ctx)
    mod3 = _mod_call(cc, w_mod[0], b_mod[0][None]).reshape(8, N_MOD, d)

    lat_row = lambda i: i // (t // tm)
    ctx_rowf = lambda i: ctx_row
    row = lambda v: v.reshape(1, -1)

    w1a, w3a, w2a = ffn1_w1[0].astype(BF16), ffn1_w3[0].astype(BF16), ffn1_w2[0].astype(BF16)
    x1 = _ffn_call(x.reshape(bsz * t, d), mod3, lat_row, row(g_ffn1[0]), w1a, w3a, w2a, tm, "ffn1_lat")
    c1 = _ffn_call(ctx.reshape(bsz * tc, d), mod3, ctx_rowf, row(g_ffn1[0]), w1a, w3a, w2a, tc, "ffn1_ctx")

    w_r = _reorder_w_in(w_in[0])
    alog_row = _lane_row(dn_a_log[0].reshape(-1), 2 * DN_HEADS)
    dtb_row = _lane_row(dn_dt_bias[0].reshape(-1), 2 * DN_HEADS)
    cos, sin = _rope_tables(t)
    qn, kn = row(q_norm[0]), row(k_norm[0])
    qkv_l, z_l, bg_l, aq_l, ak_l, av_l = _inproj_call(
        x1, mod3, lat_row, row(g_mix[0]), w_r, alog_row, dtb_row, qn, kn, cos, sin, bsz, t, tm, True, "inproj_lat")
    qkv_c, _, bg_c, _, ak_c, av_c = _inproj_call(
        c1, mod3, ctx_rowf, row(g_mix[0]), w_r, alog_row, dtb_row, qn, kn, cos[:tc], sin[:tc], bsz, tc, tc, False,
        "inproj_ctx")

    bg_all = jnp.concatenate([_bg_heads(bg_c, bsz, tc), _bg_heads(bg_l, bsz, t)], axis=2)
    conv_w = dn_conv[0].reshape(DN_CONV, 3 * DN_HEADS, LANES).transpose(1, 0, 2)
    dn_lat = _dn_call(qkv_l, qkv_c, conv_w, bg_all, z_l, row(dn_norm[0]))
    at_lat = _attn_call(aq_l, ak_l, av_l, ak_c, av_c, 128)

    out = _tail_call(x1, dn_lat.reshape(bsz * t, DN_WIDTH), at_lat.reshape(bsz * t, AT_WIDTH), mod3, lat_row,
                     w_out[0].astype(BF16), row(g_ffn2[0]), ffn2_w1[0].astype(BF16), ffn2_w3[0].astype(BF16),
                     ffn2_w2[0].astype(BF16), row(g_final), tm)
    return out.reshape(bsz, t, d)
```

```python
import functools

import numpy as np
import jax
import jax.numpy as jnp
from jax import lax
from jax.experimental import pallas as pl
from jax.experimental.pallas import tpu as pltpu

F32 = jnp.float32
BF16 = jnp.bfloat16

D_MODEL = 1024
CTX_LEN = 256
GRID_W = 64
EPS = 1e-6
N_MOD = 9
D_FF = 2816
DN_HEADS = 4
DN_DK = 128
DN_DV = 128
DN_WIDTH = DN_HEADS * DN_DV
DN_CONV = 5
AT_HEADS = 4
AT_KV_HEADS = 2
AT_HD = 128
AT_WIDTH = AT_HEADS * AT_HD
ATT_SCALE = AT_HD ** -0.5
ROPE_AXIS_DIM = AT_HD // 2
ROPE_THETA = 10000.0
LEN_DN_QKV = 3 * DN_WIDTH
OFF_DN_Z = LEN_DN_QKV
OFF_DN_B = OFF_DN_Z + DN_WIDTH
OFF_DN_A = OFF_DN_B + 2 * DN_HEADS
OFF_AT_Q = OFF_DN_A + 2 * DN_HEADS
OFF_AT_K = OFF_AT_Q + AT_WIDTH
OFF_AT_V = OFF_AT_K + AT_KV_HEADS * AT_HD
P_IN = OFF_AT_V + AT_KV_HEADS * AT_HD

LANES = 128
CHUNK = 128
FF_TILE = 256
DN_GROUP = 4
VMEM_LIMIT = 56 * 1024 * 1024
NEG_BIG = -1e30

_NT = (((1,), (1,)), ((), ()))


def _sigmoid(x):
    return 1.0 / (1.0 + jnp.exp(-x))


def _silu(x):
    return x * _sigmoid(x)


def _rms(x, gain):
    ms = jnp.mean(x * x, axis=-1, keepdims=True)
    return x * lax.rsqrt(ms + EPS) * gain


def _dot(a, b):
    return jnp.dot(a, b, preferred_element_type=F32)


def _const_spec(shape):
    nd = len(shape)
    return pl.BlockSpec(shape, lambda *_: (0,) * nd, pipeline_mode=pl.Buffered(1))


def _params(n_axes):
    return pltpu.CompilerParams(dimension_semantics=("arbitrary",) * n_axes,
                                vmem_limit_bytes=VMEM_LIMIT)


def _mod_kernel(c_ref, w_ref, b_ref, o_ref):
    s = _silu(c_ref[...]).astype(BF16)
    o_ref[...] = _dot(s, w_ref[...].astype(BF16)) + b_ref[...]


def _mod_call(cc, w_mod, b_mod):
    d = cc.shape[1]
    n = w_mod.shape[1]
    return pl.pallas_call(
        _mod_kernel,
        out_shape=jax.ShapeDtypeStruct((cc.shape[0], n), F32),
        grid=(n // d,),
        in_specs=[pl.BlockSpec(cc.shape, lambda j: (0, 0)),
                  pl.BlockSpec((d, d), lambda j: (0, j)),
                  pl.BlockSpec((1, d), lambda j: (0, j))],
        out_specs=pl.BlockSpec((cc.shape[0], d), lambda j: (0, j)),
        compiler_params=_params(1),
        name="mod",
    )(cc, w_mod, b_mod)


def _swiglu_update(x, mod_ref, j0, g_ref, w1_ref, w3_ref, w2_ref, act_ref):
    h = (_rms(x, g_ref[...]) * (1.0 + mod_ref[j0 + 1:j0 + 2, :]) + mod_ref[j0:j0 + 1, :]).astype(BF16)
    for c in range(D_FF // FF_TILE):
        sl = slice(c * FF_TILE, (c + 1) * FF_TILE)
        a = _dot(h, w1_ref[:, sl])
        b = _dot(h, w3_ref[:, sl])
        act_ref[:, sl] = (_silu(a) * b).astype(BF16)
    y = _dot(act_ref[...], w2_ref[...])
    return x + (0.5 * mod_ref[j0 + 2:j0 + 3, :]) * y


def _ffn_kernel(x_ref, mod_ref, g_ref, w1_ref, w3_ref, w2_ref, o_ref, act_ref):
    o_ref[...] = _swiglu_update(x_ref[...], mod_ref, 0, g_ref, w1_ref, w3_ref, w2_ref, act_ref)


def _ffn_call(x2d, mod3, mod_row, g, w1, w3, w2, tm, name):
    n, d = x2d.shape
    return pl.pallas_call(
        _ffn_kernel,
        out_shape=jax.ShapeDtypeStruct((n, d), F32),
        grid=(n // tm,),
        in_specs=[pl.BlockSpec((tm, d), lambda i: (i, 0)),
                  pl.BlockSpec((None, N_MOD, d), lambda i: (mod_row(i), 0, 0)),
                  _const_spec((1, d)),
                  _const_spec(w1.shape), _const_spec(w3.shape), _const_spec(w2.shape)],
        out_specs=pl.BlockSpec((tm, d), lambda i: (i, 0)),
        scratch_shapes=[pltpu.VMEM((tm, D_FF), BF16)],
        compiler_params=_params(1),
        name=name,
    )(x2d, mod3, g, w1, w3, w2)


def _rope(x, cos, sin_signed, even):
    nxt = pltpu.roll(x, LANES - 1, axis=1)
    prv = pltpu.roll(x, 1, axis=1)
    return x * cos + jnp.where(even, nxt, prv) * sin_signed


def _inproj_kernel(x_ref, mod_ref, g_ref, w_ref, alog_ref, dtb_ref, qn_ref, kn_ref, cos_ref, sin_ref,
                   qkv_ref, z_ref, bg_ref, aq_ref, ak_ref, av_ref, *, rope):
    tm = x_ref.shape[0]
    x = x_ref[...]
    h = (_rms(x, g_ref[...]) * (1.0 + mod_ref[4:5, :]) + mod_ref[3:4, :]).astype(BF16)

    for part in range(3):
        p = _dot(h, w_ref[:, part * DN_WIDTH:(part + 1) * DN_WIDTH])
        for hd in range(DN_HEADS):
            qkv_ref[part * DN_HEADS + hd] = p[:, hd * LANES:(hd + 1) * LANES].astype(BF16)
    z_ref[...] = _dot(h, w_ref[:, LEN_DN_QKV:LEN_DN_QKV + DN_WIDTH]).astype(BF16)

    lane = lax.broadcasted_iota(jnp.int32, (tm, LANES), 1)
    even = (lane % 2) == 0
    if rope:
        cos = cos_ref[...]
        sin = sin_ref[...]

    off = LEN_DN_QKV + DN_WIDTH
    p = _dot(h, w_ref[:, off:off + AT_WIDTH])
    for hd in range(AT_HEADS):
        q = _rms(p[:, hd * LANES:(hd + 1) * LANES], qn_ref[...])
        if rope:
            q = _rope(q, cos, sin, even)
        aq_ref[hd // 2, :, (hd % 2) * LANES:(hd % 2 + 1) * LANES] = (q * ATT_SCALE).astype(BF16)
    off += AT_WIDTH
    p = _dot(h, w_ref[:, off:off + 2 * AT_KV_HEADS * AT_HD])
    for hd in range(AT_KV_HEADS):
        k = _rms(p[:, hd * LANES:(hd + 1) * LANES], kn_ref[...])
        if rope:
            k = _rope(k, cos, sin, even)
        ak_ref[hd] = k.astype(BF16)
        av_ref[hd] = p[:, (AT_KV_HEADS + hd) * LANES:(AT_KV_HEADS + hd + 1) * LANES].astype(BF16)

    off += 2 * AT_KV_HEADS * AT_HD
    p = _dot(h, w_ref[:, off:off + LANES])
    beta = _sigmoid(p)
    t = p + dtb_ref[...]
    softplus = jnp.maximum(t, 0.0) + jnp.log(1.0 + jnp.exp(-jnp.abs(t)))
    g = -jnp.exp(alog_ref[...]) * softplus
    ri = lax.broadcasted_iota(jnp.int32, (CHUNK, CHUNK), 0)
    ci = lax.broadcasted_iota(jnp.int32, (CHUNK, CHUNK), 1)
    ltri = jnp.where(ri >= ci, 1.0, 0.0).astype(BF16)
    lane_c = lax.broadcasted_iota(jnp.int32, (CHUNK, LANES), 1)
    for c in range(tm // CHUNK):
        rows = slice(c * CHUNK, (c + 1) * CHUNK)
        gc = g[rows]
        g1 = gc.astype(BF16)
        r1 = gc - g1.astype(F32)
        g2 = r1.astype(BF16)
        g3 = (r1 - g2.astype(F32)).astype(BF16)
        pre = _dot(ltri, g1) + _dot(ltri, g2) + _dot(ltri, g3)
        suf = pre[CHUNK - 1:CHUNK, :] - pre + gc
        gcs = jnp.where(lane_c < 8 + DN_HEADS, pre, suf)
        bg_ref[rows, :] = jnp.where(lane_c < 2 * DN_HEADS, beta[rows], gcs)


def _inproj_call(x2d, mod3, mod_row, g, w_r, alog_row, dtb_row, qn, kn, cos, sin, bsz, t, tm, rope, name):
    n, d = x2d.shape
    tpb = t // tm
    bt = lambda i: (i // tpb, 0, i % tpb, 0)
    out_shape = (
        jax.ShapeDtypeStruct((bsz, 3 * DN_HEADS, t, LANES), BF16),
        jax.ShapeDtypeStruct((n, DN_WIDTH), BF16),
        jax.ShapeDtypeStruct((n, LANES), F32),
        jax.ShapeDtypeStruct((bsz, AT_KV_HEADS, t, 2 * AT_HD), BF16),
        jax.ShapeDtypeStruct((bsz, AT_KV_HEADS, t, AT_HD), BF16),
        jax.ShapeDtypeStruct((bsz, AT_KV_HEADS, t, AT_HD), BF16),
    )
    out_specs = (
        pl.BlockSpec((None, 3 * DN_HEADS, tm, LANES), bt),
        pl.BlockSpec((tm, DN_WIDTH), lambda i: (i, 0)),
        pl.BlockSpec((tm, LANES), lambda i: (i, 0)),
        pl.BlockSpec((None, AT_KV_HEADS, tm, 2 * AT_HD), bt),
        pl.BlockSpec((None, AT_KV_HEADS, tm, AT_HD), bt),
        pl.BlockSpec((None, AT_KV_HEADS, tm, AT_HD), bt),
    )
    return pl.pallas_call(
        functools.partial(_inproj_kernel, rope=rope),
        out_shape=out_shape,
        grid=(n // tm,),
        in_specs=[pl.BlockSpec((tm, d), lambda i: (i, 0)),
                  pl.BlockSpec((None, N_MOD, d), lambda i: (mod_row(i), 0, 0)),
                  _const_spec((1, d)),
                  _const_spec(w_r.shape),
                  _const_spec((1, LANES)), _const_spec((1, LANES)),
                  _const_spec((1, LANES)), _const_spec((1, LANES)),
                  pl.BlockSpec((tm, LANES), lambda i: (i % tpb, 0)),
                  pl.BlockSpec((tm, LANES), lambda i: (i % tpb, 0))],
        out_specs=out_specs,
        compiler_params=_params(1),
        name=name,
    )(x2d, mod3, g, w_r, alog_row, dtb_row, qn, kn, cos, sin)


def _dn_conv_fill(src_ref, w_ref, xpad_ref, t, emit):
    nblk = t // CHUNK
    xpad_ref[0:8, :] = jnp.zeros((8, LANES), F32)
    xpad_ref[8 + t:16 + t, :] = jnp.zeros((8, LANES), F32)

    def fill(r, carry):
        r0 = pl.multiple_of(r * CHUNK, CHUNK)
        xpad_ref[pl.ds(r0 + 8, CHUNK), :] = src_ref[pl.ds(r0, CHUNK), :].astype(F32)
        return carry

    lax.fori_loop(0, nblk, fill, 0)
    w = w_ref[...]
    pad = (DN_CONV - 1) // 2
    win = CHUNK + 16

    def conv(r, carry):
        r0 = pl.multiple_of(r * CHUNK, CHUNK)
        val = xpad_ref[pl.ds(r0, win), :]
        acc = val * w[pad:pad + 1, :]
        for j in range(DN_CONV):
            if j != pad:
                acc = acc + pltpu.roll(val, (pad - j) % win, axis=0) * w[j:j + 1, :]
        y = acc[8:8 + CHUNK]
        emit(r0, _silu(y))
        return carry

    lax.fori_loop(0, nblk, conv, 0)


def _dn_prep_chunk(a, qs_ref, ks_ref, vs_ref, bg_ref, lvl_ref, bd_ref):
    row = pl.multiple_of(a * CHUNK, CHUNK)
    k = ks_ref[pl.ds(row, CHUNK), :]
    q = qs_ref[pl.ds(row, CHUNK), :]
    v = vs_ref[pl.ds(row, CHUNK), :]
    bg = bg_ref[pl.ds(row, CHUNK), :]
    kf = k.astype(F32)
    ri = lax.broadcasted_iota(jnp.int32, (CHUNK, CHUNK), 0)
    ci = lax.broadcasted_iota(jnp.int32, (CHUNK, CHUNK), 1)
    per_dir = []
    for d in range(2):
        beta = bg[:, d:d + 1]
        g_c = jnp.broadcast_to(bg[:, 2 + d:3 + d], (CHUNK, CHUNK))
        incl = (ri >= ci) if d == 0 else (ri <= ci)
        decay = jnp.exp(jnp.where(incl, g_c - g_c.T, NEG_BIG))
        per_dir.append((beta, g_c, decay, kf * beta))
    aq = lax.dot_general(jnp.concatenate([per_dir[0][3].astype(BF16), per_dir[1][3].astype(BF16), q], axis=0),
                         k, _NT, preferred_element_type=F32)
    yield
    qk = aq[2 * CHUNK:]
    a_cat = jnp.concatenate([jnp.where(ri > ci, aq[:CHUNK] * per_dir[0][2], 0.0),
                             jnp.where(ri < ci, aq[CHUNK:2 * CHUNK] * per_dir[1][2], 0.0)], axis=1)

    eye = jnp.where(ri == ci, 1.0, 0.0)
    eye_cat = jnp.concatenate([eye, eye], axis=1)
    pair = jnp.where((ri >> 1) == (ci >> 1), 1.0, 0.0)
    t_cat = eye_cat - a_cat * jnp.concatenate([pair, pair], axis=1)
    a_rows = jnp.concatenate([a_cat, a_cat], axis=0)
    bd = bd_ref[...]
    for lvl in range(lvl_ref.shape[0]):
        a_sub = (a_rows * lvl_ref[lvl]).astype(BF16)
        x = _dot(t_cat.astype(BF16), a_sub)
        yield
        t_bd = (jnp.concatenate([t_cat, t_cat], axis=0) * bd).astype(BF16)
        t_cat = t_cat - _dot(x.astype(BF16), t_bd)
        yield
    toff = t_cat - eye_cat

    results = []
    for d in range(2):
        beta, g_c, decay, kb = per_dir[d]
        eg = jnp.exp(g_c)
        rhs = jnp.concatenate([v * beta, kb * eg], axis=1)
        uw = rhs + _dot(toff[:, d * CHUNK:(d + 1) * CHUNK].astype(BF16), rhs.astype(BF16))
        g_last = g_c[CHUNK - 1:CHUNK, :] if d == 0 else g_c[0:1, :]
        k_tail = kf * jnp.exp(g_last - g_c)
        q_head = q.astype(F32) * eg
        results.append((uw[:, :DN_DV].astype(BF16),
                        jnp.concatenate([uw[:, DN_DV:].astype(BF16), q_head.astype(BF16)], axis=0),
                        jnp.concatenate([(qk * decay).astype(BF16), k_tail.T.astype(BF16)], axis=0),
                        jnp.broadcast_to(jnp.exp(g_last), (8, LANES))))
    return results


def _dn_prep_store(a, results, wq_ref, ak_ref, u_ref, eg_ref):
    for d in range(2):
        u_ref[d, a], wq_ref[d, a], ak_ref[d, a], eg_ref[d, a] = results[d]


def _dn_prep_kernel(ql_ref, kl_ref, vl_ref, qc_ref, kc_ref, vc_ref, wq_w_ref, wk_w_ref, wv_w_ref, bg_ref,
                    lvl_ref, bd_ref, wq_ref, ak_ref, u_ref, eg_ref, qs_ref, ks_ref, vs_ref, xpad_ref):
    t = ql_ref.shape[0]
    tc = qc_ref.shape[0]

    def emit_qk(dst_ref, base, scale):
        def emit(r0, y):
            yn = y * lax.rsqrt(jnp.sum(y * y, axis=-1, keepdims=True) + EPS)
            dst_ref[pl.ds(base + r0, CHUNK), :] = (yn * scale).astype(BF16)
        return emit

    def emit_v(base):
        def emit(r0, y):
            vs_ref[pl.ds(base + r0, CHUNK), :] = y
        return emit

    _dn_conv_fill(qc_ref, wq_w_ref, xpad_ref, tc, emit_qk(qs_ref, 0, DN_DK ** -0.5))
    _dn_conv_fill(kc_ref, wk_w_ref, xpad_ref, tc, emit_qk(ks_ref, 0, 1.0))
    _dn_conv_fill(vc_ref, wv_w_ref, xpad_ref, tc, emit_v(0))
    _dn_conv_fill(ql_ref, wq_w_ref, xpad_ref, t, emit_qk(qs_ref, tc, DN_DK ** -0.5))
    _dn_conv_fill(kl_ref, wk_w_ref, xpad_ref, t, emit_qk(ks_ref, tc, 1.0))
    _dn_conv_fill(vl_ref, wv_w_ref, xpad_ref, t, emit_v(tc))

    refs = (qs_ref, ks_ref, vs_ref, bg_ref, lvl_ref, bd_ref)
    outs = (wq_ref, ak_ref, u_ref, eg_ref)
    n_ctx = tc // CHUNK

    def run(chunks):
        gens = [_dn_prep_chunk(a, *refs) for a in chunks]
        results = [None] * len(gens)
        while any(r is None for r in results):
            for i, gen in enumerate(gens):
                if results[i] is None:
                    try:
                        next(gen)
                    except StopIteration as done:
                        results[i] = done.value
        for a, res in zip(chunks, results):
            _dn_prep_store(a, res, *outs)

    run(list(range(n_ctx)))

    def group(gi, carry):
        run([n_ctx + gi * DN_GROUP + j for j in range(DN_GROUP)])
        return carry

    lax.fori_loop(0, t // (CHUNK * DN_GROUP), group, 0)


def _dn_masks():
    i = np.arange(CHUNK)
    lvls = []
    b = 2
    while b < CHUNK:
        rb, cb = (i // b)[:, None], (i // b)[None, :]
        fwd = ((rb == cb + 1) & (rb % 2 == 1)).astype(np.float32)
        m = np.zeros((2 * CHUNK, 2 * CHUNK), np.float32)
        m[:CHUNK, :CHUNK] = fwd
        m[CHUNK:, CHUNK:] = fwd.T
        lvls.append(m)
        b *= 2
    bd = np.zeros((2 * CHUNK, 2 * CHUNK), np.float32)
    bd[:CHUNK, :CHUNK] = 1.0
    bd[CHUNK:, CHUNK:] = 1.0
    return jnp.asarray(np.stack(lvls)), jnp.asarray(bd)


def _dn_prep_call(qkv_l, qkv_c, conv_w, bg_all, lvl_masks, bd_mask):
    bsz, _, t, _ = qkv_l.shape
    tc = qkv_c.shape[2]
    ta = t + tc
    nc = ta // CHUNK

    def head(part, tt):
        return pl.BlockSpec((None, None, tt, LANES), lambda b, h: (b, part * DN_HEADS + h, 0, 0))

    def wspec(part):
        return pl.BlockSpec((None, DN_CONV, LANES), lambda b, h: (part * DN_HEADS + h, 0, 0))

    def out(rows, dt):
        return (jax.ShapeDtypeStruct((bsz, DN_HEADS, 2, nc, rows, LANES), dt),
                pl.BlockSpec((None, None, 2, nc, rows, LANES), lambda b, h: (b, h, 0, 0, 0, 0)))

    outs = [out(2 * CHUNK, BF16), out(2 * CHUNK, BF16), out(CHUNK, BF16), out(8, F32)]
    return pl.pallas_call(
        _dn_prep_kernel,
        out_shape=tuple(o[0] for o in outs),
        grid=(bsz, DN_HEADS),
        in_specs=[head(0, t), head(1, t), head(2, t), head(0, tc), head(1, tc), head(2, tc),
                  wspec(0), wspec(1), wspec(2),
                  pl.BlockSpec((None, None, ta, 4), lambda b, h: (b, h, 0, 0)),
                  _const_spec(lvl_masks.shape), _const_spec(bd_mask.shape)],
        out_specs=tuple(o[1] for o in outs),
        scratch_shapes=[pltpu.VMEM((ta, LANES), BF16), pltpu.VMEM((ta, LANES), BF16),
                        pltpu.VMEM((ta, LANES), F32), pltpu.VMEM((t + 16, LANES), F32)],
        compiler_params=_params(2),
        name="dn_prep",
    )(qkv_l, qkv_l, qkv_l, qkv_c, qkv_c, qkv_c, conv_w, conv_w, conv_w, bg_all, lvl_masks, bd_mask)


def _dn_scan_kernel(wq0_ref, wq1_ref, ak0_ref, ak1_ref, u0_ref, u1_ref, eg0_ref, eg1_ref, bd_ref,
                    o0_ref, o1_ref, s_ref):
    @pl.when(pl.program_id(1) == 0)
    def _():
        s_ref[...] = jnp.zeros(s_ref.shape, F32)

    bd = bd_ref[...]
    for h in range(DN_HEADS):
        s = s_ref[h]
        s_bd = (jnp.concatenate([s, s], axis=0) * bd).astype(BF16)
        r1 = _dot(jnp.concatenate([wq0_ref[h], wq1_ref[h]], axis=1), s_bd)
        u = jnp.concatenate([u0_ref[h], u1_ref[h]], axis=1).astype(F32)
        v_new = u - r1[:CHUNK]
        v_bd = (jnp.concatenate([v_new, v_new], axis=0) * bd).astype(BF16)
        r2 = _dot(jnp.concatenate([ak0_ref[h], ak1_ref[h]], axis=1), v_bd)
        o = r1[CHUNK:] + r2[:CHUNK]
        o0_ref[:, h * DN_DV:(h + 1) * DN_DV] = o[:, :DN_DV]
        o1_ref[:, h * DN_DV:(h + 1) * DN_DV] = o[:, DN_DV:]
        dec = jnp.concatenate([eg0_ref[h, 0:1, :], eg1_ref[h, 0:1, :]], axis=1)
        s_ref[h] = s * dec + r2[CHUNK:]


def _dn_scan_call(wq, ak, u, eg, bd_mask, t):
    bsz, nh, _, nc, _, _ = wq.shape
    n_ctx = nc - t // CHUNK

    def bwd_chunk(i):
        return jnp.where(i < n_ctx, n_ctx - 1 - i, nc - 1 + n_ctx - i)

    def spec(rows, d):
        if d == 0:
            return pl.BlockSpec((None, nh, None, None, rows, LANES), lambda b, i: (b, 0, 0, i, 0, 0))
        return pl.BlockSpec((None, nh, None, None, rows, LANES), lambda b, i: (b, 0, 1, bwd_chunk(i), 0, 0))

    o_shape = jax.ShapeDtypeStruct((bsz, t, nh * DN_DV), F32)
    o0_spec = pl.BlockSpec((None, CHUNK, nh * DN_DV), lambda b, i: (b, jnp.maximum(i - n_ctx, 0), 0))
    o1_spec = pl.BlockSpec((None, CHUNK, nh * DN_DV),
                           lambda b, i: (b, bwd_chunk(jnp.maximum(i, n_ctx)) - n_ctx, 0))
    return pl.pallas_call(
        _dn_scan_kernel,
        out_shape=(o_shape, o_shape),
        grid=(bsz, nc),
        in_specs=[spec(2 * CHUNK, 0), spec(2 * CHUNK, 1), spec(2 * CHUNK, 0), spec(2 * CHUNK, 1),
                  spec(CHUNK, 0), spec(CHUNK, 1), spec(8, 0), spec(8, 1), _const_spec(bd_mask.shape)],
        out_specs=(o0_spec, o1_spec),
        scratch_shapes=[pltpu.VMEM((nh, DN_DK, 2 * DN_DV), F32)],
        compiler_params=_params(2),
        name="dn_scan",
    )(wq, wq, ak, ak, u, u, eg, eg, bd_mask)


def _attn_kernel(q_ref, kl_ref, vl_ref, kc_ref, vc_ref, o_ref):
    tq = q_ref.shape[0]
    q = q_ref[...]
    q2 = jnp.concatenate([q[:, :AT_HD], q[:, AT_HD:]], axis=0)
    s_l = lax.dot_general(q2, kl_ref[...], _NT, preferred_element_type=F32)
    s_c = lax.dot_general(q2, kc_ref[...], _NT, preferred_element_type=F32)
    m = jnp.maximum(jnp.max(s_l, axis=-1, keepdims=True), jnp.max(s_c, axis=-1, keepdims=True))
    p_l = jnp.exp(s_l - m)
    p_c = jnp.exp(s_c - m)
    den = jnp.sum(p_l, axis=-1, keepdims=True) + jnp.sum(p_c, axis=-1, keepdims=True)
    o = (_dot(p_l.astype(BF16), vl_ref[...]) + _dot(p_c.astype(BF16), vc_ref[...])) / den
    o_ref[:, :AT_HD] = o[:tq].astype(BF16)
    o_ref[:, AT_HD:] = o[tq:].astype(BF16)


def _attn_call(aq, ak_l, av_l, ak_c, av_c, tq):
    bsz, hkv, t, _ = aq.shape
    tc = ak_c.shape[2]
    kv = lambda tt: pl.BlockSpec((None, None, tt, AT_HD), lambda b, j, i: (b, j, 0, 0))
    return pl.pallas_call(
        _attn_kernel,
        out_shape=jax.ShapeDtypeStruct((bsz, t, AT_WIDTH), BF16),
        grid=(bsz, hkv, t // tq),
        in_specs=[pl.BlockSpec((None, None, tq, 2 * AT_HD), lambda b, j, i: (b, j, i, 0)),
                  kv(t), kv(t), kv(tc), kv(tc)],
        out_specs=pl.BlockSpec((None, tq, 2 * AT_HD), lambda b, j, i: (b, i, j)),
        compiler_params=_params(3),
        name="attn",
    )(aq, ak_l, av_l, ak_c, av_c)


def _tail_kernel(x_ref, o0_ref, o1_ref, z_ref, at_ref, mod_ref, gn_ref, wo_ref, g_ref, w1_ref, w3_ref, w2_ref,
                 gf_ref, o_ref, act_ref, dn_ref):
    for hd in range(DN_HEADS):
        sl = slice(hd * DN_DV, (hd + 1) * DN_DV)
        o = o0_ref[:, sl] + o1_ref[:, sl]
        dn_ref[:, sl] = (_rms(o, gn_ref[...]) * _silu(z_ref[:, sl].astype(F32))).astype(BF16)
    mix = _dot(dn_ref[...], wo_ref[:DN_WIDTH, :]) + _dot(at_ref[...], wo_ref[DN_WIDTH:, :])
    x = x_ref[...] + mod_ref[5:6, :] * mix
    x = _swiglu_update(x, mod_ref, 6, g_ref, w1_ref, w3_ref, w2_ref, act_ref)
    o_ref[...] = _rms(x, gf_ref[...])


def _tail_call(x2d, o0, o1, z2d, at2d, mod3, mod_row, gn, wo, g, w1, w3, w2, gf, tm):
    n, d = x2d.shape
    return pl.pallas_call(
        _tail_kernel,
        out_shape=jax.ShapeDtypeStruct((n, d), F32),
        grid=(n // tm,),
        in_specs=[pl.BlockSpec((tm, d), lambda i: (i, 0)),
                  pl.BlockSpec((tm, DN_WIDTH), lambda i: (i, 0)),
                  pl.BlockSpec((tm, DN_WIDTH), lambda i: (i, 0)),
                  pl.BlockSpec((tm, DN_WIDTH), lambda i: (i, 0)),
                  pl.BlockSpec((tm, AT_WIDTH), lambda i: (i, 0)),
                  pl.BlockSpec((None, N_MOD, d), lambda i: (mod_row(i), 0, 0)),
                  _const_spec((1, LANES)),
                  _const_spec(wo.shape), _const_spec((1, d)),
                  _const_spec(w1.shape), _const_spec(w3.shape), _const_spec(w2.shape),
                  _const_spec((1, d))],
        out_specs=pl.BlockSpec((tm, d), lambda i: (i, 0)),
        scratch_shapes=[pltpu.VMEM((tm, D_FF), BF16), pltpu.VMEM((tm, DN_WIDTH), BF16)],
        compiler_params=_params(1),
        name="tail",
    )(x2d, o0, o1, z2d, at2d, mod3, gn, wo, g, w1, w3, w2, gf)


def _rope_tables(n):
    rows = n // GRID_W
    row = jnp.repeat(jnp.arange(rows, dtype=jnp.int32), GRID_W).astype(F32)
    col = jnp.tile(jnp.arange(GRID_W, dtype=jnp.int32), rows).astype(F32)
    freqs = 1.0 / (ROPE_THETA ** (jnp.arange(0, ROPE_AXIS_DIM, 2, dtype=F32) / ROPE_AXIS_DIM))
    ang = jnp.concatenate([row[:, None] * freqs, col[:, None] * freqs], axis=-1)
    cos = jnp.repeat(jnp.cos(ang), 2, axis=-1)
    sin = jnp.repeat(jnp.sin(ang), 2, axis=-1)
    sign = jnp.tile(jnp.array([-1.0, 1.0], F32), AT_HD // 2)
    return cos, sin * sign


def _reorder_w_in(w_in):
    ba = w_in[:, OFF_DN_B:OFF_AT_Q]
    ba = jnp.pad(ba, ((0, 0), (0, LANES - ba.shape[1])))
    return jnp.concatenate([w_in[:, :OFF_DN_B], w_in[:, OFF_AT_Q:], ba], axis=1).astype(BF16)


def _lane_row(vec8, offset):
    return jnp.zeros((1, LANES), F32).at[0, offset:offset + vec8.shape[0]].set(vec8)


def _bg_heads(bg, bsz, t):
    return bg[:, :4 * DN_HEADS].reshape(bsz, t, 4, DN_HEADS).transpose(0, 3, 1, 2)


def kernel(x, c, ctx, c_ctx, w_mod, b_mod, g_ffn1, ffn1_w1, ffn1_w3, ffn1_w2, g_mix, w_in, dn_conv, dn_a_log,
           dn_dt_bias, dn_norm, q_norm, k_norm, w_out, g_ffn2, ffn2_w1, ffn2_w3, ffn2_w2, g_final):
    bsz, t, d = x.shape
    tc = ctx.shape[1]
    assert w_mod.shape[0] == 1, "single-layer block"
    tm = 512
    ctx_row = bsz

    cc = jnp.zeros((8, d), F32).at[:bsz].set(c).at[ctx_row].set(c_ctx)
    mod3 = _mod_call(cc, w_mod[0], b_mod[0][None]).reshape(8, N_MOD, d)

    lat_row = lambda i: i // (t // tm)
    ctx_rowf = lambda i: ctx_row
    row = lambda v: v.reshape(1, -1)

    w1a, w3a, w2a = ffn1_w1[0].astype(BF16), ffn1_w3[0].astype(BF16), ffn1_w2[0].astype(BF16)
    x1 = _ffn_call(x.reshape(bsz * t, d), mod3, lat_row, row(g_ffn1[0]), w1a, w3a, w2a, tm, "ffn1_lat")
    c1 = _ffn_call(ctx.reshape(bsz * tc, d), mod3, ctx_rowf, row(g_ffn1[0]), w1a, w3a, w2a, tc, "ffn1_ctx")

    w_r = _reorder_w_in(w_in[0])
    alog_row = _lane_row(dn_a_log[0].reshape(-1), 2 * DN_HEADS)
    dtb_row = _lane_row(dn_dt_bias[0].reshape(-1), 2 * DN_HEADS)
    cos, sin = _rope_tables(t)
    qn, kn = row(q_norm[0]), row(k_norm[0])
    qkv_l, z_l, bg_l, aq_l, ak_l, av_l = _inproj_call(
        x1, mod3, lat_row, row(g_mix[0]), w_r, alog_row, dtb_row, qn, kn, cos, sin, bsz, t, tm, True, "inproj_lat")
    qkv_c, _, bg_c, _, ak_c, av_c = _inproj_call(
        c1, mod3, ctx_rowf, row(g_mix[0]), w_r, alog_row, dtb_row, qn, kn, cos[:tc], sin[:tc], bsz, tc, tc, False,
        "inproj_ctx")

    bg_all = jnp.concatenate([_bg_heads(bg_c, bsz, tc), _bg_heads(bg_l, bsz, t)], axis=2)
    conv_w = dn_conv[0].reshape(DN_CONV, 3 * DN_HEADS, LANES).transpose(1, 0, 2)
    lvl_masks, bd_mask = _dn_masks()
    dn_wq, dn_ak, dn_u, dn_eg = _dn_prep_call(qkv_l, qkv_c, conv_w, bg_all, lvl_masks, bd_mask)
    o_fwd, o_bwd = _dn_scan_call(dn_wq, dn_ak, dn_u, dn_eg, bd_mask, t)
    at_lat = _attn_call(aq_l, ak_l, av_l, ak_c, av_c, 128)

    out = _tail_call(x1, o_fwd.reshape(bsz * t, DN_WIDTH), o_bwd.reshape(bsz * t, DN_WIDTH), z_l,
                     at_lat.reshape(bsz * t, AT_WIDTH), mod3, lat_row, row(dn_norm[0]), w_out[0].astype(BF16),
                     row(g_ffn2[0]), ffn2_w1[0].astype(BF16), ffn2_w3[0].astype(BF16), ffn2_w2[0].astype(BF16),
                     row(g_final), tm)
    return out.reshape(bsz, t, d)
```

```python
import functools

import numpy as np
import jax
import jax.numpy as jnp
from jax import lax
from jax.experimental import pallas as pl
from jax.experimental.pallas import tpu as pltpu

F32 = jnp.float32
BF16 = jnp.bfloat16

D_MODEL = 1024
CTX_LEN = 256
GRID_W = 64
EPS = 1e-6
N_MOD = 9
D_FF = 2816
DN_HEADS = 4
DN_DK = 128
DN_DV = 128
DN_WIDTH = DN_HEADS * DN_DV
DN_CONV = 5
AT_HEADS = 4
AT_KV_HEADS = 2
AT_HD = 128
AT_WIDTH = AT_HEADS * AT_HD
ATT_SCALE = AT_HD ** -0.5
ROPE_AXIS_DIM = AT_HD // 2
ROPE_THETA = 10000.0
LEN_DN_QKV = 3 * DN_WIDTH
OFF_DN_Z = LEN_DN_QKV
OFF_DN_B = OFF_DN_Z + DN_WIDTH
OFF_DN_A = OFF_DN_B + 2 * DN_HEADS
OFF_AT_Q = OFF_DN_A + 2 * DN_HEADS
OFF_AT_K = OFF_AT_Q + AT_WIDTH
OFF_AT_V = OFF_AT_K + AT_KV_HEADS * AT_HD
P_IN = OFF_AT_V + AT_KV_HEADS * AT_HD

LANES = 128
CHUNK = 128
FF_TILE = 256
DN_GROUP = 4
AT_KEYS = 512
LOG2E = 1.4426950408889634
AT_SAFE_LOG2 = 60.0
VMEM_LIMIT = 56 * 1024 * 1024
NEG_BIG = -1e30

_NT = (((1,), (1,)), ((), ()))


def _sigmoid(x):
    return 1.0 / (1.0 + jnp.exp(-x))


def _silu(x):
    return x * _sigmoid(x)


def _rms(x, gain):
    ms = jnp.mean(x * x, axis=-1, keepdims=True)
    return x * lax.rsqrt(ms + EPS) * gain


def _dot(a, b):
    return jnp.dot(a, b, preferred_element_type=F32)


def _const_spec(shape):
    nd = len(shape)
    return pl.BlockSpec(shape, lambda *_: (0,) * nd, pipeline_mode=pl.Buffered(1))


def _params(n_axes):
    return pltpu.CompilerParams(dimension_semantics=("arbitrary",) * n_axes,
                                vmem_limit_bytes=VMEM_LIMIT)


def _mod_kernel(c_ref, w_ref, b_ref, o_ref):
    s = _silu(c_ref[...]).astype(BF16)
    o_ref[...] = _dot(s, w_ref[...].astype(BF16)) + b_ref[...]


def _mod_call(cc, w_mod, b_mod):
    d = cc.shape[1]
    n = w_mod.shape[1]
    return pl.pallas_call(
        _mod_kernel,
        out_shape=jax.ShapeDtypeStruct((cc.shape[0], n), F32),
        grid=(n // d,),
        in_specs=[pl.BlockSpec(cc.shape, lambda j: (0, 0)),
                  pl.BlockSpec((d, d), lambda j: (0, j)),
                  pl.BlockSpec((1, d), lambda j: (0, j))],
        out_specs=pl.BlockSpec((cc.shape[0], d), lambda j: (0, j)),
        compiler_params=_params(1),
        name="mod",
    )(cc, w_mod, b_mod)


def _swiglu_update(x, mod_ref, j0, g_ref, w1_ref, w3_ref, w2_ref, act_ref):
    h = (_rms(x, g_ref[...]) * (1.0 + mod_ref[j0 + 1:j0 + 2, :]) + mod_ref[j0:j0 + 1, :]).astype(BF16)
    for c in range(D_FF // FF_TILE):
        sl = slice(c * FF_TILE, (c + 1) * FF_TILE)
        a = _dot(h, w1_ref[:, sl])
        b = _dot(h, w3_ref[:, sl])
        act_ref[:, sl] = (_silu(a) * b).astype(BF16)
    y = _dot(act_ref[...], w2_ref[...])
    return x + (0.5 * mod_ref[j0 + 2:j0 + 3, :]) * y


def _ffn_kernel(x_ref, mod_ref, g_ref, w1_ref, w3_ref, w2_ref, o_ref, act_ref):
    o_ref[...] = _swiglu_update(x_ref[...], mod_ref, 0, g_ref, w1_ref, w3_ref, w2_ref, act_ref)


def _ffn_call(x2d, mod3, mod_row, g, w1, w3, w2, tm, name):
    n, d = x2d.shape
    return pl.pallas_call(
        _ffn_kernel,
        out_shape=jax.ShapeDtypeStruct((n, d), F32),
        grid=(n // tm,),
        in_specs=[pl.BlockSpec((tm, d), lambda i: (i, 0)),
                  pl.BlockSpec((None, N_MOD, d), lambda i: (mod_row(i), 0, 0)),
                  _const_spec((1, d)),
                  _const_spec(w1.shape), _const_spec(w3.shape), _const_spec(w2.shape)],
        out_specs=pl.BlockSpec((tm, d), lambda i: (i, 0)),
        scratch_shapes=[pltpu.VMEM((tm, D_FF), BF16)],
        compiler_params=_params(1),
        name=name,
    )(x2d, mod3, g, w1, w3, w2)


def _rope(x, cos, sin_signed, even):
    nxt = pltpu.roll(x, LANES - 1, axis=1)
    prv = pltpu.roll(x, 1, axis=1)
    return x * cos + jnp.where(even, nxt, prv) * sin_signed


def _inproj_kernel(x_ref, mod_ref, g_ref, w_ref, alog_ref, dtb_ref, qn_ref, kn_ref, cos_ref, sin_ref,
                   qkv_ref, z_ref, bg_ref, aq_ref, ak_ref, av_ref, *, rope):
    tm = x_ref.shape[0]
    x = x_ref[...]
    h = (_rms(x, g_ref[...]) * (1.0 + mod_ref[4:5, :]) + mod_ref[3:4, :]).astype(BF16)

    for part in range(3):
        p = _dot(h, w_ref[:, part * DN_WIDTH:(part + 1) * DN_WIDTH])
        for hd in range(DN_HEADS):
            qkv_ref[part * DN_HEADS + hd] = p[:, hd * LANES:(hd + 1) * LANES].astype(BF16)
    z_ref[...] = _dot(h, w_ref[:, LEN_DN_QKV:LEN_DN_QKV + DN_WIDTH]).astype(BF16)

    lane = lax.broadcasted_iota(jnp.int32, (tm, LANES), 1)
    even = (lane % 2) == 0
    if rope:
        cos = cos_ref[...]
        sin = sin_ref[...]

    off = LEN_DN_QKV + DN_WIDTH
    p = _dot(h, w_ref[:, off:off + AT_WIDTH])
    for hd in range(AT_HEADS):
        q = _rms(p[:, hd * LANES:(hd + 1) * LANES], qn_ref[...])
        if rope:
            q = _rope(q, cos, sin, even)
        aq_ref[hd // 2, :, (hd % 2) * LANES:(hd % 2 + 1) * LANES] = (q * (ATT_SCALE * LOG2E)).astype(BF16)
    off += AT_WIDTH
    p = _dot(h, w_ref[:, off:off + 2 * AT_KV_HEADS * AT_HD])
    for hd in range(AT_KV_HEADS):
        k = _rms(p[:, hd * LANES:(hd + 1) * LANES], kn_ref[...])
        if rope:
            k = _rope(k, cos, sin, even)
        ak_ref[hd] = k.astype(BF16)
        av_ref[hd] = p[:, (AT_KV_HEADS + hd) * LANES:(AT_KV_HEADS + hd + 1) * LANES].T.astype(BF16)

    off += 2 * AT_KV_HEADS * AT_HD
    p = _dot(h, w_ref[:, off:off + LANES])
    beta = _sigmoid(p)
    t = p + dtb_ref[...]
    softplus = jnp.maximum(t, 0.0) + jnp.log(1.0 + jnp.exp(-jnp.abs(t)))
    g = -jnp.exp(alog_ref[...]) * softplus
    ri = lax.broadcasted_iota(jnp.int32, (CHUNK, CHUNK), 0)
    ci = lax.broadcasted_iota(jnp.int32, (CHUNK, CHUNK), 1)
    ltri = jnp.where(ri >= ci, 1.0, 0.0).astype(BF16)
    lane_c = lax.broadcasted_iota(jnp.int32, (CHUNK, LANES), 1)
    for c in range(tm // CHUNK):
        rows = slice(c * CHUNK, (c + 1) * CHUNK)
        gc = g[rows]
        g1 = gc.astype(BF16)
        r1 = gc - g1.astype(F32)
        g2 = r1.astype(BF16)
        g3 = (r1 - g2.astype(F32)).astype(BF16)
        pre = _dot(ltri, g1) + _dot(ltri, g2) + _dot(ltri, g3)
        suf = pre[CHUNK - 1:CHUNK, :] - pre + gc
        gcs = jnp.where(lane_c < 8 + DN_HEADS, pre, suf)
        bg_ref[rows, :] = jnp.where(lane_c < 2 * DN_HEADS, beta[rows], gcs)


def _inproj_call(x2d, mod3, mod_row, g, w_r, alog_row, dtb_row, qn, kn, cos, sin, bsz, t, tm, rope, name):
    n, d = x2d.shape
    tpb = t // tm
    bt = lambda i: (i // tpb, 0, i % tpb, 0)
    out_shape = (
        jax.ShapeDtypeStruct((bsz, 3 * DN_HEADS, t, LANES), BF16),
        jax.ShapeDtypeStruct((n, DN_WIDTH), BF16),
        jax.ShapeDtypeStruct((n, LANES), F32),
        jax.ShapeDtypeStruct((bsz, AT_KV_HEADS, t, 2 * AT_HD), BF16),
        jax.ShapeDtypeStruct((bsz, AT_KV_HEADS, t, AT_HD), BF16),
        jax.ShapeDtypeStruct((bsz, AT_KV_HEADS, AT_HD, t), BF16),
    )
    out_specs = (
        pl.BlockSpec((None, 3 * DN_HEADS, tm, LANES), bt),
        pl.BlockSpec((tm, DN_WIDTH), lambda i: (i, 0)),
        pl.BlockSpec((tm, LANES), lambda i: (i, 0)),
        pl.BlockSpec((None, AT_KV_HEADS, tm, 2 * AT_HD), bt),
        pl.BlockSpec((None, AT_KV_HEADS, tm, AT_HD), bt),
        pl.BlockSpec((None, AT_KV_HEADS, AT_HD, tm), lambda i: (i // tpb, 0, 0, i % tpb)),
    )
    return pl.pallas_call(
        functools.partial(_inproj_kernel, rope=rope),
        out_shape=out_shape,
        grid=(n // tm,),
        in_specs=[pl.BlockSpec((tm, d), lambda i: (i, 0)),
                  pl.BlockSpec((None, N_MOD, d), lambda i: (mod_row(i), 0, 0)),
                  _const_spec((1, d)),
                  _const_spec(w_r.shape),
                  _const_spec((1, LANES)), _const_spec((1, LANES)),
                  _const_spec((1, LANES)), _const_spec((1, LANES)),
                  pl.BlockSpec((tm, LANES), lambda i: (i % tpb, 0)),
                  pl.BlockSpec((tm, LANES), lambda i: (i % tpb, 0))],
        out_specs=out_specs,
        compiler_params=_params(1),
        name=name,
    )(x2d, mod3, g, w_r, alog_row, dtb_row, qn, kn, cos, sin)


def _dn_conv_fill(src_ref, w_ref, xpad_ref, t, emit):
    nblk = t // CHUNK
    xpad_ref[0:8, :] = jnp.zeros((8, LANES), F32)
    xpad_ref[8 + t:16 + t, :] = jnp.zeros((8, LANES), F32)

    def fill(r, carry):
        r0 = pl.multiple_of(r * CHUNK, CHUNK)
        xpad_ref[pl.ds(r0 + 8, CHUNK), :] = src_ref[pl.ds(r0, CHUNK), :].astype(F32)
        return carry

    lax.fori_loop(0, nblk, fill, 0)
    w = w_ref[...]
    pad = (DN_CONV - 1) // 2
    win = CHUNK + 16

    def conv(r, carry):
        r0 = pl.multiple_of(r * CHUNK, CHUNK)
        val = xpad_ref[pl.ds(r0, win), :]
        acc = val * w[pad:pad + 1, :]
        for j in range(DN_CONV):
            if j != pad:
                acc = acc + pltpu.roll(val, (pad - j) % win, axis=0) * w[j:j + 1, :]
        y = acc[8:8 + CHUNK]
        emit(r0, _silu(y))
        return carry

    lax.fori_loop(0, nblk, conv, 0)


def _dn_prep_chunk(a, qs_ref, ks_ref, vs_ref, bg_ref, lvl_ref, bd_ref):
    row = pl.multiple_of(a * CHUNK, CHUNK)
    k = ks_ref[pl.ds(row, CHUNK), :]
    q = qs_ref[pl.ds(row, CHUNK), :]
    v = vs_ref[pl.ds(row, CHUNK), :]
    bg = bg_ref[pl.ds(row, CHUNK), :]
    kf = k.astype(F32)
    ri = lax.broadcasted_iota(jnp.int32, (CHUNK, CHUNK), 0)
    ci = lax.broadcasted_iota(jnp.int32, (CHUNK, CHUNK), 1)
    per_dir = []
    for d in range(2):
        beta = bg[:, d:d + 1]
        g_c = jnp.broadcast_to(bg[:, 2 + d:3 + d], (CHUNK, CHUNK))
        incl = (ri >= ci) if d == 0 else (ri <= ci)
        decay = jnp.exp(jnp.where(incl, g_c - g_c.T, NEG_BIG))
        per_dir.append((beta, g_c, decay, kf * beta))
    aq = lax.dot_general(jnp.concatenate([per_dir[0][3].astype(BF16), per_dir[1][3].astype(BF16), q], axis=0),
                         k, _NT, preferred_element_type=F32)
    yield
    qk = aq[2 * CHUNK:]
    a_cat = jnp.concatenate([jnp.where(ri > ci, aq[:CHUNK] * per_dir[0][2], 0.0),
                             jnp.where(ri < ci, aq[CHUNK:2 * CHUNK] * per_dir[1][2], 0.0)], axis=1)

    eye = jnp.where(ri == ci, 1.0, 0.0)
    eye_cat = jnp.concatenate([eye, eye], axis=1)
    pair = jnp.where((ri >> 1) == (ci >> 1), 1.0, 0.0)
    t_cat = eye_cat - a_cat * jnp.concatenate([pair, pair], axis=1)
    a_rows = jnp.concatenate([a_cat, a_cat], axis=0)
    bd = bd_ref[...]
    for lvl in range(lvl_ref.shape[0]):
        a_sub = (a_rows * lvl_ref[lvl]).astype(BF16)
        x = _dot(t_cat.astype(BF16), a_sub)
        yield
        t_bd = (jnp.concatenate([t_cat, t_cat], axis=0) * bd).astype(BF16)
        t_cat = t_cat - _dot(x.astype(BF16), t_bd)
        yield
    toff = t_cat - eye_cat

    results = []
    for d in range(2):
        beta, g_c, decay, kb = per_dir[d]
        eg = jnp.exp(g_c)
        rhs = jnp.concatenate([v * beta, kb * eg], axis=1)
        uw = rhs + _dot(toff[:, d * CHUNK:(d + 1) * CHUNK].astype(BF16), rhs.astype(BF16))
        g_last = g_c[CHUNK - 1:CHUNK, :] if d == 0 else g_c[0:1, :]
        k_tail = kf * jnp.exp(g_last - g_c)
        q_head = q.astype(F32) * eg
        results.append((uw[:, :DN_DV].astype(BF16),
                        jnp.concatenate([uw[:, DN_DV:].astype(BF16), q_head.astype(BF16)], axis=0),
                        jnp.concatenate([(qk * decay).astype(BF16), k_tail.T.astype(BF16)], axis=0),
                        jnp.broadcast_to(jnp.exp(g_last), (8, LANES))))
    return results


def _dn_prep_store(a, results, wq_ref, ak_ref, u_ref, eg_ref):
    for d in range(2):
        u_ref[d, a], wq_ref[d, a], ak_ref[d, a], eg_ref[d, a] = results[d]


def _dn_prep_kernel(ql_ref, kl_ref, vl_ref, qc_ref, kc_ref, vc_ref, wq_w_ref, wk_w_ref, wv_w_ref, bg_ref,
                    lvl_ref, bd_ref, wq_ref, ak_ref, u_ref, eg_ref, qs_ref, ks_ref, vs_ref, xpad_ref):
    t = ql_ref.shape[0]
    tc = qc_ref.shape[0]

    def emit_qk(dst_ref, base, scale):
        def emit(r0, y):
            yn = y * lax.rsqrt(jnp.sum(y * y, axis=-1, keepdims=True) + EPS)
            dst_ref[pl.ds(base + r0, CHUNK), :] = (yn * scale).astype(BF16)
        return emit

    def emit_v(base):
        def emit(r0, y):
            vs_ref[pl.ds(base + r0, CHUNK), :] = y
        return emit

    _dn_conv_fill(qc_ref, wq_w_ref, xpad_ref, tc, emit_qk(qs_ref, 0, DN_DK ** -0.5))
    _dn_conv_fill(kc_ref, wk_w_ref, xpad_ref, tc, emit_qk(ks_ref, 0, 1.0))
    _dn_conv_fill(vc_ref, wv_w_ref, xpad_ref, tc, emit_v(0))
    _dn_conv_fill(ql_ref, wq_w_ref, xpad_ref, t, emit_qk(qs_ref, tc, DN_DK ** -0.5))
    _dn_conv_fill(kl_ref, wk_w_ref, xpad_ref, t, emit_qk(ks_ref, tc, 1.0))
    _dn_conv_fill(vl_ref, wv_w_ref, xpad_ref, t, emit_v(tc))

    refs = (qs_ref, ks_ref, vs_ref, bg_ref, lvl_ref, bd_ref)
    outs = (wq_ref, ak_ref, u_ref, eg_ref)
    n_ctx = tc // CHUNK

    def run(chunks):
        gens = [_dn_prep_chunk(a, *refs) for a in chunks]
        results = [None] * len(gens)
        while any(r is None for r in results):
            for i, gen in enumerate(gens):
                if results[i] is None:
                    try:
                        next(gen)
                    except StopIteration as done:
                        results[i] = done.value
        for a, res in zip(chunks, results):
            _dn_prep_store(a, res, *outs)

    run(list(range(n_ctx)))

    def group(gi, carry):
        run([n_ctx + gi * DN_GROUP + j for j in range(DN_GROUP)])
        return carry

    lax.fori_loop(0, t // (CHUNK * DN_GROUP), group, 0)


def _dn_masks():
    i = np.arange(CHUNK)
    lvls = []
    b = 2
    while b < CHUNK:
        rb, cb = (i // b)[:, None], (i // b)[None, :]
        fwd = ((rb == cb + 1) & (rb % 2 == 1)).astype(np.float32)
        m = np.zeros((2 * CHUNK, 2 * CHUNK), np.float32)
        m[:CHUNK, :CHUNK] = fwd
        m[CHUNK:, CHUNK:] = fwd.T
        lvls.append(m)
        b *= 2
    bd = np.zeros((2 * CHUNK, 2 * CHUNK), np.float32)
    bd[:CHUNK, :CHUNK] = 1.0
    bd[CHUNK:, CHUNK:] = 1.0
    return jnp.asarray(np.stack(lvls)), jnp.asarray(bd)


def _dn_prep_call(qkv_l, qkv_c, conv_w, bg_all, lvl_masks, bd_mask):
    bsz, _, t, _ = qkv_l.shape
    tc = qkv_c.shape[2]
    ta = t + tc
    nc = ta // CHUNK

    def head(part, tt):
        return pl.BlockSpec((None, None, tt, LANES), lambda b, h: (b, part * DN_HEADS + h, 0, 0))

    def wspec(part):
        return pl.BlockSpec((None, DN_CONV, LANES), lambda b, h: (part * DN_HEADS + h, 0, 0))

    def out(rows, dt):
        return (jax.ShapeDtypeStruct((bsz, DN_HEADS, 2, nc, rows, LANES), dt),
                pl.BlockSpec((None, None, 2, nc, rows, LANES), lambda b, h: (b, h, 0, 0, 0, 0)))

    outs = [out(2 * CHUNK, BF16), out(2 * CHUNK, BF16), out(CHUNK, BF16), out(8, F32)]
    return pl.pallas_call(
        _dn_prep_kernel,
        out_shape=tuple(o[0] for o in outs),
        grid=(bsz, DN_HEADS),
        in_specs=[head(0, t), head(1, t), head(2, t), head(0, tc), head(1, tc), head(2, tc),
                  wspec(0), wspec(1), wspec(2),
                  pl.BlockSpec((None, None, ta, 4), lambda b, h: (b, h, 0, 0)),
                  _const_spec(lvl_masks.shape), _const_spec(bd_mask.shape)],
        out_specs=tuple(o[1] for o in outs),
        scratch_shapes=[pltpu.VMEM((ta, LANES), BF16), pltpu.VMEM((ta, LANES), BF16),
                        pltpu.VMEM((ta, LANES), F32), pltpu.VMEM((t + 16, LANES), F32)],
        compiler_params=_params(2),
        name="dn_prep",
    )(qkv_l, qkv_l, qkv_l, qkv_c, qkv_c, qkv_c, conv_w, conv_w, conv_w, bg_all, lvl_masks, bd_mask)


def _dn_scan_kernel(wq0_ref, wq1_ref, ak0_ref, ak1_ref, u0_ref, u1_ref, eg0_ref, eg1_ref, bd_ref,
                    o0_ref, o1_ref, s_ref):
    @pl.when(pl.program_id(1) == 0)
    def _():
        s_ref[...] = jnp.zeros(s_ref.shape, F32)

    bd = bd_ref[...]
    for h in range(DN_HEADS):
        s = s_ref[h]
        s_bd = (jnp.concatenate([s, s], axis=0) * bd).astype(BF16)
        r1 = _dot(jnp.concatenate([wq0_ref[h], wq1_ref[h]], axis=1), s_bd)
        u = jnp.concatenate([u0_ref[h], u1_ref[h]], axis=1).astype(F32)
        v_new = u - r1[:CHUNK]
        v_bd = (jnp.concatenate([v_new, v_new], axis=0) * bd).astype(BF16)
        r2 = _dot(jnp.concatenate([ak0_ref[h], ak1_ref[h]], axis=1), v_bd)
        o = r1[CHUNK:] + r2[:CHUNK]
        o0_ref[:, h * DN_DV:(h + 1) * DN_DV] = o[:, :DN_DV]
        o1_ref[:, h * DN_DV:(h + 1) * DN_DV] = o[:, DN_DV:]
        dec = jnp.concatenate([eg0_ref[h, 0:1, :], eg1_ref[h, 0:1, :]], axis=1)
        s_ref[h] = s * dec + r2[CHUNK:]


def _dn_scan_call(wq, ak, u, eg, bd_mask, t):
    bsz, nh, _, nc, _, _ = wq.shape
    n_ctx = nc - t // CHUNK

    def bwd_chunk(i):
        return jnp.where(i < n_ctx, n_ctx - 1 - i, nc - 1 + n_ctx - i)

    def spec(rows, d):
        if d == 0:
            return pl.BlockSpec((None, nh, None, None, rows, LANES), lambda b, i: (b, 0, 0, i, 0, 0))
        return pl.BlockSpec((None, nh, None, None, rows, LANES), lambda b, i: (b, 0, 1, bwd_chunk(i), 0, 0))

    o_shape = jax.ShapeDtypeStruct((bsz, t, nh * DN_DV), F32)
    o0_spec = pl.BlockSpec((None, CHUNK, nh * DN_DV), lambda b, i: (b, jnp.maximum(i - n_ctx, 0), 0))
    o1_spec = pl.BlockSpec((None, CHUNK, nh * DN_DV),
                           lambda b, i: (b, bwd_chunk(jnp.maximum(i, n_ctx)) - n_ctx, 0))
    return pl.pallas_call(
        _dn_scan_kernel,
        out_shape=(o_shape, o_shape),
        grid=(bsz, nc),
        in_specs=[spec(2 * CHUNK, 0), spec(2 * CHUNK, 1), spec(2 * CHUNK, 0), spec(2 * CHUNK, 1),
                  spec(CHUNK, 0), spec(CHUNK, 1), spec(8, 0), spec(8, 1), _const_spec(bd_mask.shape)],
        out_specs=(o0_spec, o1_spec),
        scratch_shapes=[pltpu.VMEM((nh, DN_DK, 2 * DN_DV), F32)],
        compiler_params=_params(2),
        name="dn_scan",
    )(wq, wq, ak, ak, u, u, eg, eg, bd_mask)


def _attn_kernel(q_ref, kl_ref, vtl_ref, kc_ref, vtc_ref, o_ref, kmax_ref, shift_ref):
    tq = q_ref.shape[0]
    n = 2 * tq
    t = kl_ref.shape[0]
    blocks = [(kl_ref, vtl_ref, j * AT_KEYS, AT_KEYS) for j in range(t // AT_KEYS)]
    blocks.append((kc_ref, vtc_ref, 0, kc_ref.shape[0]))

    @pl.when(pl.program_id(2) == 0)
    def _():
        best = jnp.zeros((1, 1), F32)
        for k_ref, _, off, size in blocks:
            kk = k_ref[off:off + size, :].astype(F32)
            best = jnp.maximum(best, jnp.max(jnp.sum(kk * kk, axis=-1, keepdims=True), axis=0, keepdims=True))
        kmax_ref[...] = jnp.broadcast_to(best, kmax_ref.shape)

    q = q_ref[...].astype(F32)
    qt32 = jnp.concatenate([q[:, :AT_HD].T, q[:, AT_HD:].T], axis=1)
    qt = qt32.astype(BF16)
    bound = jnp.sqrt(jnp.sum(qt32 * qt32, axis=0, keepdims=True) * kmax_ref[0:1, 0:1])
    shift_ref[...] = bound

    def scores(blk):
        k_ref, _, off, size = blk
        return _dot(k_ref[off:off + size, :], qt)

    @pl.when(jnp.max(bound) > AT_SAFE_LOG2)
    def _():
        m = jnp.full((1, n), NEG_BIG, F32)
        for blk in blocks:
            m = jnp.maximum(m, jnp.max(scores(blk), axis=0, keepdims=True))
        shift_ref[...] = m

    shift = shift_ref[...]
    den = jnp.zeros((1, n), F32)
    acc = jnp.zeros((AT_HD, n), F32)
    s_next = scores(blocks[0])
    for j, (_, vt_ref, off, size) in enumerate(blocks):
        s = s_next
        if j + 1 < len(blocks):
            s_next = scores(blocks[j + 1])
        p = jnp.exp2(s - shift)
        den = den + jnp.sum(p, axis=0, keepdims=True)
        acc = acc + _dot(vt_ref[:, off:off + size], p.astype(BF16))
    o = (acc / den).T
    o_ref[:, :AT_HD] = o[:tq].astype(BF16)
    o_ref[:, AT_HD:] = o[tq:].astype(BF16)


def _attn_call(aq, ak_l, avt_l, ak_c, avt_c, tq):
    bsz, hkv, t, _ = aq.shape
    tc = ak_c.shape[2]
    kspec = lambda tt: pl.BlockSpec((None, None, tt, AT_HD), lambda b, j, i: (b, j, 0, 0))
    vspec = lambda tt: pl.BlockSpec((None, None, AT_HD, tt), lambda b, j, i: (b, j, 0, 0))
    return pl.pallas_call(
        _attn_kernel,
        out_shape=jax.ShapeDtypeStruct((bsz, t, AT_WIDTH), BF16),
        grid=(bsz, hkv, t // tq),
        in_specs=[pl.BlockSpec((None, None, tq, 2 * AT_HD), lambda b, j, i: (b, j, i, 0)),
                  kspec(t), vspec(t), kspec(tc), vspec(tc)],
        out_specs=pl.BlockSpec((None, tq, 2 * AT_HD), lambda b, j, i: (b, i, j)),
        scratch_shapes=[pltpu.VMEM((8, LANES), F32), pltpu.VMEM((1, 2 * tq), F32)],
        compiler_params=_params(3),
        name="attn",
    )(aq, ak_l, avt_l, ak_c, avt_c)


def _tail_kernel(x_ref, o0_ref, o1_ref, z_ref, at_ref, mod_ref, gn_ref, wo_ref, g_ref, w1_ref, w3_ref, w2_ref,
                 gf_ref, o_ref, act_ref, dn_ref):
    for hd in range(DN_HEADS):
        sl = slice(hd * DN_DV, (hd + 1) * DN_DV)
        o = o0_ref[:, sl] + o1_ref[:, sl]
        dn_ref[:, sl] = (_rms(o, gn_ref[...]) * _silu(z_ref[:, sl].astype(F32))).astype(BF16)
    mix = _dot(dn_ref[...], wo_ref[:DN_WIDTH, :]) + _dot(at_ref[...], wo_ref[DN_WIDTH:, :])
    x = x_ref[...] + mod_ref[5:6, :] * mix
    x = _swiglu_update(x, mod_ref, 6, g_ref, w1_ref, w3_ref, w2_ref, act_ref)
    o_ref[...] = _rms(x, gf_ref[...])


def _tail_call(x2d, o0, o1, z2d, at2d, mod3, mod_row, gn, wo, g, w1, w3, w2, gf, tm):
    n, d = x2d.shape
    return pl.pallas_call(
        _tail_kernel,
        out_shape=jax.ShapeDtypeStruct((n, d), F32),
        grid=(n // tm,),
        in_specs=[pl.BlockSpec((tm, d), lambda i: (i, 0)),
                  pl.BlockSpec((tm, DN_WIDTH), lambda i: (i, 0)),
                  pl.BlockSpec((tm, DN_WIDTH), lambda i: (i, 0)),
                  pl.BlockSpec((tm, DN_WIDTH), lambda i: (i, 0)),
                  pl.BlockSpec((tm, AT_WIDTH), lambda i: (i, 0)),
                  pl.BlockSpec((None, N_MOD, d), lambda i: (mod_row(i), 0, 0)),
                  _const_spec((1, LANES)),
                  _const_spec(wo.shape), _const_spec((1, d)),
                  _const_spec(w1.shape), _const_spec(w3.shape), _const_spec(w2.shape),
                  _const_spec((1, d))],
        out_specs=pl.BlockSpec((tm, d), lambda i: (i, 0)),
        scratch_shapes=[pltpu.VMEM((tm, D_FF), BF16), pltpu.VMEM((tm, DN_WIDTH), BF16)],
        compiler_params=_params(1),
        name="tail",
    )(x2d, o0, o1, z2d, at2d, mod3, gn, wo, g, w1, w3, w2, gf)


def _rope_tables(n):
    rows = n // GRID_W
    row = jnp.repeat(jnp.arange(rows, dtype=jnp.int32), GRID_W).astype(F32)
    col = jnp.tile(jnp.arange(GRID_W, dtype=jnp.int32), rows).astype(F32)
    freqs = 1.0 / (ROPE_THETA ** (jnp.arange(0, ROPE_AXIS_DIM, 2, dtype=F32) / ROPE_AXIS_DIM))
    ang = jnp.concatenate([row[:, None] * freqs, col[:, None] * freqs], axis=-1)
    cos = jnp.repeat(jnp.cos(ang), 2, axis=-1)
    sin = jnp.repeat(jnp.sin(ang), 2, axis=-1)
    sign = jnp.tile(jnp.array([-1.0, 1.0], F32), AT_HD // 2)
    return cos, sin * sign


def _reorder_w_in(w_in):
    ba = w_in[:, OFF_DN_B:OFF_AT_Q]
    ba = jnp.pad(ba, ((0, 0), (0, LANES - ba.shape[1])))
    return jnp.concatenate([w_in[:, :OFF_DN_B], w_in[:, OFF_AT_Q:], ba], axis=1).astype(BF16)


def _lane_row(vec8, offset):
    return jnp.zeros((1, LANES), F32).at[0, offset:offset + vec8.shape[0]].set(vec8)


def _bg_heads(bg, bsz, t):
    return bg[:, :4 * DN_HEADS].reshape(bsz, t, 4, DN_HEADS).transpose(0, 3, 1, 2)


def kernel(x, c, ctx, c_ctx, w_mod, b_mod, g_ffn1, ffn1_w1, ffn1_w3, ffn1_w2, g_mix, w_in, dn_conv, dn_a_log,
           dn_dt_bias, dn_norm, q_norm, k_norm, w_out, g_ffn2, ffn2_w1, ffn2_w3, ffn2_w2, g_final):
    bsz, t, d = x.shape
    tc = ctx.shape[1]
    assert w_mod.shape[0] == 1, "single-layer block"
    tm = 512
    ctx_row = bsz

    cc = jnp.zeros((8, d), F32).at[:bsz].set(c).at[ctx_row].set(c_ctx)
    mod3 = _mod_call(cc, w_mod[0], b_mod[0][None]).reshape(8, N_MOD, d)

    lat_row = lambda i: i // (t // tm)
    ctx_rowf = lambda i: ctx_row
    row = lambda v: v.reshape(1, -1)

    w1a, w3a, w2a = ffn1_w1[0].astype(BF16), ffn1_w3[0].astype(BF16), ffn1_w2[0].astype(BF16)
    x1 = _ffn_call(x.reshape(bsz * t, d), mod3, lat_row, row(g_ffn1[0]), w1a, w3a, w2a, tm, "ffn1_lat")
    c1 = _ffn_call(ctx.reshape(bsz * tc, d), mod3, ctx_rowf, row(g_ffn1[0]), w1a, w3a, w2a, tc, "ffn1_ctx")

    w_r = _reorder_w_in(w_in[0])
    alog_row = _lane_row(dn_a_log[0].reshape(-1), 2 * DN_HEADS)
    dtb_row = _lane_row(dn_dt_bias[0].reshape(-1), 2 * DN_HEADS)
    cos, sin = _rope_tables(t)
    qn, kn = row(q_norm[0]), row(k_norm[0])
    qkv_l, z_l, bg_l, aq_l, ak_l, av_l = _inproj_call(
        x1, mod3, lat_row, row(g_mix[0]), w_r, alog_row, dtb_row, qn, kn, cos, sin, bsz, t, tm, True, "inproj_lat")
    qkv_c, _, bg_c, _, ak_c, av_c = _inproj_call(
        c1, mod3, ctx_rowf, row(g_mix[0]), w_r, alog_row, dtb_row, qn, kn, cos[:tc], sin[:tc], bsz, tc, tc, False,
        "inproj_ctx")

    bg_all = jnp.concatenate([_bg_heads(bg_c, bsz, tc), _bg_heads(bg_l, bsz, t)], axis=2)
    conv_w = dn_conv[0].reshape(DN_CONV, 3 * DN_HEADS, LANES).transpose(1, 0, 2)
    lvl_masks, bd_mask = _dn_masks()
    dn_wq, dn_ak, dn_u, dn_eg = _dn_prep_call(qkv_l, qkv_c, conv_w, bg_all, lvl_masks, bd_mask)
    o_fwd, o_bwd = _dn_scan_call(dn_wq, dn_ak, dn_u, dn_eg, bd_mask, t)
    at_lat = _attn_call(aq_l, ak_l, av_l, ak_c, av_c, 512)

    out = _tail_call(x1, o_fwd.reshape(bsz * t, DN_WIDTH), o_bwd.reshape(bsz * t, DN_WIDTH), z_l,
                     at_lat.reshape(bsz * t, AT_WIDTH), mod3, lat_row, row(dn_norm[0]), w_out[0].astype(BF16),
                     row(g_ffn2[0]), ffn2_w1[0].astype(BF16), ffn2_w3[0].astype(BF16), ffn2_w2[0].astype(BF16),
                     row(g_final), tm)
    return out.reshape(bsz, t, d)
```

```python
import functools

import numpy as np
import jax
import jax.numpy as jnp
from jax import lax
from jax.experimental import pallas as pl
from jax.experimental.pallas import tpu as pltpu

F32 = jnp.float32
BF16 = jnp.bfloat16

D_MODEL = 1024
CTX_LEN = 256
GRID_W = 64
EPS = 1e-6
N_MOD = 9
D_FF = 2816
DN_HEADS = 4
DN_DK = 128
DN_DV = 128
DN_WIDTH = DN_HEADS * DN_DV
DN_CONV = 5
AT_HEADS = 4
AT_KV_HEADS = 2
AT_HD = 128
AT_WIDTH = AT_HEADS * AT_HD
ATT_SCALE = AT_HD ** -0.5
ROPE_AXIS_DIM = AT_HD // 2
ROPE_THETA = 10000.0
LEN_DN_QKV = 3 * DN_WIDTH
OFF_DN_Z = LEN_DN_QKV
OFF_DN_B = OFF_DN_Z + DN_WIDTH
OFF_DN_A = OFF_DN_B + 2 * DN_HEADS
OFF_AT_Q = OFF_DN_A + 2 * DN_HEADS
OFF_AT_K = OFF_AT_Q + AT_WIDTH
OFF_AT_V = OFF_AT_K + AT_KV_HEADS * AT_HD
P_IN = OFF_AT_V + AT_KV_HEADS * AT_HD

LANES = 128
CHUNK = 128
FF_TILE = 256
DN_GROUP = 8
DN_CONV_BLOCKS = 2
AT_KEYS = 512
LOG2E = 1.4426950408889634
AT_SAFE_LOG2 = 60.0
VMEM_LIMIT = 56 * 1024 * 1024
NEG_BIG = -1e30

_NT = (((1,), (1,)), ((), ()))


def _sigmoid(x):
    return 1.0 / (1.0 + jnp.exp(-x))


def _silu(x):
    return x * _sigmoid(x)


def _rms(x, gain):
    ms = jnp.mean(x * x, axis=-1, keepdims=True)
    return x * lax.rsqrt(ms + EPS) * gain


def _dot(a, b):
    return jnp.dot(a, b, preferred_element_type=F32)


def _const_spec(shape):
    nd = len(shape)
    return pl.BlockSpec(shape, lambda *_: (0,) * nd, pipeline_mode=pl.Buffered(1))


def _params(n_axes):
    return pltpu.CompilerParams(dimension_semantics=("arbitrary",) * n_axes,
                                vmem_limit_bytes=VMEM_LIMIT)


def _mod_kernel(c_ref, w_ref, b_ref, o_ref):
    s = _silu(c_ref[...]).astype(BF16)
    o_ref[...] = _dot(s, w_ref[...].astype(BF16)) + b_ref[...]


def _mod_call(cc, w_mod, b_mod):
    d = cc.shape[1]
    n = w_mod.shape[1]
    return pl.pallas_call(
        _mod_kernel,
        out_shape=jax.ShapeDtypeStruct((cc.shape[0], n), F32),
        grid=(n // d,),
        in_specs=[pl.BlockSpec(cc.shape, lambda j: (0, 0)),
                  pl.BlockSpec((d, d), lambda j: (0, j)),
                  pl.BlockSpec((1, d), lambda j: (0, j))],
        out_specs=pl.BlockSpec((cc.shape[0], d), lambda j: (0, j)),
        compiler_params=_params(1),
        name="mod",
    )(cc, w_mod, b_mod)


def _swiglu_update(x, mod_ref, j0, g_ref, w1_ref, w3_ref, w2_ref, act_ref):
    h = (_rms(x, g_ref[...]) * (1.0 + mod_ref[j0 + 1:j0 + 2, :]) + mod_ref[j0:j0 + 1, :]).astype(BF16)
    for c in range(D_FF // FF_TILE):
        sl = slice(c * FF_TILE, (c + 1) * FF_TILE)
        a = _dot(h, w1_ref[:, sl])
        b = _dot(h, w3_ref[:, sl])
        act_ref[:, sl] = (_silu(a) * b).astype(BF16)
    y = _dot(act_ref[...], w2_ref[...])
    return x + (0.5 * mod_ref[j0 + 2:j0 + 3, :]) * y


def _ffn_kernel(x_ref, mod_ref, g_ref, w1_ref, w3_ref, w2_ref, o_ref, act_ref):
    o_ref[...] = _swiglu_update(x_ref[...], mod_ref, 0, g_ref, w1_ref, w3_ref, w2_ref, act_ref)


def _ffn_call(x2d, mod3, mod_row, g, w1, w3, w2, tm, name):
    n, d = x2d.shape
    return pl.pallas_call(
        _ffn_kernel,
        out_shape=jax.ShapeDtypeStruct((n, d), F32),
        grid=(n // tm,),
        in_specs=[pl.BlockSpec((tm, d), lambda i: (i, 0)),
                  pl.BlockSpec((None, N_MOD, d), lambda i: (mod_row(i), 0, 0)),
                  _const_spec((1, d)),
                  _const_spec(w1.shape), _const_spec(w3.shape), _const_spec(w2.shape)],
        out_specs=pl.BlockSpec((tm, d), lambda i: (i, 0)),
        scratch_shapes=[pltpu.VMEM((tm, D_FF), BF16)],
        compiler_params=_params(1),
        name=name,
    )(x2d, mod3, g, w1, w3, w2)


def _rope(x, cos, sin_signed, even):
    nxt = pltpu.roll(x, LANES - 1, axis=1)
    prv = pltpu.roll(x, 1, axis=1)
    return x * cos + jnp.where(even, nxt, prv) * sin_signed


def _inproj_kernel(x_ref, mod_ref, g_ref, w_ref, alog_ref, dtb_ref, qn_ref, kn_ref, cos_ref, sin_ref,
                   qkv_ref, z_ref, bg_ref, aq_ref, ak_ref, av_ref, *, rope):
    tm = x_ref.shape[0]
    x = x_ref[...]
    h = (_rms(x, g_ref[...]) * (1.0 + mod_ref[4:5, :]) + mod_ref[3:4, :]).astype(BF16)

    for part in range(3):
        p = _dot(h, w_ref[:, part * DN_WIDTH:(part + 1) * DN_WIDTH])
        for hd in range(DN_HEADS):
            qkv_ref[part * DN_HEADS + hd] = p[:, hd * LANES:(hd + 1) * LANES].astype(BF16)
    z_ref[...] = _dot(h, w_ref[:, LEN_DN_QKV:LEN_DN_QKV + DN_WIDTH]).astype(BF16)

    lane = lax.broadcasted_iota(jnp.int32, (tm, LANES), 1)
    even = (lane % 2) == 0
    if rope:
        cos = cos_ref[...]
        sin = sin_ref[...]

    off = LEN_DN_QKV + DN_WIDTH
    p = _dot(h, w_ref[:, off:off + AT_WIDTH])
    for hd in range(AT_HEADS):
        q = _rms(p[:, hd * LANES:(hd + 1) * LANES], qn_ref[...])
        if rope:
            q = _rope(q, cos, sin, even)
        aq_ref[hd // 2, :, (hd % 2) * LANES:(hd % 2 + 1) * LANES] = (q * (ATT_SCALE * LOG2E)).astype(BF16)
    off += AT_WIDTH
    p = _dot(h, w_ref[:, off:off + 2 * AT_KV_HEADS * AT_HD])
    for hd in range(AT_KV_HEADS):
        k = _rms(p[:, hd * LANES:(hd + 1) * LANES], kn_ref[...])
        if rope:
            k = _rope(k, cos, sin, even)
        ak_ref[hd] = k.astype(BF16)
        av_ref[hd] = p[:, (AT_KV_HEADS + hd) * LANES:(AT_KV_HEADS + hd + 1) * LANES].T.astype(BF16)

    off += 2 * AT_KV_HEADS * AT_HD
    p = _dot(h, w_ref[:, off:off + LANES])
    beta = _sigmoid(p)
    t = p + dtb_ref[...]
    softplus = jnp.maximum(t, 0.0) + jnp.log(1.0 + jnp.exp(-jnp.abs(t)))
    g = -jnp.exp(alog_ref[...]) * softplus
    ri = lax.broadcasted_iota(jnp.int32, (CHUNK, CHUNK), 0)
    ci = lax.broadcasted_iota(jnp.int32, (CHUNK, CHUNK), 1)
    ltri = jnp.where(ri >= ci, 1.0, 0.0).astype(BF16)
    lane_c = lax.broadcasted_iota(jnp.int32, (CHUNK, LANES), 1)
    for c in range(tm // CHUNK):
        rows = slice(c * CHUNK, (c + 1) * CHUNK)
        gc = g[rows]
        g1 = gc.astype(BF16)
        r1 = gc - g1.astype(F32)
        g2 = r1.astype(BF16)
        g3 = (r1 - g2.astype(F32)).astype(BF16)
        pre = _dot(ltri, g1) + _dot(ltri, g2) + _dot(ltri, g3)
        suf = pre[CHUNK - 1:CHUNK, :] - pre + gc
        gcs = jnp.where(lane_c < 8 + DN_HEADS, pre, suf)
        bg_ref[rows, :] = jnp.where(lane_c < 2 * DN_HEADS, beta[rows], gcs)


def _inproj_call(x2d, mod3, mod_row, g, w_r, alog_row, dtb_row, qn, kn, cos, sin, bsz, t, tm, rope, name):
    n, d = x2d.shape
    tpb = t // tm
    bt = lambda i: (i // tpb, 0, i % tpb, 0)
    out_shape = (
        jax.ShapeDtypeStruct((bsz, 3 * DN_HEADS, t, LANES), BF16),
        jax.ShapeDtypeStruct((n, DN_WIDTH), BF16),
        jax.ShapeDtypeStruct((n, LANES), F32),
        jax.ShapeDtypeStruct((bsz, AT_KV_HEADS, t, 2 * AT_HD), BF16),
        jax.ShapeDtypeStruct((bsz, AT_KV_HEADS, t, AT_HD), BF16),
        jax.ShapeDtypeStruct((bsz, AT_KV_HEADS, AT_HD, t), BF16),
    )
    out_specs = (
        pl.BlockSpec((None, 3 * DN_HEADS, tm, LANES), bt),
        pl.BlockSpec((tm, DN_WIDTH), lambda i: (i, 0)),
        pl.BlockSpec((tm, LANES), lambda i: (i, 0)),
        pl.BlockSpec((None, AT_KV_HEADS, tm, 2 * AT_HD), bt),
        pl.BlockSpec((None, AT_KV_HEADS, tm, AT_HD), bt),
        pl.BlockSpec((None, AT_KV_HEADS, AT_HD, tm), lambda i: (i // tpb, 0, 0, i % tpb)),
    )
    return pl.pallas_call(
        functools.partial(_inproj_kernel, rope=rope),
        out_shape=out_shape,
        grid=(n // tm,),
        in_specs=[pl.BlockSpec((tm, d), lambda i: (i, 0)),
                  pl.BlockSpec((None, N_MOD, d), lambda i: (mod_row(i), 0, 0)),
                  _const_spec((1, d)),
                  _const_spec(w_r.shape),
                  _const_spec((1, LANES)), _const_spec((1, LANES)),
                  _const_spec((1, LANES)), _const_spec((1, LANES)),
                  pl.BlockSpec((tm, LANES), lambda i: (i % tpb, 0)),
                  pl.BlockSpec((tm, LANES), lambda i: (i % tpb, 0))],
        out_specs=out_specs,
        compiler_params=_params(1),
        name=name,
    )(x2d, mod3, g, w_r, alog_row, dtb_row, qn, kn, cos, sin)


def _dn_conv_all(src_refs, w_refs, xpad_ref, t, emits):
    nblk = t // CHUNK
    n = len(src_refs)
    for i in range(n):
        xpad_ref[i, 0:8, :] = jnp.zeros((8, LANES), F32)
        xpad_ref[i, 8 + t:16 + t, :] = jnp.zeros((8, LANES), F32)

    def fill(r, carry):
        r0 = pl.multiple_of(r * CHUNK, CHUNK)
        for i in range(n):
            xpad_ref[i, pl.ds(r0 + 8, CHUNK), :] = src_refs[i][pl.ds(r0, CHUNK), :].astype(F32)
        return carry

    lax.fori_loop(0, nblk, fill, 0)
    ws = [w_ref[...] for w_ref in w_refs]
    pad = (DN_CONV - 1) // 2
    win = CHUNK + 16

    def conv(r, carry):
        for u in range(DN_CONV_BLOCKS):
            r0 = pl.multiple_of((r * DN_CONV_BLOCKS + u) * CHUNK, CHUNK)
            for i in range(n):
                acc = None
                for j in range(DN_CONV):
                    tap = xpad_ref[i, pl.ds(r0 + (8 - pad + j), CHUNK), :] * ws[i][j:j + 1, :]
                    acc = tap if acc is None else acc + tap
                emits[i](r0, _silu(acc))
        return carry

    lax.fori_loop(0, nblk // DN_CONV_BLOCKS, conv, 0)


def _block_diag(cat):
    c = cat.shape[0]
    zero = jnp.zeros((c, c), cat.dtype)
    return jnp.concatenate([jnp.concatenate([cat[:, :c], zero], axis=1),
                            jnp.concatenate([zero, cat[:, c:]], axis=1)], axis=0)


def _dn_prep_chunk(a, qs_ref, ks_ref, vs_ref, bg_ref, lvl_ref):
    row = pl.multiple_of(a * CHUNK, CHUNK)
    k = ks_ref[pl.ds(row, CHUNK), :]
    q = qs_ref[pl.ds(row, CHUNK), :]
    v = vs_ref[pl.ds(row, CHUNK), :]
    bg = bg_ref[pl.ds(row, CHUNK), :]
    kf = k.astype(F32)
    ri = lax.broadcasted_iota(jnp.int32, (CHUNK, CHUNK), 0)
    ci = lax.broadcasted_iota(jnp.int32, (CHUNK, CHUNK), 1)
    per_dir = []
    for d in range(2):
        beta = bg[:, d:d + 1]
        g_c = jnp.broadcast_to(bg[:, 2 + d:3 + d], (CHUNK, CHUNK))
        incl = (ri >= ci) if d == 0 else (ri <= ci)
        decay = jnp.exp(jnp.where(incl, g_c - g_c.T, NEG_BIG))
        per_dir.append((beta, g_c, decay, kf * beta))
    aq = lax.dot_general(jnp.concatenate([per_dir[0][3].astype(BF16), per_dir[1][3].astype(BF16), q], axis=0),
                         k, _NT, preferred_element_type=F32)
    yield
    qk = aq[2 * CHUNK:]
    a_cat = jnp.concatenate([jnp.where(ri > ci, aq[:CHUNK] * per_dir[0][2], 0.0),
                             jnp.where(ri < ci, aq[CHUNK:2 * CHUNK] * per_dir[1][2], 0.0)], axis=1)

    eye = jnp.where(ri == ci, 1.0, 0.0)
    eye_cat = jnp.concatenate([eye, eye], axis=1)
    pair = jnp.where((ri >> 1) == (ci >> 1), 1.0, 0.0)
    t_cat = eye_cat - a_cat * jnp.concatenate([pair, pair], axis=1)
    for lvl in range(lvl_ref.shape[0]):
        a_sub = _block_diag((a_cat * lvl_ref[lvl]).astype(BF16))
        x = _dot(t_cat.astype(BF16), a_sub)
        yield
        t_cat = t_cat - _dot(x.astype(BF16), _block_diag(t_cat.astype(BF16)))
        yield
    toff = t_cat - eye_cat

    results = []
    for d in range(2):
        beta, g_c, decay, kb = per_dir[d]
        eg = jnp.exp(g_c)
        rhs = jnp.concatenate([v * beta, kb * eg], axis=1)
        uw = rhs + _dot(toff[:, d * CHUNK:(d + 1) * CHUNK].astype(BF16), rhs.astype(BF16))
        g_last = g_c[CHUNK - 1:CHUNK, :] if d == 0 else g_c[0:1, :]
        k_tail = kf * jnp.exp(g_last - g_c)
        q_head = q.astype(F32) * eg
        results.append((uw[:, :DN_DV].astype(BF16),
                        jnp.concatenate([uw[:, DN_DV:].astype(BF16), q_head.astype(BF16)], axis=0),
                        jnp.concatenate([(qk * decay).astype(BF16), k_tail.T.astype(BF16)], axis=0),
                        jnp.broadcast_to(jnp.exp(g_last), (8, LANES))))
    return results


def _dn_prep_store(a, results, wq_ref, ak_ref, u_ref, eg_ref):
    for d in range(2):
        u_ref[d, a], wq_ref[d, a], ak_ref[d, a], eg_ref[d, a] = results[d]


def _dn_prep_kernel(ql_ref, kl_ref, vl_ref, qc_ref, kc_ref, vc_ref, wq_w_ref, wk_w_ref, wv_w_ref, bg_ref,
                    lvl_ref, wq_ref, ak_ref, u_ref, eg_ref, qs_ref, ks_ref, vs_ref, xpad_ref):
    t = ql_ref.shape[0]
    tc = qc_ref.shape[0]

    def emit_qk(dst_ref, base, scale):
        def emit(r0, y):
            yn = y * lax.rsqrt(jnp.sum(y * y, axis=-1, keepdims=True) + EPS)
            dst_ref[pl.ds(base + r0, CHUNK), :] = (yn * scale).astype(BF16)
        return emit

    def emit_v(base):
        def emit(r0, y):
            vs_ref[pl.ds(base + r0, CHUNK), :] = y
        return emit

    w_refs = (wq_w_ref, wk_w_ref, wv_w_ref)
    for srcs, tt, base in (((qc_ref, kc_ref, vc_ref), tc, 0), ((ql_ref, kl_ref, vl_ref), t, tc)):
        _dn_conv_all(srcs, w_refs, xpad_ref, tt,
                     (emit_qk(qs_ref, base, DN_DK ** -0.5), emit_qk(ks_ref, base, 1.0), emit_v(base)))

    refs = (qs_ref, ks_ref, vs_ref, bg_ref, lvl_ref)
    outs = (wq_ref, ak_ref, u_ref, eg_ref)
    n_ctx = tc // CHUNK

    def run(chunks):
        gens = [_dn_prep_chunk(a, *refs) for a in chunks]
        results = [None] * len(gens)
        while any(r is None for r in results):
            for i, gen in enumerate(gens):
                if results[i] is None:
                    try:
                        next(gen)
                    except StopIteration as done:
                        results[i] = done.value
        for a, res in zip(chunks, results):
            _dn_prep_store(a, res, *outs)

    run(list(range(n_ctx)))

    def group(gi, carry):
        run([n_ctx + gi * DN_GROUP + j for j in range(DN_GROUP)])
        return carry

    lax.fori_loop(0, t // (CHUNK * DN_GROUP), group, 0)


def _dn_masks():
    i = np.arange(CHUNK)
    lvls = []
    b = 2
    while b < CHUNK:
        rb, cb = (i // b)[:, None], (i // b)[None, :]
        fwd = ((rb == cb + 1) & (rb % 2 == 1)).astype(np.float32)
        lvls.append(np.concatenate([fwd, fwd.T], axis=1))
        b *= 2
    return jnp.asarray(np.stack(lvls))


def _dn_prep_call(qkv_l, qkv_c, conv_w, bg_all, lvl_masks):
    bsz, _, t, _ = qkv_l.shape
    tc = qkv_c.shape[2]
    ta = t + tc
    nc = ta // CHUNK

    def head(part, tt):
        return pl.BlockSpec((None, None, tt, LANES), lambda b, h: (b, part * DN_HEADS + h, 0, 0))

    def wspec(part):
        return pl.BlockSpec((None, DN_CONV, LANES), lambda b, h: (part * DN_HEADS + h, 0, 0))

    def out(rows, dt):
        return (jax.ShapeDtypeStruct((bsz, DN_HEADS, 2, nc, rows, LANES), dt),
                pl.BlockSpec((None, None, 2, nc, rows, LANES), lambda b, h: (b, h, 0, 0, 0, 0)))

    outs = [out(2 * CHUNK, BF16), out(2 * CHUNK, BF16), out(CHUNK, BF16), out(8, F32)]
    return pl.pallas_call(
        _dn_prep_kernel,
        out_shape=tuple(o[0] for o in outs),
        grid=(bsz, DN_HEADS),
        in_specs=[head(0, t), head(1, t), head(2, t), head(0, tc), head(1, tc), head(2, tc),
                  wspec(0), wspec(1), wspec(2),
                  pl.BlockSpec((None, None, ta, 4), lambda b, h: (b, h, 0, 0)),
                  _const_spec(lvl_masks.shape)],
        out_specs=tuple(o[1] for o in outs),
        scratch_shapes=[pltpu.VMEM((ta, LANES), BF16), pltpu.VMEM((ta, LANES), BF16),
                        pltpu.VMEM((ta, LANES), F32), pltpu.VMEM((3, t + 16, LANES), F32)],
        compiler_params=_params(2),
        name="dn_prep",
    )(qkv_l, qkv_l, qkv_l, qkv_c, qkv_c, qkv_c, conv_w, conv_w, conv_w, bg_all, lvl_masks)


def _dn_scan_kernel(wq0_ref, wq1_ref, ak0_ref, ak1_ref, u0_ref, u1_ref, eg0_ref, eg1_ref, o0_ref, o1_ref, s_ref):
    @pl.when(pl.program_id(1) == 0)
    def _():
        s_ref[...] = jnp.zeros(s_ref.shape, F32)

    for h in range(DN_HEADS):
        s = s_ref[h]
        s_bd = _block_diag(s.astype(BF16))
        r1 = _dot(jnp.concatenate([wq0_ref[h], wq1_ref[h]], axis=1), s_bd)
        u = jnp.concatenate([u0_ref[h], u1_ref[h]], axis=1).astype(F32)
        v_new = u - r1[:CHUNK]
        v_bd = _block_diag(v_new.astype(BF16))
        r2 = _dot(jnp.concatenate([ak0_ref[h], ak1_ref[h]], axis=1), v_bd)
        o = r1[CHUNK:] + r2[:CHUNK]
        o0_ref[:, h * DN_DV:(h + 1) * DN_DV] = o[:, :DN_DV]
        o1_ref[:, h * DN_DV:(h + 1) * DN_DV] = o[:, DN_DV:]
        dec = jnp.concatenate([eg0_ref[h, 0:1, :], eg1_ref[h, 0:1, :]], axis=1)
        s_ref[h] = s * dec + r2[CHUNK:]


def _dn_scan_call(wq, ak, u, eg, t):
    bsz, nh, _, nc, _, _ = wq.shape
    n_ctx = nc - t // CHUNK

    def bwd_chunk(i):
        return jnp.where(i < n_ctx, n_ctx - 1 - i, nc - 1 + n_ctx - i)

    def spec(rows, d):
        if d == 0:
            return pl.BlockSpec((None, nh, None, None, rows, LANES), lambda b, i: (b, 0, 0, i, 0, 0))
        return pl.BlockSpec((None, nh, None, None, rows, LANES), lambda b, i: (b, 0, 1, bwd_chunk(i), 0, 0))

    o_shape = jax.ShapeDtypeStruct((bsz, t, nh * DN_DV), F32)
    o0_spec = pl.BlockSpec((None, CHUNK, nh * DN_DV), lambda b, i: (b, jnp.maximum(i - n_ctx, 0), 0))
    o1_spec = pl.BlockSpec((None, CHUNK, nh * DN_DV),
                           lambda b, i: (b, bwd_chunk(jnp.maximum(i, n_ctx)) - n_ctx, 0))
    return pl.pallas_call(
        _dn_scan_kernel,
        out_shape=(o_shape, o_shape),
        grid=(bsz, nc),
        in_specs=[spec(2 * CHUNK, 0), spec(2 * CHUNK, 1), spec(2 * CHUNK, 0), spec(2 * CHUNK, 1),
                  spec(CHUNK, 0), spec(CHUNK, 1), spec(8, 0), spec(8, 1)],
        out_specs=(o0_spec, o1_spec),
        scratch_shapes=[pltpu.VMEM((nh, DN_DK, 2 * DN_DV), F32)],
        compiler_params=_params(2),
        name="dn_scan",
    )(wq, wq, ak, ak, u, u, eg, eg)


def _attn_kernel(q_ref, kl_ref, vtl_ref, kc_ref, vtc_ref, o_ref, kmax_ref, shift_ref):
    tq = q_ref.shape[0]
    n = 2 * tq
    t = kl_ref.shape[0]
    blocks = [(kl_ref, vtl_ref, j * AT_KEYS, AT_KEYS) for j in range(t // AT_KEYS)]
    blocks.append((kc_ref, vtc_ref, 0, kc_ref.shape[0]))

    @pl.when(pl.program_id(2) == 0)
    def _():
        best = jnp.zeros((1, 1), F32)
        for k_ref, _, off, size in blocks:
            kk = k_ref[off:off + size, :].astype(F32)
            best = jnp.maximum(best, jnp.max(jnp.sum(kk * kk, axis=-1, keepdims=True), axis=0, keepdims=True))
        kmax_ref[...] = jnp.broadcast_to(best, kmax_ref.shape)

    q = q_ref[...].astype(F32)
    qt32 = jnp.concatenate([q[:, :AT_HD].T, q[:, AT_HD:].T], axis=1)
    qt = qt32.astype(BF16)
    bound = jnp.sqrt(jnp.sum(qt32 * qt32, axis=0, keepdims=True) * kmax_ref[0:1, 0:1])
    shift_ref[...] = bound

    def scores(blk):
        k_ref, _, off, size = blk
        return _dot(k_ref[off:off + size, :], qt)

    @pl.when(jnp.max(bound) > AT_SAFE_LOG2)
    def _():
        m = jnp.full((1, n), NEG_BIG, F32)
        for blk in blocks:
            m = jnp.maximum(m, jnp.max(scores(blk), axis=0, keepdims=True))
        shift_ref[...] = m

    shift = shift_ref[...]
    den = jnp.zeros((1, n), F32)
    acc = jnp.zeros((AT_HD, n), F32)
    s_next = scores(blocks[0])
    for j, (_, vt_ref, off, size) in enumerate(blocks):
        s = s_next
        if j + 1 < len(blocks):
            s_next = scores(blocks[j + 1])
        p = jnp.exp2(s - shift)
        den = den + jnp.sum(p, axis=0, keepdims=True)
        acc = acc + _dot(vt_ref[:, off:off + size], p.astype(BF16))
    o = (acc / den).T
    o_ref[:, :AT_HD] = o[:tq].astype(BF16)
    o_ref[:, AT_HD:] = o[tq:].astype(BF16)


def _attn_call(aq, ak_l, avt_l, ak_c, avt_c, tq):
    bsz, hkv, t, _ = aq.shape
    tc = ak_c.shape[2]
    kspec = lambda tt: pl.BlockSpec((None, None, tt, AT_HD), lambda b, j, i: (b, j, 0, 0))
    vspec = lambda tt: pl.BlockSpec((None, None, AT_HD, tt), lambda b, j, i: (b, j, 0, 0))
    return pl.pallas_call(
        _attn_kernel,
        out_shape=jax.ShapeDtypeStruct((bsz, t, AT_WIDTH), BF16),
        grid=(bsz, hkv, t // tq),
        in_specs=[pl.BlockSpec((None, None, tq, 2 * AT_HD), lambda b, j, i: (b, j, i, 0)),
                  kspec(t), vspec(t), kspec(tc), vspec(tc)],
        out_specs=pl.BlockSpec((None, tq, 2 * AT_HD), lambda b, j, i: (b, i, j)),
        scratch_shapes=[pltpu.VMEM((8, LANES), F32), pltpu.VMEM((1, 2 * tq), F32)],
        compiler_params=_params(3),
        name="attn",
    )(aq, ak_l, avt_l, ak_c, avt_c)


def _tail_kernel(x_ref, o0_ref, o1_ref, z_ref, at_ref, mod_ref, gn_ref, wo_ref, g_ref, w1_ref, w3_ref, w2_ref,
                 gf_ref, o_ref, act_ref, dn_ref):
    for hd in range(DN_HEADS):
        sl = slice(hd * DN_DV, (hd + 1) * DN_DV)
        o = o0_ref[:, sl] + o1_ref[:, sl]
        dn_ref[:, sl] = (_rms(o, gn_ref[...]) * _silu(z_ref[:, sl].astype(F32))).astype(BF16)
    mix = _dot(dn_ref[...], wo_ref[:DN_WIDTH, :]) + _dot(at_ref[...], wo_ref[DN_WIDTH:, :])
    x = x_ref[...] + mod_ref[5:6, :] * mix
    x = _swiglu_update(x, mod_ref, 6, g_ref, w1_ref, w3_ref, w2_ref, act_ref)
    o_ref[...] = _rms(x, gf_ref[...])


def _tail_call(x2d, o0, o1, z2d, at2d, mod3, mod_row, gn, wo, g, w1, w3, w2, gf, tm):
    n, d = x2d.shape
    return pl.pallas_call(
        _tail_kernel,
        out_shape=jax.ShapeDtypeStruct((n, d), F32),
        grid=(n // tm,),
        in_specs=[pl.BlockSpec((tm, d), lambda i: (i, 0)),
                  pl.BlockSpec((tm, DN_WIDTH), lambda i: (i, 0)),
                  pl.BlockSpec((tm, DN_WIDTH), lambda i: (i, 0)),
                  pl.BlockSpec((tm, DN_WIDTH), lambda i: (i, 0)),
                  pl.BlockSpec((tm, AT_WIDTH), lambda i: (i, 0)),
                  pl.BlockSpec((None, N_MOD, d), lambda i: (mod_row(i), 0, 0)),
                  _const_spec((1, LANES)),
                  _const_spec(wo.shape), _const_spec((1, d)),
                  _const_spec(w1.shape), _const_spec(w3.shape), _const_spec(w2.shape),
                  _const_spec((1, d))],
        out_specs=pl.BlockSpec((tm, d), lambda i: (i, 0)),
        scratch_shapes=[pltpu.VMEM((tm, D_FF), BF16), pltpu.VMEM((tm, DN_WIDTH), BF16)],
        compiler_params=_params(1),
        name="tail",
    )(x2d, o0, o1, z2d, at2d, mod3, gn, wo, g, w1, w3, w2, gf)


def _rope_tables(n):
    rows = n // GRID_W
    row = jnp.repeat(jnp.arange(rows, dtype=jnp.int32), GRID_W).astype(F32)
    col = jnp.tile(jnp.arange(GRID_W, dtype=jnp.int32), rows).astype(F32)
    freqs = 1.0 / (ROPE_THETA ** (jnp.arange(0, ROPE_AXIS_DIM, 2, dtype=F32) / ROPE_AXIS_DIM))
    ang = jnp.concatenate([row[:, None] * freqs, col[:, None] * freqs], axis=-1)
    cos = jnp.repeat(jnp.cos(ang), 2, axis=-1)
    sin = jnp.repeat(jnp.sin(ang), 2, axis=-1)
    sign = jnp.tile(jnp.array([-1.0, 1.0], F32), AT_HD // 2)
    return cos, sin * sign


def _reorder_w_in(w_in):
    ba = w_in[:, OFF_DN_B:OFF_AT_Q]
    ba = jnp.pad(ba, ((0, 0), (0, LANES - ba.shape[1])))
    return jnp.concatenate([w_in[:, :OFF_DN_B], w_in[:, OFF_AT_Q:], ba], axis=1).astype(BF16)


def _lane_row(vec8, offset):
    return jnp.zeros((1, LANES), F32).at[0, offset:offset + vec8.shape[0]].set(vec8)


def _bg_heads(bg, bsz, t):
    return bg[:, :4 * DN_HEADS].reshape(bsz, t, 4, DN_HEADS).transpose(0, 3, 1, 2)


def kernel(x, c, ctx, c_ctx, w_mod, b_mod, g_ffn1, ffn1_w1, ffn1_w3, ffn1_w2, g_mix, w_in, dn_conv, dn_a_log,
           dn_dt_bias, dn_norm, q_norm, k_norm, w_out, g_ffn2, ffn2_w1, ffn2_w3, ffn2_w2, g_final):
    bsz, t, d = x.shape
    tc = ctx.shape[1]
    assert w_mod.shape[0] == 1, "single-layer block"
    tm = 512
    ctx_row = bsz

    cc = jnp.zeros((8, d), F32).at[:bsz].set(c).at[ctx_row].set(c_ctx)
    mod3 = _mod_call(cc, w_mod[0], b_mod[0][None]).reshape(8, N_MOD, d)

    lat_row = lambda i: i // (t // tm)
    ctx_rowf = lambda i: ctx_row
    row = lambda v: v.reshape(1, -1)

    w1a, w3a, w2a = ffn1_w1[0].astype(BF16), ffn1_w3[0].astype(BF16), ffn1_w2[0].astype(BF16)
    x1 = _ffn_call(x.reshape(bsz * t, d), mod3, lat_row, row(g_ffn1[0]), w1a, w3a, w2a, tm, "ffn1_lat")
    c1 = _ffn_call(ctx.reshape(bsz * tc, d), mod3, ctx_rowf, row(g_ffn1[0]), w1a, w3a, w2a, tc, "ffn1_ctx")

    w_r = _reorder_w_in(w_in[0])
    alog_row = _lane_row(dn_a_log[0].reshape(-1), 2 * DN_HEADS)
    dtb_row = _lane_row(dn_dt_bias[0].reshape(-1), 2 * DN_HEADS)
    cos, sin = _rope_tables(t)
    qn, kn = row(q_norm[0]), row(k_norm[0])
    qkv_l, z_l, bg_l, aq_l, ak_l, av_l = _inproj_call(
        x1, mod3, lat_row, row(g_mix[0]), w_r, alog_row, dtb_row, qn, kn, cos, sin, bsz, t, tm, True, "inproj_lat")
    qkv_c, _, bg_c, _, ak_c, av_c = _inproj_call(
        c1, mod3, ctx_rowf, row(g_mix[0]), w_r, alog_row, dtb_row, qn, kn, cos[:tc], sin[:tc], bsz, tc, tc, False,
        "inproj_ctx")

    bg_all = jnp.concatenate([_bg_heads(bg_c, bsz, tc), _bg_heads(bg_l, bsz, t)], axis=2)
    conv_w = dn_conv[0].reshape(DN_CONV, 3 * DN_HEADS, LANES).transpose(1, 0, 2)
    dn_wq, dn_ak, dn_u, dn_eg = _dn_prep_call(qkv_l, qkv_c, conv_w, bg_all, _dn_masks())
    o_fwd, o_bwd = _dn_scan_call(dn_wq, dn_ak, dn_u, dn_eg, t)
    at_lat = _attn_call(aq_l, ak_l, av_l, ak_c, av_c, 512)

    out = _tail_call(x1, o_fwd.reshape(bsz * t, DN_WIDTH), o_bwd.reshape(bsz * t, DN_WIDTH), z_l,
                     at_lat.reshape(bsz * t, AT_WIDTH), mod3, lat_row, row(dn_norm[0]), w_out[0].astype(BF16),
                     row(g_ffn2[0]), ffn2_w1[0].astype(BF16), ffn2_w3[0].astype(BF16), ffn2_w2[0].astype(BF16),
                     row(g_final), tm)
    return out.reshape(bsz, t, d)
```

```python
import functools

import numpy as np
import jax
import jax.numpy as jnp
from jax import lax
from jax.experimental import pallas as pl
from jax.experimental.pallas import tpu as pltpu

F32 = jnp.float32
BF16 = jnp.bfloat16

D_MODEL = 1024
CTX_LEN = 256
GRID_W = 64
EPS = 1e-6
N_MOD = 9
D_FF = 2816
DN_HEADS = 4
DN_DK = 128
DN_DV = 128
DN_WIDTH = DN_HEADS * DN_DV
DN_CONV = 5
AT_HEADS = 4
AT_KV_HEADS = 2
AT_HD = 128
AT_WIDTH = AT_HEADS * AT_HD
ATT_SCALE = AT_HD ** -0.5
ROPE_AXIS_DIM = AT_HD // 2
ROPE_THETA = 10000.0
LEN_DN_QKV = 3 * DN_WIDTH
OFF_DN_Z = LEN_DN_QKV
OFF_DN_B = OFF_DN_Z + DN_WIDTH
OFF_DN_A = OFF_DN_B + 2 * DN_HEADS
OFF_AT_Q = OFF_DN_A + 2 * DN_HEADS
OFF_AT_K = OFF_AT_Q + AT_WIDTH
OFF_AT_V = OFF_AT_K + AT_KV_HEADS * AT_HD
P_IN = OFF_AT_V + AT_KV_HEADS * AT_HD

LANES = 128
CHUNK = 128
FF_TILE = 256
DN_GROUP = 8
DN_BG_ROWS = 8
DN_SCAN_CHUNKS = 2
DN_CONV_BLOCKS = 2
AT_KEYS = 512
LOG2E = 1.4426950408889634
AT_SAFE_LOG2 = 60.0
VMEM_LIMIT = 56 * 1024 * 1024
NEG_BIG = -1e30

_NT = (((1,), (1,)), ((), ()))


def _sigmoid(x):
    return 1.0 / (1.0 + jnp.exp(-x))


def _silu(x):
    return x * _sigmoid(x)


def _rms(x, gain):
    ms = jnp.mean(x * x, axis=-1, keepdims=True)
    return x * lax.rsqrt(ms + EPS) * gain


def _dot(a, b):
    return jnp.dot(a, b, preferred_element_type=F32)


def _const_spec(shape):
    nd = len(shape)
    return pl.BlockSpec(shape, lambda *_: (0,) * nd, pipeline_mode=pl.Buffered(1))


def _params(n_axes):
    return pltpu.CompilerParams(dimension_semantics=("arbitrary",) * n_axes,
                                vmem_limit_bytes=VMEM_LIMIT)


def _mod_kernel(c_ref, w_ref, b_ref, o_ref):
    s = _silu(c_ref[...]).astype(BF16)
    o_ref[...] = _dot(s, w_ref[...].astype(BF16)) + b_ref[...]


def _mod_call(cc, w_mod, b_mod):
    d = cc.shape[1]
    n = w_mod.shape[1]
    return pl.pallas_call(
        _mod_kernel,
        out_shape=jax.ShapeDtypeStruct((cc.shape[0], n), F32),
        grid=(n // d,),
        in_specs=[pl.BlockSpec(cc.shape, lambda j: (0, 0)),
                  pl.BlockSpec((d, d), lambda j: (0, j)),
                  pl.BlockSpec((1, d), lambda j: (0, j))],
        out_specs=pl.BlockSpec((cc.shape[0], d), lambda j: (0, j)),
        compiler_params=_params(1),
        name="mod",
    )(cc, w_mod, b_mod)


def _swiglu_update(x, mod_ref, j0, g_ref, w1_ref, w3_ref, w2_ref, act_ref):
    h = (_rms(x, g_ref[...]) * (1.0 + mod_ref[j0 + 1:j0 + 2, :]) + mod_ref[j0:j0 + 1, :]).astype(BF16)
    for c in range(D_FF // FF_TILE):
        sl = slice(c * FF_TILE, (c + 1) * FF_TILE)
        a = _dot(h, w1_ref[:, sl])
        b = _dot(h, w3_ref[:, sl])
        act_ref[:, sl] = (_silu(a) * b).astype(BF16)
    y = _dot(act_ref[...], w2_ref[...])
    return x + (0.5 * mod_ref[j0 + 2:j0 + 3, :]) * y


def _ffn_kernel(x_ref, mod_ref, g_ref, w1_ref, w3_ref, w2_ref, o_ref, act_ref):
    o_ref[...] = _swiglu_update(x_ref[...], mod_ref, 0, g_ref, w1_ref, w3_ref, w2_ref, act_ref)


def _ffn_call(x2d, mod3, mod_row, g, w1, w3, w2, tm, name):
    n, d = x2d.shape
    return pl.pallas_call(
        _ffn_kernel,
        out_shape=jax.ShapeDtypeStruct((n, d), F32),
        grid=(n // tm,),
        in_specs=[pl.BlockSpec((tm, d), lambda i: (i, 0)),
                  pl.BlockSpec((None, N_MOD, d), lambda i: (mod_row(i), 0, 0)),
                  _const_spec((1, d)),
                  _const_spec(w1.shape), _const_spec(w3.shape), _const_spec(w2.shape)],
        out_specs=pl.BlockSpec((tm, d), lambda i: (i, 0)),
        scratch_shapes=[pltpu.VMEM((tm, D_FF), BF16)],
        compiler_params=_params(1),
        name=name,
    )(x2d, mod3, g, w1, w3, w2)


def _rope(x, cos, sin_signed, even):
    nxt = pltpu.roll(x, LANES - 1, axis=1)
    prv = pltpu.roll(x, 1, axis=1)
    return x * cos + jnp.where(even, nxt, prv) * sin_signed


def _inproj_kernel(x_ref, mod_ref, g_ref, w_ref, alog_ref, dtb_ref, qn_ref, kn_ref, cos_ref, sin_ref,
                   qkv_ref, z_ref, bg_ref, aq_ref, ak_ref, av_ref, *, rope):
    tm = x_ref.shape[0]
    x = x_ref[...]
    h = (_rms(x, g_ref[...]) * (1.0 + mod_ref[4:5, :]) + mod_ref[3:4, :]).astype(BF16)

    off_z = LEN_DN_QKV
    off_q = off_z + DN_WIDTH
    off_kv = off_q + AT_WIDTH
    off_ba = off_kv + 2 * AT_KV_HEADS * AT_HD
    p_ba = _dot(h, w_ref[:, off_ba:off_ba + LANES])
    p_q = _dot(h, w_ref[:, off_q:off_q + AT_WIDTH])

    beta = _sigmoid(p_ba)
    t = p_ba + dtb_ref[...]
    softplus = jnp.maximum(t, 0.0) + jnp.log(1.0 + jnp.exp(-jnp.abs(t)))
    g = -jnp.exp(alog_ref[...]) * softplus
    ri = lax.broadcasted_iota(jnp.int32, (CHUNK, CHUNK), 0)
    ci = lax.broadcasted_iota(jnp.int32, (CHUNK, CHUNK), 1)
    ltri = jnp.where(ri >= ci, 1.0, 0.0).astype(BF16)
    kind = lax.broadcasted_iota(jnp.int32, (CHUNK, LANES), 1) % DN_BG_ROWS
    for c in range(tm // CHUNK):
        rows = slice(c * CHUNK, (c + 1) * CHUNK)
        gc = g[rows]
        g1 = gc.astype(BF16)
        r1 = gc - g1.astype(F32)
        g2 = r1.astype(BF16)
        g3 = (r1 - g2.astype(F32)).astype(BF16)
        pre = _dot(ltri, g1) + _dot(ltri, g2) + _dot(ltri, g3)
        suf = pre[CHUNK - 1:CHUNK, :] - pre + gc
        tile = jnp.where(kind < 2, beta[rows], jnp.where(kind == 2, pre, suf))
        bg_ref[:, rows] = tile.T[:DN_HEADS * DN_BG_ROWS]

    p_kv = _dot(h, w_ref[:, off_kv:off_kv + 2 * AT_KV_HEADS * AT_HD])
    lane = lax.broadcasted_iota(jnp.int32, (tm, LANES), 1)
    even = (lane % 2) == 0
    if rope:
        cos = cos_ref[...]
        sin = sin_ref[...]
    for hd in range(AT_HEADS):
        q = _rms(p_q[:, hd * LANES:(hd + 1) * LANES], qn_ref[...])
        if rope:
            q = _rope(q, cos, sin, even)
        aq_ref[hd // 2, :, (hd % 2) * LANES:(hd % 2 + 1) * LANES] = (q * (ATT_SCALE * LOG2E)).astype(BF16)
    for hd in range(AT_KV_HEADS):
        k = _rms(p_kv[:, hd * LANES:(hd + 1) * LANES], kn_ref[...])
        if rope:
            k = _rope(k, cos, sin, even)
        ak_ref[hd] = k.astype(BF16)
        av_ref[hd] = p_kv[:, (AT_KV_HEADS + hd) * LANES:(AT_KV_HEADS + hd + 1) * LANES].T.astype(BF16)

    z_ref[...] = _dot(h, w_ref[:, off_z:off_z + DN_WIDTH]).astype(BF16)
    for part in range(3):
        p = _dot(h, w_ref[:, part * DN_WIDTH:(part + 1) * DN_WIDTH])
        for hd in range(DN_HEADS):
            qkv_ref[part * DN_HEADS + hd] = p[:, hd * LANES:(hd + 1) * LANES].astype(BF16)


def _inproj_call(x2d, mod3, mod_row, g, w_r, alog_row, dtb_row, qn, kn, cos, sin, bsz, t, tm, rope, name):
    n, d = x2d.shape
    tpb = t // tm
    bt = lambda i: (i // tpb, 0, i % tpb, 0)
    out_shape = (
        jax.ShapeDtypeStruct((bsz, 3 * DN_HEADS, t, LANES), BF16),
        jax.ShapeDtypeStruct((n, DN_WIDTH), BF16),
        jax.ShapeDtypeStruct((bsz, DN_HEADS * DN_BG_ROWS, t), F32),
        jax.ShapeDtypeStruct((bsz, AT_KV_HEADS, t, 2 * AT_HD), BF16),
        jax.ShapeDtypeStruct((bsz, AT_KV_HEADS, t, AT_HD), BF16),
        jax.ShapeDtypeStruct((bsz, AT_KV_HEADS, AT_HD, t), BF16),
    )
    out_specs = (
        pl.BlockSpec((None, 3 * DN_HEADS, tm, LANES), bt),
        pl.BlockSpec((tm, DN_WIDTH), lambda i: (i, 0)),
        pl.BlockSpec((None, DN_HEADS * DN_BG_ROWS, tm), lambda i: (i // tpb, 0, i % tpb)),
        pl.BlockSpec((None, AT_KV_HEADS, tm, 2 * AT_HD), bt),
        pl.BlockSpec((None, AT_KV_HEADS, tm, AT_HD), bt),
        pl.BlockSpec((None, AT_KV_HEADS, AT_HD, tm), lambda i: (i // tpb, 0, 0, i % tpb)),
    )
    return pl.pallas_call(
        functools.partial(_inproj_kernel, rope=rope),
        out_shape=out_shape,
        grid=(n // tm,),
        in_specs=[pl.BlockSpec((tm, d), lambda i: (i, 0)),
                  pl.BlockSpec((None, N_MOD, d), lambda i: (mod_row(i), 0, 0)),
                  _const_spec((1, d)),
                  _const_spec(w_r.shape),
                  _const_spec((1, LANES)), _const_spec((1, LANES)),
                  _const_spec((1, LANES)), _const_spec((1, LANES)),
                  pl.BlockSpec((tm, LANES), lambda i: (i % tpb, 0)),
                  pl.BlockSpec((tm, LANES), lambda i: (i % tpb, 0))],
        out_specs=out_specs,
        compiler_params=_params(1),
        name=name,
    )(x2d, mod3, g, w_r, alog_row, dtb_row, qn, kn, cos, sin)


def _dn_conv_all(src_refs, w_refs, xpad_ref, t, emits):
    nblk = t // CHUNK
    n = len(src_refs)
    for i in range(n):
        xpad_ref[i, 0:8, :] = jnp.zeros((8, LANES), F32)
        xpad_ref[i, 8 + t:16 + t, :] = jnp.zeros((8, LANES), F32)

    def fill(r, carry):
        r0 = pl.multiple_of(r * CHUNK, CHUNK)
        for i in range(n):
            xpad_ref[i, pl.ds(r0 + 8, CHUNK), :] = src_refs[i][pl.ds(r0, CHUNK), :].astype(F32)
        return carry

    lax.fori_loop(0, nblk, fill, 0)
    ws = [w_ref[...] for w_ref in w_refs]
    pad = (DN_CONV - 1) // 2
    win = CHUNK + 16

    def conv(r, carry):
        for u in range(DN_CONV_BLOCKS):
            r0 = pl.multiple_of((r * DN_CONV_BLOCKS + u) * CHUNK, CHUNK)
            for i in range(n):
                acc = None
                for j in range(DN_CONV):
                    tap = xpad_ref[i, pl.ds(r0 + (8 - pad + j), CHUNK), :] * ws[i][j:j + 1, :]
                    acc = tap if acc is None else acc + tap
                emits[i](r0, _silu(acc))
        return carry

    lax.fori_loop(0, nblk // DN_CONV_BLOCKS, conv, 0)


def _block_diag(cat):
    c = cat.shape[0]
    zero = jnp.zeros((c, c), cat.dtype)
    return jnp.concatenate([jnp.concatenate([cat[:, :c], zero], axis=1),
                            jnp.concatenate([zero, cat[:, c:]], axis=1)], axis=0)


def _dn_prep_chunk(a, qs_ref, ks_ref, vs_ref, bg_ref, lvl_ref):
    row = pl.multiple_of(a * CHUNK, CHUNK)
    k = ks_ref[pl.ds(row, CHUNK), :]
    q = qs_ref[pl.ds(row, CHUNK), :]
    v = vs_ref[pl.ds(row, CHUNK), :]
    bg = bg_ref[a]
    kf = k.astype(F32)
    ri = lax.broadcasted_iota(jnp.int32, (CHUNK, CHUNK), 0)
    ci = lax.broadcasted_iota(jnp.int32, (CHUNK, CHUNK), 1)
    per_dir = []
    for d in range(2):
        beta = jnp.broadcast_to(bg[d:d + 1, :], (CHUNK, CHUNK)).T
        g_r = jnp.broadcast_to(bg[2 + d:3 + d, :], (CHUNK, CHUNK))
        g_c = g_r.T
        incl = (ri >= ci) if d == 0 else (ri <= ci)
        decay = jnp.exp(jnp.where(incl, g_c - g_r, NEG_BIG))
        per_dir.append((beta, g_c, decay, kf * beta))
    aq = lax.dot_general(jnp.concatenate([per_dir[0][3].astype(BF16), per_dir[1][3].astype(BF16), q], axis=0),
                         k, _NT, preferred_element_type=F32)
    yield
    qk = aq[2 * CHUNK:]
    a_cat = jnp.concatenate([jnp.where(ri > ci, aq[:CHUNK] * per_dir[0][2], 0.0),
                             jnp.where(ri < ci, aq[CHUNK:2 * CHUNK] * per_dir[1][2], 0.0)], axis=1)

    eye = jnp.where(ri == ci, 1.0, 0.0)
    eye_cat = jnp.concatenate([eye, eye], axis=1)
    pair = jnp.where((ri >> 1) == (ci >> 1), 1.0, 0.0)
    t_cat = eye_cat - a_cat * jnp.concatenate([pair, pair], axis=1)
    for lvl in range(lvl_ref.shape[0]):
        a_sub = _block_diag((a_cat * lvl_ref[lvl]).astype(BF16))
        x = _dot(t_cat.astype(BF16), a_sub)
        yield
        t_cat = t_cat - _dot(x.astype(BF16), _block_diag(t_cat.astype(BF16)))
        yield
    toff = t_cat - eye_cat

    results = []
    for d in range(2):
        beta, g_c, decay, kb = per_dir[d]
        eg = jnp.exp(g_c)
        rhs = jnp.concatenate([v * beta, kb * eg], axis=1)
        uw = rhs + _dot(toff[:, d * CHUNK:(d + 1) * CHUNK].astype(BF16), rhs.astype(BF16))
        g_last = g_c[CHUNK - 1:CHUNK, :] if d == 0 else g_c[0:1, :]
        k_tail = kf * jnp.exp(g_last - g_c)
        q_head = q.astype(F32) * eg
        results.append((uw[:, :DN_DV].astype(BF16),
                        jnp.concatenate([uw[:, DN_DV:].astype(BF16), q_head.astype(BF16)], axis=0),
                        jnp.concatenate([(qk * decay).astype(BF16), k_tail.T.astype(BF16)], axis=0),
                        jnp.broadcast_to(jnp.exp(g_last), (8, LANES))))
    return results


def _dn_prep_store(a, results, wq_ref, ak_ref, u_ref, eg_ref):
    for d in range(2):
        u_ref[d, a], wq_ref[d, a], ak_ref[d, a], eg_ref[d, a] = results[d]


def _dn_prep_kernel(ql_ref, kl_ref, vl_ref, qc_ref, kc_ref, vc_ref, wq_w_ref, wk_w_ref, wv_w_ref, bg_ref,
                    lvl_ref, wq_ref, ak_ref, u_ref, eg_ref, qs_ref, ks_ref, vs_ref, xpad_ref):
    t = ql_ref.shape[0]
    tc = qc_ref.shape[0]

    def emit_qk(dst_ref, base, scale):
        def emit(r0, y):
            yn = y * lax.rsqrt(jnp.sum(y * y, axis=-1, keepdims=True) + EPS)
            dst_ref[pl.ds(base + r0, CHUNK), :] = (yn * scale).astype(BF16)
        return emit

    def emit_v(base):
        def emit(r0, y):
            vs_ref[pl.ds(base + r0, CHUNK), :] = y
        return emit

    w_refs = (wq_w_ref, wk_w_ref, wv_w_ref)
    for srcs, tt, base in (((qc_ref, kc_ref, vc_ref), tc, 0), ((ql_ref, kl_ref, vl_ref), t, tc)):
        _dn_conv_all(srcs, w_refs, xpad_ref, tt,
                     (emit_qk(qs_ref, base, DN_DK ** -0.5), emit_qk(ks_ref, base, 1.0), emit_v(base)))

    refs = (qs_ref, ks_ref, vs_ref, bg_ref, lvl_ref)
    outs = (wq_ref, ak_ref, u_ref, eg_ref)
    n_ctx = tc // CHUNK

    def run(chunks):
        gens = [_dn_prep_chunk(a, *refs) for a in chunks]
        results = [None] * len(gens)
        while any(r is None for r in results):
            for i, gen in enumerate(gens):
                if results[i] is None:
                    try:
                        next(gen)
                    except StopIteration as done:
                        results[i] = done.value
        for a, res in zip(chunks, results):
            _dn_prep_store(a, res, *outs)

    run(list(range(n_ctx)))

    def group(gi, carry):
        run([n_ctx + gi * DN_GROUP + j for j in range(DN_GROUP)])
        return carry

    lax.fori_loop(0, t // (CHUNK * DN_GROUP), group, 0)


def _dn_masks():
    i = np.arange(CHUNK)
    lvls = []
    b = 2
    while b < CHUNK:
        rb, cb = (i // b)[:, None], (i // b)[None, :]
        fwd = ((rb == cb + 1) & (rb % 2 == 1)).astype(np.float32)
        lvls.append(np.concatenate([fwd, fwd.T], axis=1))
        b *= 2
    return jnp.asarray(np.stack(lvls))


def _dn_prep_call(qkv_l, qkv_c, conv_w, bg_all, lvl_masks):
    bsz, _, t, _ = qkv_l.shape
    tc = qkv_c.shape[2]
    ta = t + tc
    nc = ta // CHUNK

    def head(part, tt):
        return pl.BlockSpec((None, None, tt, LANES), lambda b, h: (b, part * DN_HEADS + h, 0, 0))

    def wspec(part):
        return pl.BlockSpec((None, DN_CONV, LANES), lambda b, h: (part * DN_HEADS + h, 0, 0))

    def out(rows, dt):
        return (jax.ShapeDtypeStruct((bsz, DN_HEADS, 2, nc, rows, LANES), dt),
                pl.BlockSpec((None, None, 2, nc, rows, LANES), lambda b, h: (b, h, 0, 0, 0, 0)))

    outs = [out(2 * CHUNK, BF16), out(2 * CHUNK, BF16), out(CHUNK, BF16), out(8, F32)]
    return pl.pallas_call(
        _dn_prep_kernel,
        out_shape=tuple(o[0] for o in outs),
        grid=(bsz, DN_HEADS),
        in_specs=[head(0, t), head(1, t), head(2, t), head(0, tc), head(1, tc), head(2, tc),
                  wspec(0), wspec(1), wspec(2),
                  pl.BlockSpec((None, None, nc, DN_BG_ROWS, LANES), lambda b, h: (b, h, 0, 0, 0)),
                  _const_spec(lvl_masks.shape)],
        out_specs=tuple(o[1] for o in outs),
        scratch_shapes=[pltpu.VMEM((ta, LANES), BF16), pltpu.VMEM((ta, LANES), BF16),
                        pltpu.VMEM((ta, LANES), F32), pltpu.VMEM((3, t + 16, LANES), F32)],
        compiler_params=_params(2),
        name="dn_prep",
    )(qkv_l, qkv_l, qkv_l, qkv_c, qkv_c, qkv_c, conv_w, conv_w, conv_w, bg_all, lvl_masks)


def _dn_scan_kernel(wq0_ref, wq1_ref, ak0_ref, ak1_ref, u0_ref, u1_ref, eg0_ref, eg1_ref, o0_ref, o1_ref, s_ref):
    @pl.when(pl.program_id(1) == 0)
    def _():
        s_ref[...] = jnp.zeros(s_ref.shape, F32)

    heads = range(DN_HEADS)
    s = [s_ref[h] for h in heads]
    for j in range(DN_SCAN_CHUNKS):
        jb = DN_SCAN_CHUNKS - 1 - j
        r1 = [_dot(jnp.concatenate([wq0_ref[h, j], wq1_ref[h, jb]], axis=1), _block_diag(s[h].astype(BF16)))
              for h in heads]
        v_new = [jnp.concatenate([u0_ref[h, j], u1_ref[h, jb]], axis=1).astype(F32) - r1[h][:CHUNK] for h in heads]
        r2 = [_dot(jnp.concatenate([ak0_ref[h, j], ak1_ref[h, jb]], axis=1), _block_diag(v_new[h].astype(BF16)))
              for h in heads]
        for h in heads:
            o = r1[h][CHUNK:] + r2[h][:CHUNK]
            o0_ref[j * CHUNK:(j + 1) * CHUNK, h * DN_DV:(h + 1) * DN_DV] = o[:, :DN_DV]
            o1_ref[jb * CHUNK:(jb + 1) * CHUNK, h * DN_DV:(h + 1) * DN_DV] = o[:, DN_DV:]
            dec = jnp.concatenate([eg0_ref[h, j, 0:1, :], eg1_ref[h, jb, 0:1, :]], axis=1)
            s[h] = s[h] * dec + r2[h][CHUNK:]
    for h in heads:
        s_ref[h] = s[h]


def _dn_scan_call(wq, ak, u, eg, t):
    bsz, nh, _, nc, _, _ = wq.shape
    k = DN_SCAN_CHUNKS
    nb = nc // k
    nb_ctx = (nc - t // CHUNK) // k

    def bwd_block(i):
        return jnp.where(i < nb_ctx, nb_ctx - 1 - i, nb - 1 + nb_ctx - i)

    def spec(rows, d):
        if d == 0:
            return pl.BlockSpec((None, nh, None, k, rows, LANES), lambda b, i: (b, 0, 0, i, 0, 0))
        return pl.BlockSpec((None, nh, None, k, rows, LANES), lambda b, i: (b, 0, 1, bwd_block(i), 0, 0))

    o_shape = jax.ShapeDtypeStruct((bsz, t, nh * DN_DV), F32)
    o0_spec = pl.BlockSpec((None, k * CHUNK, nh * DN_DV), lambda b, i: (b, jnp.maximum(i - nb_ctx, 0), 0))
    o1_spec = pl.BlockSpec((None, k * CHUNK, nh * DN_DV),
                           lambda b, i: (b, bwd_block(jnp.maximum(i, nb_ctx)) - nb_ctx, 0))
    return pl.pallas_call(
        _dn_scan_kernel,
        out_shape=(o_shape, o_shape),
        grid=(bsz, nb),
        in_specs=[spec(2 * CHUNK, 0), spec(2 * CHUNK, 1), spec(2 * CHUNK, 0), spec(2 * CHUNK, 1),
                  spec(CHUNK, 0), spec(CHUNK, 1), spec(8, 0), spec(8, 1)],
        out_specs=(o0_spec, o1_spec),
        scratch_shapes=[pltpu.VMEM((nh, DN_DK, 2 * DN_DV), F32)],
        compiler_params=_params(2),
        name="dn_scan",
    )(wq, wq, ak, ak, u, u, eg, eg)


def _attn_kernel(q_ref, kl_ref, vtl_ref, kc_ref, vtc_ref, o_ref, kmax_ref, shift_ref):
    tq = q_ref.shape[0]
    n = 2 * tq
    t = kl_ref.shape[0]
    blocks = [(kl_ref, vtl_ref, j * AT_KEYS, AT_KEYS) for j in range(t // AT_KEYS)]
    blocks.append((kc_ref, vtc_ref, 0, kc_ref.shape[0]))

    @pl.when(pl.program_id(2) == 0)
    def _():
        best = jnp.zeros((1, 1), F32)
        for k_ref, _, off, size in blocks:
            kk = k_ref[off:off + size, :].astype(F32)
            best = jnp.maximum(best, jnp.max(jnp.sum(kk * kk, axis=-1, keepdims=True), axis=0, keepdims=True))
        kmax_ref[...] = jnp.broadcast_to(best, kmax_ref.shape)

    q = q_ref[...].astype(F32)
    qt32 = jnp.concatenate([q[:, :AT_HD].T, q[:, AT_HD:].T], axis=1)
    qt = qt32.astype(BF16)
    bound = jnp.sqrt(jnp.sum(qt32 * qt32, axis=0, keepdims=True) * kmax_ref[0:1, 0:1])
    shift_ref[...] = bound

    def scores(blk):
        k_ref, _, off, size = blk
        return _dot(k_ref[off:off + size, :], qt)

    @pl.when(jnp.max(bound) > AT_SAFE_LOG2)
    def _():
        m = jnp.full((1, n), NEG_BIG, F32)
        for blk in blocks:
            m = jnp.maximum(m, jnp.max(scores(blk), axis=0, keepdims=True))
        shift_ref[...] = m

    shift = shift_ref[...]
    den = jnp.zeros((1, n), F32)
    acc = jnp.zeros((AT_HD, n), F32)
    s_next = scores(blocks[0])
    for j, (_, vt_ref, off, size) in enumerate(blocks):
        s = s_next
        if j + 1 < len(blocks):
            s_next = scores(blocks[j + 1])
        p = jnp.exp2(s - shift)
        den = den + jnp.sum(p, axis=0, keepdims=True)
        acc = acc + _dot(vt_ref[:, off:off + size], p.astype(BF16))
    o = (acc / den).T
    o_ref[:, :AT_HD] = o[:tq].astype(BF16)
    o_ref[:, AT_HD:] = o[tq:].astype(BF16)


def _attn_call(aq, ak_l, avt_l, ak_c, avt_c, tq):
    bsz, hkv, t, _ = aq.shape
    tc = ak_c.shape[2]
    kspec = lambda tt: pl.BlockSpec((None, None, tt, AT_HD), lambda b, j, i: (b, j, 0, 0))
    vspec = lambda tt: pl.BlockSpec((None, None, AT_HD, tt), lambda b, j, i: (b, j, 0, 0))
    return pl.pallas_call(
        _attn_kernel,
        out_shape=jax.ShapeDtypeStruct((bsz, t, AT_WIDTH), BF16),
        grid=(bsz, hkv, t // tq),
        in_specs=[pl.BlockSpec((None, None, tq, 2 * AT_HD), lambda b, j, i: (b, j, i, 0)),
                  kspec(t), vspec(t), kspec(tc), vspec(tc)],
        out_specs=pl.BlockSpec((None, tq, 2 * AT_HD), lambda b, j, i: (b, i, j)),
        scratch_shapes=[pltpu.VMEM((8, LANES), F32), pltpu.VMEM((1, 2 * tq), F32)],
        compiler_params=_params(3),
        name="attn",
    )(aq, ak_l, avt_l, ak_c, avt_c)


def _tail_kernel(x_ref, o0_ref, o1_ref, z_ref, at_ref, mod_ref, gn_ref, wo_ref, g_ref, w1_ref, w3_ref, w2_ref,
                 gf_ref, o_ref, act_ref, dn_ref):
    for hd in range(DN_HEADS):
        sl = slice(hd * DN_DV, (hd + 1) * DN_DV)
        o = o0_ref[:, sl] + o1_ref[:, sl]
        dn_ref[:, sl] = (_rms(o, gn_ref[...]) * _silu(z_ref[:, sl].astype(F32))).astype(BF16)
    mix = _dot(dn_ref[...], wo_ref[:DN_WIDTH, :]) + _dot(at_ref[...], wo_ref[DN_WIDTH:, :])
    x = x_ref[...] + mod_ref[5:6, :] * mix
    x = _swiglu_update(x, mod_ref, 6, g_ref, w1_ref, w3_ref, w2_ref, act_ref)
    o_ref[...] = _rms(x, gf_ref[...])


def _tail_call(x2d, o0, o1, z2d, at2d, mod3, mod_row, gn, wo, g, w1, w3, w2, gf, tm):
    n, d = x2d.shape
    return pl.pallas_call(
        _tail_kernel,
        out_shape=jax.ShapeDtypeStruct((n, d), F32),
        grid=(n // tm,),
        in_specs=[pl.BlockSpec((tm, d), lambda i: (i, 0)),
                  pl.BlockSpec((tm, DN_WIDTH), lambda i: (i, 0)),
                  pl.BlockSpec((tm, DN_WIDTH), lambda i: (i, 0)),
                  pl.BlockSpec((tm, DN_WIDTH), lambda i: (i, 0)),
                  pl.BlockSpec((tm, AT_WIDTH), lambda i: (i, 0)),
                  pl.BlockSpec((None, N_MOD, d), lambda i: (mod_row(i), 0, 0)),
                  _const_spec((1, LANES)),
                  _const_spec(wo.shape), _const_spec((1, d)),
                  _const_spec(w1.shape), _const_spec(w3.shape), _const_spec(w2.shape),
                  _const_spec((1, d))],
        out_specs=pl.BlockSpec((tm, d), lambda i: (i, 0)),
        scratch_shapes=[pltpu.VMEM((tm, D_FF), BF16), pltpu.VMEM((tm, DN_WIDTH), BF16)],
        compiler_params=_params(1),
        name="tail",
    )(x2d, o0, o1, z2d, at2d, mod3, gn, wo, g, w1, w3, w2, gf)


def _rope_tables(n):
    rows = n // GRID_W
    row = jnp.repeat(jnp.arange(rows, dtype=jnp.int32), GRID_W).astype(F32)
    col = jnp.tile(jnp.arange(GRID_W, dtype=jnp.int32), rows).astype(F32)
    freqs = 1.0 / (ROPE_THETA ** (jnp.arange(0, ROPE_AXIS_DIM, 2, dtype=F32) / ROPE_AXIS_DIM))
    ang = jnp.concatenate([row[:, None] * freqs, col[:, None] * freqs], axis=-1)
    cos = jnp.repeat(jnp.cos(ang), 2, axis=-1)
    sin = jnp.repeat(jnp.sin(ang), 2, axis=-1)
    sign = jnp.tile(jnp.array([-1.0, 1.0], F32), AT_HD // 2)
    return cos, sin * sign


def _head_kind_lanes(b_part, a_part):
    lead = b_part.shape[:-1]
    kinds = jnp.stack([b_part.reshape(*lead, 2, DN_HEADS), a_part.reshape(*lead, 2, DN_HEADS)], axis=-3)
    kinds = jnp.moveaxis(kinds.reshape(*lead, 4, DN_HEADS), -1, -2)
    kinds = jnp.pad(kinds, [(0, 0)] * len(lead) + [(0, 0), (0, DN_BG_ROWS - 4)])
    flat = kinds.reshape(*lead, DN_HEADS * DN_BG_ROWS)
    return jnp.pad(flat, [(0, 0)] * len(lead) + [(0, LANES - DN_HEADS * DN_BG_ROWS)])


def _reorder_w_in(w_in):
    ba = _head_kind_lanes(w_in[:, OFF_DN_B:OFF_DN_A], w_in[:, OFF_DN_A:OFF_AT_Q])
    return jnp.concatenate([w_in[:, :OFF_DN_B], w_in[:, OFF_AT_Q:], ba], axis=1).astype(BF16)


def kernel(x, c, ctx, c_ctx, w_mod, b_mod, g_ffn1, ffn1_w1, ffn1_w3, ffn1_w2, g_mix, w_in, dn_conv, dn_a_log,
           dn_dt_bias, dn_norm, q_norm, k_norm, w_out, g_ffn2, ffn2_w1, ffn2_w3, ffn2_w2, g_final):
    bsz, t, d = x.shape
    tc = ctx.shape[1]
    assert w_mod.shape[0] == 1, "single-layer block"
    tm = 512
    ctx_row = bsz

    cc = jnp.zeros((8, d), F32).at[:bsz].set(c).at[ctx_row].set(c_ctx)
    mod3 = _mod_call(cc, w_mod[0], b_mod[0][None]).reshape(8, N_MOD, d)

    lat_row = lambda i: i // (t // tm)
    ctx_rowf = lambda i: ctx_row
    row = lambda v: v.reshape(1, -1)

    w1a, w3a, w2a = ffn1_w1[0].astype(BF16), ffn1_w3[0].astype(BF16), ffn1_w2[0].astype(BF16)
    x1 = _ffn_call(x.reshape(bsz * t, d), mod3, lat_row, row(g_ffn1[0]), w1a, w3a, w2a, tm, "ffn1_lat")
    c1 = _ffn_call(ctx.reshape(bsz * tc, d), mod3, ctx_rowf, row(g_ffn1[0]), w1a, w3a, w2a, tc, "ffn1_ctx")

    w_r = _reorder_w_in(w_in[0])
    no_b = jnp.zeros((1, 2 * DN_HEADS), F32)
    alog_row = _head_kind_lanes(no_b, dn_a_log[0].reshape(1, -1))
    dtb_row = _head_kind_lanes(no_b, dn_dt_bias[0].reshape(1, -1))
    cos, sin = _rope_tables(t)
    qn, kn = row(q_norm[0]), row(k_norm[0])
    qkv_l, z_l, bg_l, aq_l, ak_l, av_l = _inproj_call(
        x1, mod3, lat_row, row(g_mix[0]), w_r, alog_row, dtb_row, qn, kn, cos, sin, bsz, t, tm, True, "inproj_lat")
    qkv_c, _, bg_c, _, ak_c, av_c = _inproj_call(
        c1, mod3, ctx_rowf, row(g_mix[0]), w_r, alog_row, dtb_row, qn, kn, cos[:tc], sin[:tc], bsz, tc, tc, False,
        "inproj_ctx")

    bg_all = jnp.concatenate([bg_c, bg_l], axis=2)
    bg_all = bg_all.reshape(bsz, DN_HEADS, DN_BG_ROWS, (tc + t) // CHUNK, CHUNK).transpose(0, 1, 3, 2, 4)
    conv_w = dn_conv[0].reshape(DN_CONV, 3 * DN_HEADS, LANES).transpose(1, 0, 2)
    dn_wq, dn_ak, dn_u, dn_eg = _dn_prep_call(qkv_l, qkv_c, conv_w, bg_all, _dn_masks())
    o_fwd, o_bwd = _dn_scan_call(dn_wq, dn_ak, dn_u, dn_eg, t)
    at_lat = _attn_call(aq_l, ak_l, av_l, ak_c, av_c, 512)

    out = _tail_call(x1, o_fwd.reshape(bsz * t, DN_WIDTH), o_bwd.reshape(bsz * t, DN_WIDTH), z_l,
                     at_lat.reshape(bsz * t, AT_WIDTH), mod3, lat_row, row(dn_norm[0]), w_out[0].astype(BF16),
                     row(g_ffn2[0]), ffn2_w1[0].astype(BF16), ffn2_w3[0].astype(BF16), ffn2_w2[0].astype(BF16),
                     row(g_final), tm)
    return out.reshape(bsz, t, d)
```

```python
import functools

import numpy as np
import jax
import jax.numpy as jnp
from jax import lax
from jax.experimental import pallas as pl
from jax.experimental.pallas import tpu as pltpu

F32 = jnp.float32
BF16 = jnp.bfloat16

D_MODEL = 1024
CTX_LEN = 256
GRID_W = 64
EPS = 1e-6
N_MOD = 9
D_FF = 2816
DN_HEADS = 4
DN_DK = 128
DN_DV = 128
DN_WIDTH = DN_HEADS * DN_DV
DN_CONV = 5
AT_HEADS = 4
AT_KV_HEADS = 2
AT_HD = 128
AT_WIDTH = AT_HEADS * AT_HD
ATT_SCALE = AT_HD ** -0.5
ROPE_AXIS_DIM = AT_HD // 2
ROPE_THETA = 10000.0
LEN_DN_QKV = 3 * DN_WIDTH
OFF_DN_Z = LEN_DN_QKV
OFF_DN_B = OFF_DN_Z + DN_WIDTH
OFF_DN_A = OFF_DN_B + 2 * DN_HEADS
OFF_AT_Q = OFF_DN_A + 2 * DN_HEADS
OFF_AT_K = OFF_AT_Q + AT_WIDTH
OFF_AT_V = OFF_AT_K + AT_KV_HEADS * AT_HD
P_IN = OFF_AT_V + AT_KV_HEADS * AT_HD

LANES = 128
CHUNK = 128
FF_TILE = 256
ROW_SPLIT = 2
DN_GROUP = 17
DN_BG_ROWS = 8
DN_SCAN_CHUNKS = 2
DN_CONV_BLOCKS = 2
AT_KEYS = 512
LOG2E = 1.4426950408889634
AT_SAFE_LOG2 = 60.0
VMEM_LIMIT = 56 * 1024 * 1024
NEG_BIG = -1e30

_NT = (((1,), (1,)), ((), ()))


def _sigmoid(x):
    return 1.0 / (1.0 + jnp.exp(-x))


def _silu(x):
    return x * _sigmoid(x)


def _rms(x, gain):
    ms = jnp.mean(x * x, axis=-1, keepdims=True)
    return x * lax.rsqrt(ms + EPS) * gain


def _dot(a, b):
    return jnp.dot(a, b, preferred_element_type=F32)


def _const_spec(shape):
    nd = len(shape)
    return pl.BlockSpec(shape, lambda *_: (0,) * nd, pipeline_mode=pl.Buffered(1))


def _params(n_axes):
    return pltpu.CompilerParams(dimension_semantics=("arbitrary",) * n_axes,
                                vmem_limit_bytes=VMEM_LIMIT)


def _mod_kernel(c_ref, w_ref, b_ref, o_ref):
    s = _silu(c_ref[...]).astype(BF16)
    o_ref[...] = _dot(s, w_ref[...].astype(BF16)) + b_ref[...]


def _mod_call(cc, w_mod, b_mod):
    d = cc.shape[1]
    n = w_mod.shape[1]
    return pl.pallas_call(
        _mod_kernel,
        out_shape=jax.ShapeDtypeStruct((cc.shape[0], n), F32),
        grid=(n // d,),
        in_specs=[pl.BlockSpec(cc.shape, lambda j: (0, 0)),
                  pl.BlockSpec((d, d), lambda j: (0, j)),
                  pl.BlockSpec((1, d), lambda j: (0, j))],
        out_specs=pl.BlockSpec((cc.shape[0], d), lambda j: (0, j)),
        compiler_params=_params(1),
        name="mod",
    )(cc, w_mod, b_mod)


def _lockstep(gens):
    results = [None] * len(gens)
    alive = list(range(len(gens)))
    while alive:
        still = []
        for i in alive:
            try:
                next(gens[i])
                still.append(i)
            except StopIteration as done:
                results[i] = done.value
        alive = still
    return results


def _row_slices(tm):
    sub = tm // ROW_SPLIT
    return [slice(r * sub, (r + 1) * sub) for r in range(ROW_SPLIT)]


def _swiglu_update(x, mod_ref, j0, g_ref, w1_ref, w3_ref, w2_ref, act_ref, rows):
    h = (_rms(x, g_ref[...]) * (1.0 + mod_ref[j0 + 1:j0 + 2, :]) + mod_ref[j0:j0 + 1, :]).astype(BF16)
    for c in range(D_FF // FF_TILE):
        sl = slice(c * FF_TILE, (c + 1) * FF_TILE)
        a = _dot(h, w1_ref[:, sl])
        b = _dot(h, w3_ref[:, sl])
        yield
        act_ref[rows, sl] = (_silu(a) * b).astype(BF16)
    y = _dot(act_ref[rows, :], w2_ref[...])
    yield
    return x + (0.5 * mod_ref[j0 + 2:j0 + 3, :]) * y


def _ffn_kernel(x_ref, mod_ref, g_ref, w1_ref, w3_ref, w2_ref, o_ref, act_ref):
    slices = _row_slices(x_ref.shape[0])
    outs = _lockstep([_swiglu_update(x_ref[rows, :], mod_ref, 0, g_ref, w1_ref, w3_ref, w2_ref, act_ref, rows)
                      for rows in slices])
    for rows, out in zip(slices, outs):
        o_ref[rows, :] = out


def _ffn_call(x2d, mod3, mod_row, g, w1, w3, w2, tm, name):
    n, d = x2d.shape
    return pl.pallas_call(
        _ffn_kernel,
        out_shape=jax.ShapeDtypeStruct((n, d), F32),
        grid=(n // tm,),
        in_specs=[pl.BlockSpec((tm, d), lambda i: (i, 0)),
                  pl.BlockSpec((None, N_MOD, d), lambda i: (mod_row(i), 0, 0)),
                  _const_spec((1, d)),
                  _const_spec(w1.shape), _const_spec(w3.shape), _const_spec(w2.shape)],
        out_specs=pl.BlockSpec((tm, d), lambda i: (i, 0)),
        scratch_shapes=[pltpu.VMEM((tm, D_FF), BF16)],
        compiler_params=_params(1),
        name=name,
    )(x2d, mod3, g, w1, w3, w2)


def _rope(x, cos, sin_signed, even):
    nxt = pltpu.roll(x, LANES - 1, axis=1)
    prv = pltpu.roll(x, 1, axis=1)
    return x * cos + jnp.where(even, nxt, prv) * sin_signed


def _inproj_kernel(x_ref, mod_ref, g_ref, w_ref, alog_ref, dtb_ref, qn_ref, kn_ref, cos_ref, sin_ref,
                   qkv_ref, z_ref, bg_ref, aq_ref, ak_ref, av_ref, *, rope):
    tm = x_ref.shape[0]
    x = x_ref[...]
    h = (_rms(x, g_ref[...]) * (1.0 + mod_ref[4:5, :]) + mod_ref[3:4, :]).astype(BF16)

    off_z = LEN_DN_QKV
    off_q = off_z + DN_WIDTH
    off_kv = off_q + AT_WIDTH
    off_ba = off_kv + 2 * AT_KV_HEADS * AT_HD
    p_ba = _dot(h, w_ref[:, off_ba:off_ba + LANES])
    p_q = _dot(h, w_ref[:, off_q:off_q + AT_WIDTH])

    beta = _sigmoid(p_ba)
    t = p_ba + dtb_ref[...]
    softplus = jnp.maximum(t, 0.0) + jnp.log(1.0 + jnp.exp(-jnp.abs(t)))
    g = -jnp.exp(alog_ref[...]) * softplus
    ri = lax.broadcasted_iota(jnp.int32, (CHUNK, CHUNK), 0)
    ci = lax.broadcasted_iota(jnp.int32, (CHUNK, CHUNK), 1)
    ltri = jnp.where(ri >= ci, 1.0, 0.0).astype(BF16)
    kind = lax.broadcasted_iota(jnp.int32, (CHUNK, LANES), 1) % DN_BG_ROWS
    for c in range(tm // CHUNK):
        rows = slice(c * CHUNK, (c + 1) * CHUNK)
        gc = g[rows]
        g1 = gc.astype(BF16)
        r1 = gc - g1.astype(F32)
        g2 = r1.astype(BF16)
        g3 = (r1 - g2.astype(F32)).astype(BF16)
        pre = _dot(ltri, g1) + _dot(ltri, g2) + _dot(ltri, g3)
        suf = pre[CHUNK - 1:CHUNK, :] - pre + gc
        tile = jnp.where(kind < 2, beta[rows], jnp.where(kind == 2, pre, suf))
        bg_ref[:, rows] = tile.T[:DN_HEADS * DN_BG_ROWS]

    p_kv = _dot(h, w_ref[:, off_kv:off_kv + 2 * AT_KV_HEADS * AT_HD])
    lane = lax.broadcasted_iota(jnp.int32, (tm, LANES), 1)
    even = (lane % 2) == 0
    if rope:
        cos = cos_ref[...]
        sin = sin_ref[...]
    for hd in range(AT_HEADS):
        q = _rms(p_q[:, hd * LANES:(hd + 1) * LANES], qn_ref[...])
        if rope:
            q = _rope(q, cos, sin, even)
        aq_ref[hd // 2, :, (hd % 2) * LANES:(hd % 2 + 1) * LANES] = (q * (ATT_SCALE * LOG2E)).astype(BF16)
    for hd in range(AT_KV_HEADS):
        k = _rms(p_kv[:, hd * LANES:(hd + 1) * LANES], kn_ref[...])
        if rope:
            k = _rope(k, cos, sin, even)
        ak_ref[hd] = k.astype(BF16)
        av_ref[hd] = p_kv[:, (AT_KV_HEADS + hd) * LANES:(AT_KV_HEADS + hd + 1) * LANES].T.astype(BF16)

    z_ref[...] = _dot(h, w_ref[:, off_z:off_z + DN_WIDTH]).astype(BF16)
    for part in range(3):
        p = _dot(h, w_ref[:, part * DN_WIDTH:(part + 1) * DN_WIDTH])
        for hd in range(DN_HEADS):
            qkv_ref[part * DN_HEADS + hd] = p[:, hd * LANES:(hd + 1) * LANES].astype(BF16)


def _inproj_call(x2d, mod3, mod_row, g, w_r, alog_row, dtb_row, qn, kn, cos, sin, bsz, t, tm, rope, name):
    n, d = x2d.shape
    tpb = t // tm
    bt = lambda i: (i // tpb, 0, i % tpb, 0)
    out_shape = (
        jax.ShapeDtypeStruct((bsz, 3 * DN_HEADS, t, LANES), BF16),
        jax.ShapeDtypeStruct((n, DN_WIDTH), BF16),
        jax.ShapeDtypeStruct((bsz, DN_HEADS * DN_BG_ROWS, t), F32),
        jax.ShapeDtypeStruct((bsz, AT_KV_HEADS, t, 2 * AT_HD), BF16),
        jax.ShapeDtypeStruct((bsz, AT_KV_HEADS, t, AT_HD), BF16),
        jax.ShapeDtypeStruct((bsz, AT_KV_HEADS, AT_HD, t), BF16),
    )
    out_specs = (
        pl.BlockSpec((None, 3 * DN_HEADS, tm, LANES), bt),
        pl.BlockSpec((tm, DN_WIDTH), lambda i: (i, 0)),
        pl.BlockSpec((None, DN_HEADS * DN_BG_ROWS, tm), lambda i: (i // tpb, 0, i % tpb)),
        pl.BlockSpec((None, AT_KV_HEADS, tm, 2 * AT_HD), bt),
        pl.BlockSpec((None, AT_KV_HEADS, tm, AT_HD), bt),
        pl.BlockSpec((None, AT_KV_HEADS, AT_HD, tm), lambda i: (i // tpb, 0, 0, i % tpb)),
    )
    return pl.pallas_call(
        functools.partial(_inproj_kernel, rope=rope),
        out_shape=out_shape,
        grid=(n // tm,),
        in_specs=[pl.BlockSpec((tm, d), lambda i: (i, 0)),
                  pl.BlockSpec((None, N_MOD, d), lambda i: (mod_row(i), 0, 0)),
                  _const_spec((1, d)),
                  _const_spec(w_r.shape),
                  _const_spec((1, LANES)), _const_spec((1, LANES)),
                  _const_spec((1, LANES)), _const_spec((1, LANES)),
                  pl.BlockSpec((tm, LANES), lambda i: (i % tpb, 0)),
                  pl.BlockSpec((tm, LANES), lambda i: (i % tpb, 0))],
        out_specs=out_specs,
        compiler_params=_params(1),
        name=name,
    )(x2d, mod3, g, w_r, alog_row, dtb_row, qn, kn, cos, sin)


def _dn_conv_all(src_refs, w_refs, xpad_ref, t, emits):
    nblk = t // CHUNK
    n = len(src_refs)
    for i in range(n):
        xpad_ref[i, 0:8, :] = jnp.zeros((8, LANES), F32)
        xpad_ref[i, 8 + t:16 + t, :] = jnp.zeros((8, LANES), F32)

    def fill(r, carry):
        r0 = pl.multiple_of(r * CHUNK, CHUNK)
        for i in range(n):
            xpad_ref[i, pl.ds(r0 + 8, CHUNK), :] = src_refs[i][pl.ds(r0, CHUNK), :].astype(F32)
        return carry

    lax.fori_loop(0, nblk, fill, 0)
    ws = [w_ref[...] for w_ref in w_refs]
    pad = (DN_CONV - 1) // 2
    win = CHUNK + 16

    def conv(r, carry):
        for u in range(DN_CONV_BLOCKS):
            r0 = pl.multiple_of((r * DN_CONV_BLOCKS + u) * CHUNK, CHUNK)
            for i in range(n):
                acc = None
                for j in range(DN_CONV):
                    tap = xpad_ref[i, pl.ds(r0 + (8 - pad + j), CHUNK), :] * ws[i][j:j + 1, :]
                    acc = tap if acc is None else acc + tap
                emits[i](r0, _silu(acc))
        return carry

    lax.fori_loop(0, nblk // DN_CONV_BLOCKS, conv, 0)


def _block_diag(cat):
    c = cat.shape[0]
    zero = jnp.zeros((c, c), cat.dtype)
    return jnp.concatenate([jnp.concatenate([cat[:, :c], zero], axis=1),
                            jnp.concatenate([zero, cat[:, c:]], axis=1)], axis=0)


def _dn_prep_chunk(a, qs_ref, ks_ref, vs_ref, bg_ref, lvl_ref, wq_ref, ak_ref, u_ref, eg_ref):
    row = pl.multiple_of(a * CHUNK, CHUNK)
    k = ks_ref[pl.ds(row, CHUNK), :]
    q = qs_ref[pl.ds(row, CHUNK), :]
    v = vs_ref[pl.ds(row, CHUNK), :]
    bg = bg_ref[a]
    kf = k.astype(F32)
    ri = lax.broadcasted_iota(jnp.int32, (CHUNK, CHUNK), 0)
    ci = lax.broadcasted_iota(jnp.int32, (CHUNK, CHUNK), 1)
    per_dir = []
    for d in range(2):
        beta = jnp.broadcast_to(bg[d:d + 1, :], (CHUNK, CHUNK)).T
        g_r = jnp.broadcast_to(bg[2 + d:3 + d, :], (CHUNK, CHUNK))
        g_c = g_r.T
        incl = (ri >= ci) if d == 0 else (ri <= ci)
        decay = jnp.exp(jnp.where(incl, g_c - g_r, NEG_BIG))
        per_dir.append((beta, g_c, decay, kf * beta))
    aq = lax.dot_general(jnp.concatenate([per_dir[0][3].astype(BF16), per_dir[1][3].astype(BF16), q], axis=0),
                         k, _NT, preferred_element_type=F32)
    yield
    qk = aq[2 * CHUNK:]
    a_cat = jnp.concatenate([jnp.where(ri > ci, aq[:CHUNK] * per_dir[0][2], 0.0),
                             jnp.where(ri < ci, aq[CHUNK:2 * CHUNK] * per_dir[1][2], 0.0)], axis=1)

    rhs = []
    for d in range(2):
        beta, g_c, decay, kb = per_dir[d]
        eg = jnp.exp(g_c)
        g_last = g_c[CHUNK - 1:CHUNK, :] if d == 0 else g_c[0:1, :]
        k_tail = kf * jnp.exp(g_last - g_c)
        ak_ref[d, a] = jnp.concatenate([(qk * decay).astype(BF16), k_tail.T.astype(BF16)], axis=0)
        wq_ref[d, a, CHUNK:, :] = (q.astype(F32) * eg).astype(BF16)
        eg_ref[d, a] = jnp.broadcast_to(jnp.exp(g_last), (8, LANES))
        rhs.append(jnp.concatenate([v * beta, kb * eg], axis=1))

    eye = jnp.where(ri == ci, 1.0, 0.0)
    eye_cat = jnp.concatenate([eye, eye], axis=1)
    pair = jnp.where((ri >> 1) == (ci >> 1), 1.0, 0.0)
    t_cat = eye_cat - a_cat * jnp.concatenate([pair, pair], axis=1)
    for lvl in range(lvl_ref.shape[0]):
        a_sub = _block_diag((a_cat * lvl_ref[lvl]).astype(BF16))
        x = _dot(t_cat.astype(BF16), a_sub)
        yield
        t_cat = t_cat - _dot(x.astype(BF16), _block_diag(t_cat.astype(BF16)))
        yield
    toff = t_cat - eye_cat

    for d in range(2):
        uw = rhs[d] + _dot(toff[:, d * CHUNK:(d + 1) * CHUNK].astype(BF16), rhs[d].astype(BF16))
        u_ref[d, a] = uw[:, :DN_DV].astype(BF16)
        wq_ref[d, a, :CHUNK, :] = uw[:, DN_DV:].astype(BF16)


def _dn_prep_kernel(ql_ref, kl_ref, vl_ref, qc_ref, kc_ref, vc_ref, wq_w_ref, wk_w_ref, wv_w_ref, bg_ref,
                    lvl_ref, wq_ref, ak_ref, u_ref, eg_ref, qs_ref, ks_ref, vs_ref, xpad_ref):
    t = ql_ref.shape[0]
    tc = qc_ref.shape[0]

    def emit_qk(dst_ref, base, scale):
        def emit(r0, y):
            yn = y * lax.rsqrt(jnp.sum(y * y, axis=-1, keepdims=True) + EPS)
            dst_ref[pl.ds(base + r0, CHUNK), :] = (yn * scale).astype(BF16)
        return emit

    def emit_v(base):
        def emit(r0, y):
            vs_ref[pl.ds(base + r0, CHUNK), :] = y
        return emit

    w_refs = (wq_w_ref, wk_w_ref, wv_w_ref)
    for srcs, tt, base in (((qc_ref, kc_ref, vc_ref), tc, 0), ((ql_ref, kl_ref, vl_ref), t, tc)):
        _dn_conv_all(srcs, w_refs, xpad_ref, tt,
                     (emit_qk(qs_ref, base, DN_DK ** -0.5), emit_qk(ks_ref, base, 1.0), emit_v(base)))

    refs = (qs_ref, ks_ref, vs_ref, bg_ref, lvl_ref, wq_ref, ak_ref, u_ref, eg_ref)

    def group(gi, carry):
        _lockstep([_dn_prep_chunk(gi * DN_GROUP + j, *refs) for j in range(DN_GROUP)])
        return carry

    lax.fori_loop(0, (tc + t) // (CHUNK * DN_GROUP), group, 0)


def _dn_masks():
    i = np.arange(CHUNK)
    lvls = []
    b = 2
    while b < CHUNK:
        rb, cb = (i // b)[:, None], (i // b)[None, :]
        fwd = ((rb == cb + 1) & (rb % 2 == 1)).astype(np.float32)
        lvls.append(np.concatenate([fwd, fwd.T], axis=1))
        b *= 2
    return jnp.asarray(np.stack(lvls))


def _dn_prep_call(qkv_l, qkv_c, conv_w, bg_all, lvl_masks):
    bsz, _, t, _ = qkv_l.shape
    tc = qkv_c.shape[2]
    ta = t + tc
    nc = ta // CHUNK

    def head(part, tt):
        return pl.BlockSpec((None, None, tt, LANES), lambda b, h: (b, part * DN_HEADS + h, 0, 0))

    def wspec(part):
        return pl.BlockSpec((None, DN_CONV, LANES), lambda b, h: (part * DN_HEADS + h, 0, 0))

    def out(rows, dt):
        return (jax.ShapeDtypeStruct((bsz, DN_HEADS, 2, nc, rows, LANES), dt),
                pl.BlockSpec((None, None, 2, nc, rows, LANES), lambda b, h: (b, h, 0, 0, 0, 0)))

    outs = [out(2 * CHUNK, BF16), out(2 * CHUNK, BF16), out(CHUNK, BF16), out(8, F32)]
    return pl.pallas_call(
        _dn_prep_kernel,
        out_shape=tuple(o[0] for o in outs),
        grid=(bsz, DN_HEADS),
        in_specs=[head(0, t), head(1, t), head(2, t), head(0, tc), head(1, tc), head(2, tc),
                  wspec(0), wspec(1), wspec(2),
                  pl.BlockSpec((None, None, nc, DN_BG_ROWS, LANES), lambda b, h: (b, h, 0, 0, 0)),
                  _const_spec(lvl_masks.shape)],
        out_specs=tuple(o[1] for o in outs),
        scratch_shapes=[pltpu.VMEM((ta, LANES), BF16), pltpu.VMEM((ta, LANES), BF16),
                        pltpu.VMEM((ta, LANES), F32), pltpu.VMEM((3, t + 16, LANES), F32)],
        compiler_params=_params(2),
        name="dn_prep",
    )(qkv_l, qkv_l, qkv_l, qkv_c, qkv_c, qkv_c, conv_w, conv_w, conv_w, bg_all, lvl_masks)


def _dn_scan_kernel(wq0_ref, wq1_ref, ak0_ref, ak1_ref, u0_ref, u1_ref, eg0_ref, eg1_ref, o0_ref, o1_ref, s_ref):
    @pl.when(pl.program_id(1) == 0)
    def _():
        s_ref[...] = jnp.zeros(s_ref.shape, F32)

    heads = range(DN_HEADS)
    s = [s_ref[h] for h in heads]
    for j in range(DN_SCAN_CHUNKS):
        jb = DN_SCAN_CHUNKS - 1 - j
        r1 = [_dot(jnp.concatenate([wq0_ref[h, j], wq1_ref[h, jb]], axis=1), _block_diag(s[h].astype(BF16)))
              for h in heads]
        v_new = [jnp.concatenate([u0_ref[h, j], u1_ref[h, jb]], axis=1).astype(F32) - r1[h][:CHUNK] for h in heads]
        r2 = [_dot(jnp.concatenate([ak0_ref[h, j], ak1_ref[h, jb]], axis=1), _block_diag(v_new[h].astype(BF16)))
              for h in heads]
        for h in heads:
            o = r1[h][CHUNK:] + r2[h][:CHUNK]
            o0_ref[j * CHUNK:(j + 1) * CHUNK, h * DN_DV:(h + 1) * DN_DV] = o[:, :DN_DV]
            o1_ref[jb * CHUNK:(jb + 1) * CHUNK, h * DN_DV:(h + 1) * DN_DV] = o[:, DN_DV:]
            dec = jnp.concatenate([eg0_ref[h, j, 0:1, :], eg1_ref[h, jb, 0:1, :]], axis=1)
            s[h] = s[h] * dec + r2[h][CHUNK:]
    for h in heads:
        s_ref[h] = s[h]


def _dn_scan_call(wq, ak, u, eg, t):
    bsz, nh, _, nc, _, _ = wq.shape
    k = DN_SCAN_CHUNKS
    nb = nc // k
    nb_ctx = (nc - t // CHUNK) // k

    def bwd_block(i):
        return jnp.where(i < nb_ctx, nb_ctx - 1 - i, nb - 1 + nb_ctx - i)

    def spec(rows, d):
        if d == 0:
            return pl.BlockSpec((None, nh, None, k, rows, LANES), lambda b, i: (b, 0, 0, i, 0, 0))
        return pl.BlockSpec((None, nh, None, k, rows, LANES), lambda b, i: (b, 0, 1, bwd_block(i), 0, 0))

    o_shape = jax.ShapeDtypeStruct((bsz, t, nh * DN_DV), F32)
    o0_spec = pl.BlockSpec((None, k * CHUNK, nh * DN_DV), lambda b, i: (b, jnp.maximum(i - nb_ctx, 0), 0))
    o1_spec = pl.BlockSpec((None, k * CHUNK, nh * DN_DV),
                           lambda b, i: (b, bwd_block(jnp.maximum(i, nb_ctx)) - nb_ctx, 0))
    return pl.pallas_call(
        _dn_scan_kernel,
        out_shape=(o_shape, o_shape),
        grid=(bsz, nb),
        in_specs=[spec(2 * CHUNK, 0), spec(2 * CHUNK, 1), spec(2 * CHUNK, 0), spec(2 * CHUNK, 1),
                  spec(CHUNK, 0), spec(CHUNK, 1), spec(8, 0), spec(8, 1)],
        out_specs=(o0_spec, o1_spec),
        scratch_shapes=[pltpu.VMEM((nh, DN_DK, 2 * DN_DV), F32)],
        compiler_params=_params(2),
        name="dn_scan",
    )(wq, wq, ak, ak, u, u, eg, eg)


def _attn_kernel(q_ref, kl_ref, vtl_ref, kc_ref, vtc_ref, o_ref, kmax_ref, shift_ref):
    tq = q_ref.shape[0]
    n = 2 * tq
    t = kl_ref.shape[0]
    blocks = [(kl_ref, vtl_ref, j * AT_KEYS, AT_KEYS) for j in range(t // AT_KEYS)]
    blocks.append((kc_ref, vtc_ref, 0, kc_ref.shape[0]))

    @pl.when(pl.program_id(2) == 0)
    def _():
        best = jnp.zeros((1, 1), F32)
        for k_ref, _, off, size in blocks:
            kk = k_ref[off:off + size, :].astype(F32)
            best = jnp.maximum(best, jnp.max(jnp.sum(kk * kk, axis=-1, keepdims=True), axis=0, keepdims=True))
        kmax_ref[...] = jnp.broadcast_to(best, kmax_ref.shape)

    q = q_ref[...].astype(F32)
    qt32 = jnp.concatenate([q[:, :AT_HD].T, q[:, AT_HD:].T], axis=1)
    qt = qt32.astype(BF16)
    bound = jnp.sqrt(jnp.sum(qt32 * qt32, axis=0, keepdims=True) * kmax_ref[0:1, 0:1])
    shift_ref[...] = bound

    def scores(blk):
        k_ref, _, off, size = blk
        return _dot(k_ref[off:off + size, :], qt)

    @pl.when(jnp.max(bound) > AT_SAFE_LOG2)
    def _():
        m = jnp.full((1, n), NEG_BIG, F32)
        for blk in blocks:
            m = jnp.maximum(m, jnp.max(scores(blk), axis=0, keepdims=True))
        shift_ref[...] = m

    shift = shift_ref[...]
    den = jnp.zeros((1, n), F32)
    acc = jnp.zeros((AT_HD, n), F32)
    s_next = scores(blocks[0])
    for j, (_, vt_ref, off, size) in enumerate(blocks):
        s = s_next
        if j + 1 < len(blocks):
            s_next = scores(blocks[j + 1])
        p = jnp.exp2(s - shift)
        den = den + jnp.sum(p, axis=0, keepdims=True)
        acc = acc + _dot(vt_ref[:, off:off + size], p.astype(BF16))
    o = (acc / den).T
    o_ref[:, :AT_HD] = o[:tq].astype(BF16)
    o_ref[:, AT_HD:] = o[tq:].astype(BF16)


def _attn_call(aq, ak_l, avt_l, ak_c, avt_c, tq):
    bsz, hkv, t, _ = aq.shape
    tc = ak_c.shape[2]
    kspec = lambda tt: pl.BlockSpec((None, None, tt, AT_HD), lambda b, j, i: (b, j, 0, 0))
    vspec = lambda tt: pl.BlockSpec((None, None, AT_HD, tt), lambda b, j, i: (b, j, 0, 0))
    return pl.pallas_call(
        _attn_kernel,
        out_shape=jax.ShapeDtypeStruct((bsz, t, AT_WIDTH), BF16),
        grid=(bsz, hkv, t // tq),
        in_specs=[pl.BlockSpec((None, None, tq, 2 * AT_HD), lambda b, j, i: (b, j, i, 0)),
                  kspec(t), vspec(t), kspec(tc), vspec(tc)],
        out_specs=pl.BlockSpec((None, tq, 2 * AT_HD), lambda b, j, i: (b, i, j)),
        scratch_shapes=[pltpu.VMEM((8, LANES), F32), pltpu.VMEM((1, 2 * tq), F32)],
        compiler_params=_params(3),
        name="attn",
    )(aq, ak_l, avt_l, ak_c, avt_c)


def _tail_kernel(x_ref, o0_ref, o1_ref, z_ref, at_ref, mod_ref, gn_ref, wo_ref, g_ref, w1_ref, w3_ref, w2_ref,
                 gf_ref, o_ref, act_ref, dn_ref):
    def rows_gen(rows):
        for hd in range(DN_HEADS):
            sl = slice(hd * DN_DV, (hd + 1) * DN_DV)
            o = o0_ref[rows, sl] + o1_ref[rows, sl]
            dn_ref[rows, sl] = (_rms(o, gn_ref[...]) * _silu(z_ref[rows, sl].astype(F32))).astype(BF16)
        mix = _dot(dn_ref[rows, :], wo_ref[:DN_WIDTH, :]) + _dot(at_ref[rows, :], wo_ref[DN_WIDTH:, :])
        yield
        x = x_ref[rows, :] + mod_ref[5:6, :] * mix
        x = yield from _swiglu_update(x, mod_ref, 6, g_ref, w1_ref, w3_ref, w2_ref, act_ref, rows)
        o_ref[rows, :] = _rms(x, gf_ref[...])

    _lockstep([rows_gen(rows) for rows in _row_slices(x_ref.shape[0])])


def _tail_call(x2d, o0, o1, z2d, at2d, mod3, mod_row, gn, wo, g, w1, w3, w2, gf, tm):
    n, d = x2d.shape
    return pl.pallas_call(
        _tail_kernel,
        out_shape=jax.ShapeDtypeStruct((n, d), F32),
        grid=(n // tm,),
        in_specs=[pl.BlockSpec((tm, d), lambda i: (i, 0)),
                  pl.BlockSpec((tm, DN_WIDTH), lambda i: (i, 0)),
                  pl.BlockSpec((tm, DN_WIDTH), lambda i: (i, 0)),
                  pl.BlockSpec((tm, DN_WIDTH), lambda i: (i, 0)),
                  pl.BlockSpec((tm, AT_WIDTH), lambda i: (i, 0)),
                  pl.BlockSpec((None, N_MOD, d), lambda i: (mod_row(i), 0, 0)),
                  _const_spec((1, LANES)),
                  _const_spec(wo.shape), _const_spec((1, d)),
                  _const_spec(w1.shape), _const_spec(w3.shape), _const_spec(w2.shape),
                  _const_spec((1, d))],
        out_specs=pl.BlockSpec((tm, d), lambda i: (i, 0)),
        scratch_shapes=[pltpu.VMEM((tm, D_FF), BF16), pltpu.VMEM((tm, DN_WIDTH), BF16)],
        compiler_params=_params(1),
        name="tail",
    )(x2d, o0, o1, z2d, at2d, mod3, gn, wo, g, w1, w3, w2, gf)


def _rope_tables(n):
    rows = n // GRID_W
    row = jnp.repeat(jnp.arange(rows, dtype=jnp.int32), GRID_W).astype(F32)
    col = jnp.tile(jnp.arange(GRID_W, dtype=jnp.int32), rows).astype(F32)
    freqs = 1.0 / (ROPE_THETA ** (jnp.arange(0, ROPE_AXIS_DIM, 2, dtype=F32) / ROPE_AXIS_DIM))
    ang = jnp.concatenate([row[:, None] * freqs, col[:, None] * freqs], axis=-1)
    cos = jnp.repeat(jnp.cos(ang), 2, axis=-1)
    sin = jnp.repeat(jnp.sin(ang), 2, axis=-1)
    sign = jnp.tile(jnp.array([-1.0, 1.0], F32), AT_HD // 2)
    return cos, sin * sign


def _head_kind_lanes(b_part, a_part):
    lead = b_part.shape[:-1]
    kinds = jnp.stack([b_part.reshape(*lead, 2, DN_HEADS), a_part.reshape(*lead, 2, DN_HEADS)], axis=-3)
    kinds = jnp.moveaxis(kinds.reshape(*lead, 4, DN_HEADS), -1, -2)
    kinds = jnp.pad(kinds, [(0, 0)] * len(lead) + [(0, 0), (0, DN_BG_ROWS - 4)])
    flat = kinds.reshape(*lead, DN_HEADS * DN_BG_ROWS)
    return jnp.pad(flat, [(0, 0)] * len(lead) + [(0, LANES - DN_HEADS * DN_BG_ROWS)])


def _reorder_w_in(w_in):
    ba = _head_kind_lanes(w_in[:, OFF_DN_B:OFF_DN_A], w_in[:, OFF_DN_A:OFF_AT_Q])
    return jnp.concatenate([w_in[:, :OFF_DN_B], w_in[:, OFF_AT_Q:], ba], axis=1).astype(BF16)


def kernel(x, c, ctx, c_ctx, w_mod, b_mod, g_ffn1, ffn1_w1, ffn1_w3, ffn1_w2, g_mix, w_in, dn_conv, dn_a_log,
           dn_dt_bias, dn_norm, q_norm, k_norm, w_out, g_ffn2, ffn2_w1, ffn2_w3, ffn2_w2, g_final):
    bsz, t, d = x.shape
    tc = ctx.shape[1]
    assert w_mod.shape[0] == 1, "single-layer block"
    tm = 512
    ctx_row = bsz

    cc = jnp.zeros((8, d), F32).at[:bsz].set(c).at[ctx_row].set(c_ctx)
    mod3 = _mod_call(cc, w_mod[0], b_mod[0][None]).reshape(8, N_MOD, d)

    lat_row = lambda i: i // (t // tm)
    ctx_rowf = lambda i: ctx_row
    row = lambda v: v.reshape(1, -1)

    w1a, w3a, w2a = ffn1_w1[0].astype(BF16), ffn1_w3[0].astype(BF16), ffn1_w2[0].astype(BF16)
    x1 = _ffn_call(x.reshape(bsz * t, d), mod3, lat_row, row(g_ffn1[0]), w1a, w3a, w2a, tm, "ffn1_lat")
    c1 = _ffn_call(ctx.reshape(bsz * tc, d), mod3, ctx_rowf, row(g_ffn1[0]), w1a, w3a, w2a, tc, "ffn1_ctx")

    w_r = _reorder_w_in(w_in[0])
    no_b = jnp.zeros((1, 2 * DN_HEADS), F32)
    alog_row = _head_kind_lanes(no_b, dn_a_log[0].reshape(1, -1))
    dtb_row = _head_kind_lanes(no_b, dn_dt_bias[0].reshape(1, -1))
    cos, sin = _rope_tables(t)
    qn, kn = row(q_norm[0]), row(k_norm[0])
    qkv_l, z_l, bg_l, aq_l, ak_l, av_l = _inproj_call(
        x1, mod3, lat_row, row(g_mix[0]), w_r, alog_row, dtb_row, qn, kn, cos, sin, bsz, t, tm, True, "inproj_lat")
    qkv_c, _, bg_c, _, ak_c, av_c = _inproj_call(
        c1, mod3, ctx_rowf, row(g_mix[0]), w_r, alog_row, dtb_row, qn, kn, cos[:tc], sin[:tc], bsz, tc, tc, False,
        "inproj_ctx")

    bg_all = jnp.concatenate([bg_c, bg_l], axis=2)
    bg_all = bg_all.reshape(bsz, DN_HEADS, DN_BG_ROWS, (tc + t) // CHUNK, CHUNK).transpose(0, 1, 3, 2, 4)
    conv_w = dn_conv[0].reshape(DN_CONV, 3 * DN_HEADS, LANES).transpose(1, 0, 2)
    dn_wq, dn_ak, dn_u, dn_eg = _dn_prep_call(qkv_l, qkv_c, conv_w, bg_all, _dn_masks())
    o_fwd, o_bwd = _dn_scan_call(dn_wq, dn_ak, dn_u, dn_eg, t)
    at_lat = _attn_call(aq_l, ak_l, av_l, ak_c, av_c, 512)

    out = _tail_call(x1, o_fwd.reshape(bsz * t, DN_WIDTH), o_bwd.reshape(bsz * t, DN_WIDTH), z_l,
                     at_lat.reshape(bsz * t, AT_WIDTH), mod3, lat_row, row(dn_norm[0]), w_out[0].astype(BF16),
                     row(g_ffn2[0]), ffn2_w1[0].astype(BF16), ffn2_w3[0].astype(BF16), ffn2_w2[0].astype(BF16),
                     row(g_final), tm)
    return out.reshape(bsz, t, d)
```

```python
import functools

import numpy as np
import jax
import jax.numpy as jnp
from jax import lax
from jax.experimental import pallas as pl
from jax.experimental.pallas import tpu as pltpu

F32 = jnp.float32
BF16 = jnp.bfloat16

D_MODEL = 1024
CTX_LEN = 256
GRID_W = 64
EPS = 1e-6
N_MOD = 9
D_FF = 2816
DN_HEADS = 4
DN_DK = 128
DN_DV = 128
DN_WIDTH = DN_HEADS * DN_DV
DN_CONV = 5
AT_HEADS = 4
AT_KV_HEADS = 2
AT_HD = 128
AT_WIDTH = AT_HEADS * AT_HD
ATT_SCALE = AT_HD ** -0.5
ROPE_AXIS_DIM = AT_HD // 2
ROPE_THETA = 10000.0
LEN_DN_QKV = 3 * DN_WIDTH
OFF_DN_Z = LEN_DN_QKV
OFF_DN_B = OFF_DN_Z + DN_WIDTH
OFF_DN_A = OFF_DN_B + 2 * DN_HEADS
OFF_AT_Q = OFF_DN_A + 2 * DN_HEADS
OFF_AT_K = OFF_AT_Q + AT_WIDTH
OFF_AT_V = OFF_AT_K + AT_KV_HEADS * AT_HD
P_IN = OFF_AT_V + AT_KV_HEADS * AT_HD

LANES = 128
CHUNK = 128
FF_TILE = 256
ROW_SPLIT = 2
DN_GROUP = 17
DN_BG_ROWS = 8
DN_SCAN_CHUNKS = 2
DN_CONV_BLOCKS = 2
AT_KEYS = 1024
LOG2E = 1.4426950408889634
AT_SAFE_LOG2 = 60.0
VMEM_LIMIT = 56 * 1024 * 1024
NEG_BIG = -1e30

_NT = (((1,), (1,)), ((), ()))


def _sigmoid(x):
    return 1.0 / (1.0 + jnp.exp(-x))


def _silu(x):
    return x * _sigmoid(x)


def _rms(x, gain):
    ms = jnp.mean(x * x, axis=-1, keepdims=True)
    return x * lax.rsqrt(ms + EPS) * gain


def _dot(a, b):
    return jnp.dot(a, b, preferred_element_type=F32)


def _const_spec(shape):
    nd = len(shape)
    return pl.BlockSpec(shape, lambda *_: (0,) * nd, pipeline_mode=pl.Buffered(1))


def _params(n_axes):
    return pltpu.CompilerParams(dimension_semantics=("arbitrary",) * n_axes,
                                vmem_limit_bytes=VMEM_LIMIT)


def _mod_kernel(c_ref, w_ref, b_ref, o_ref):
    s = _silu(c_ref[...]).astype(BF16)
    o_ref[...] = _dot(s, w_ref[...].astype(BF16)) + b_ref[...]


def _mod_call(cc, w_mod, b_mod):
    d = cc.shape[1]
    n = w_mod.shape[1]
    return pl.pallas_call(
        _mod_kernel,
        out_shape=jax.ShapeDtypeStruct((cc.shape[0], n), F32),
        grid=(n // d,),
        in_specs=[pl.BlockSpec(cc.shape, lambda j: (0, 0)),
                  pl.BlockSpec((d, d), lambda j: (0, j)),
                  pl.BlockSpec((1, d), lambda j: (0, j))],
        out_specs=pl.BlockSpec((cc.shape[0], d), lambda j: (0, j)),
        compiler_params=_params(1),
        name="mod",
    )(cc, w_mod, b_mod)


def _lockstep(gens):
    results = [None] * len(gens)
    alive = list(range(len(gens)))
    while alive:
        still = []
        for i in alive:
            try:
                next(gens[i])
                still.append(i)
            except StopIteration as done:
                results[i] = done.value
        alive = still
    return results


def _row_slices(tm):
    sub = tm // ROW_SPLIT
    return [slice(r * sub, (r + 1) * sub) for r in range(ROW_SPLIT)]


def _swiglu_update(x, mod_ref, j0, g_ref, w1_ref, w3_ref, w2_ref, act_ref, rows):
    h = (_rms(x, g_ref[...]) * (1.0 + mod_ref[j0 + 1:j0 + 2, :]) + mod_ref[j0:j0 + 1, :]).astype(BF16)
    for c in range(D_FF // FF_TILE):
        sl = slice(c * FF_TILE, (c + 1) * FF_TILE)
        a = _dot(h, w1_ref[:, sl])
        b = _dot(h, w3_ref[:, sl])
        yield
        act_ref[rows, sl] = (_silu(a) * b).astype(BF16)
    y = _dot(act_ref[rows, :], w2_ref[...])
    yield
    return x + (0.5 * mod_ref[j0 + 2:j0 + 3, :]) * y


def _ffn_kernel(x_ref, mod_ref, g_ref, w1_ref, w3_ref, w2_ref, o_ref, act_ref):
    slices = _row_slices(x_ref.shape[0])
    outs = _lockstep([_swiglu_update(x_ref[rows, :], mod_ref, 0, g_ref, w1_ref, w3_ref, w2_ref, act_ref, rows)
                      for rows in slices])
    for rows, out in zip(slices, outs):
        o_ref[rows, :] = out


def _ffn_call(x2d, mod3, mod_row, g, w1, w3, w2, tm, name):
    n, d = x2d.shape
    return pl.pallas_call(
        _ffn_kernel,
        out_shape=jax.ShapeDtypeStruct((n, d), F32),
        grid=(n // tm,),
        in_specs=[pl.BlockSpec((tm, d), lambda i: (i, 0)),
                  pl.BlockSpec((None, N_MOD, d), lambda i: (mod_row(i), 0, 0)),
                  _const_spec((1, d)),
                  _const_spec(w1.shape), _const_spec(w3.shape), _const_spec(w2.shape)],
        out_specs=pl.BlockSpec((tm, d), lambda i: (i, 0)),
        scratch_shapes=[pltpu.VMEM((tm, D_FF), BF16)],
        compiler_params=_params(1),
        name=name,
    )(x2d, mod3, g, w1, w3, w2)


def _rope(x, cos, sin_signed, even):
    nxt = pltpu.roll(x, LANES - 1, axis=1)
    prv = pltpu.roll(x, 1, axis=1)
    return x * cos + jnp.where(even, nxt, prv) * sin_signed


def _inproj_kernel(x_ref, mod_ref, g_ref, w_ref, alog_ref, dtb_ref, qn_ref, kn_ref, cos_ref, sin_ref,
                   qkv_ref, z_ref, bg_ref, aq_ref, ak_ref, av_ref, *, rope):
    tm = x_ref.shape[0]
    x = x_ref[...]
    h = (_rms(x, g_ref[...]) * (1.0 + mod_ref[4:5, :]) + mod_ref[3:4, :]).astype(BF16)

    off_z = LEN_DN_QKV
    off_q = off_z + DN_WIDTH
    off_kv = off_q + AT_WIDTH
    off_ba = off_kv + 2 * AT_KV_HEADS * AT_HD
    p_ba = _dot(h, w_ref[:, off_ba:off_ba + LANES])
    p_q = _dot(h, w_ref[:, off_q:off_q + AT_WIDTH])

    beta = _sigmoid(p_ba)
    t = p_ba + dtb_ref[...]
    softplus = jnp.maximum(t, 0.0) + jnp.log(1.0 + jnp.exp(-jnp.abs(t)))
    g = -jnp.exp(alog_ref[...]) * softplus
    ri = lax.broadcasted_iota(jnp.int32, (CHUNK, CHUNK), 0)
    ci = lax.broadcasted_iota(jnp.int32, (CHUNK, CHUNK), 1)
    ltri = jnp.where(ri >= ci, 1.0, 0.0).astype(BF16)
    kind = lax.broadcasted_iota(jnp.int32, (CHUNK, LANES), 1) % DN_BG_ROWS
    for c in range(tm // CHUNK):
        rows = slice(c * CHUNK, (c + 1) * CHUNK)
        gc = g[rows]
        g1 = gc.astype(BF16)
        r1 = gc - g1.astype(F32)
        g2 = r1.astype(BF16)
        g3 = (r1 - g2.astype(F32)).astype(BF16)
        pre = _dot(ltri, g1) + _dot(ltri, g2) + _dot(ltri, g3)
        suf = pre[CHUNK - 1:CHUNK, :] - pre + gc
        tile = jnp.where(kind < 2, beta[rows], jnp.where(kind == 2, pre, suf))
        bg_ref[:, rows] = tile.T[:DN_HEADS * DN_BG_ROWS]

    p_kv = _dot(h, w_ref[:, off_kv:off_kv + 2 * AT_KV_HEADS * AT_HD])
    lane = lax.broadcasted_iota(jnp.int32, (tm, LANES), 1)
    even = (lane % 2) == 0
    if rope:
        cos = cos_ref[...]
        sin = sin_ref[...]
    for hd in range(AT_HEADS):
        q = _rms(p_q[:, hd * LANES:(hd + 1) * LANES], qn_ref[...])
        if rope:
            q = _rope(q, cos, sin, even)
        aq_ref[hd // 2, :, (hd % 2) * LANES:(hd % 2 + 1) * LANES] = (q * (ATT_SCALE * LOG2E)).astype(BF16)
    for hd in range(AT_KV_HEADS):
        k = _rms(p_kv[:, hd * LANES:(hd + 1) * LANES], kn_ref[...])
        if rope:
            k = _rope(k, cos, sin, even)
        ak_ref[hd] = k.astype(BF16)
        av_ref[hd] = p_kv[:, (AT_KV_HEADS + hd) * LANES:(AT_KV_HEADS + hd + 1) * LANES].T.astype(BF16)

    z_ref[...] = _dot(h, w_ref[:, off_z:off_z + DN_WIDTH]).astype(BF16)
    for part in range(3):
        p = _dot(h, w_ref[:, part * DN_WIDTH:(part + 1) * DN_WIDTH])
        for hd in range(DN_HEADS):
            qkv_ref[part * DN_HEADS + hd] = p[:, hd * LANES:(hd + 1) * LANES].astype(BF16)


def _inproj_call(x2d, mod3, mod_row, g, w_r, alog_row, dtb_row, qn, kn, cos, sin, bsz, t, tm, rope, name):
    n, d = x2d.shape
    tpb = t // tm
    bt = lambda i: (i // tpb, 0, i % tpb, 0)
    out_shape = (
        jax.ShapeDtypeStruct((bsz, 3 * DN_HEADS, t, LANES), BF16),
        jax.ShapeDtypeStruct((n, DN_WIDTH), BF16),
        jax.ShapeDtypeStruct((bsz, DN_HEADS * DN_BG_ROWS, t), F32),
        jax.ShapeDtypeStruct((bsz, AT_KV_HEADS, t, 2 * AT_HD), BF16),
        jax.ShapeDtypeStruct((bsz, AT_KV_HEADS, t, AT_HD), BF16),
        jax.ShapeDtypeStruct((bsz, AT_KV_HEADS, AT_HD, t), BF16),
    )
    out_specs = (
        pl.BlockSpec((None, 3 * DN_HEADS, tm, LANES), bt),
        pl.BlockSpec((tm, DN_WIDTH), lambda i: (i, 0)),
        pl.BlockSpec((None, DN_HEADS * DN_BG_ROWS, tm), lambda i: (i // tpb, 0, i % tpb)),
        pl.BlockSpec((None, AT_KV_HEADS, tm, 2 * AT_HD), bt),
        pl.BlockSpec((None, AT_KV_HEADS, tm, AT_HD), bt),
        pl.BlockSpec((None, AT_KV_HEADS, AT_HD, tm), lambda i: (i // tpb, 0, 0, i % tpb)),
    )
    return pl.pallas_call(
        functools.partial(_inproj_kernel, rope=rope),
        out_shape=out_shape,
        grid=(n // tm,),
        in_specs=[pl.BlockSpec((tm, d), lambda i: (i, 0)),
                  pl.BlockSpec((None, N_MOD, d), lambda i: (mod_row(i), 0, 0)),
                  _const_spec((1, d)),
                  _const_spec(w_r.shape),
                  _const_spec((1, LANES)), _const_spec((1, LANES)),
                  _const_spec((1, LANES)), _const_spec((1, LANES)),
                  pl.BlockSpec((tm, LANES), lambda i: (i % tpb, 0)),
                  pl.BlockSpec((tm, LANES), lambda i: (i % tpb, 0))],
        out_specs=out_specs,
        compiler_params=_params(1),
        name=name,
    )(x2d, mod3, g, w_r, alog_row, dtb_row, qn, kn, cos, sin)


def _dn_conv_all(src_refs, w_refs, xpad_ref, t, emits):
    nblk = t // CHUNK
    n = len(src_refs)
    for i in range(n):
        xpad_ref[i, 0:8, :] = jnp.zeros((8, LANES), F32)
        xpad_ref[i, 8 + t:16 + t, :] = jnp.zeros((8, LANES), F32)

    def fill(r, carry):
        r0 = pl.multiple_of(r * CHUNK, CHUNK)
        for i in range(n):
            xpad_ref[i, pl.ds(r0 + 8, CHUNK), :] = src_refs[i][pl.ds(r0, CHUNK), :].astype(F32)
        return carry

    lax.fori_loop(0, nblk, fill, 0)
    ws = [w_ref[...] for w_ref in w_refs]
    pad = (DN_CONV - 1) // 2
    win = CHUNK + 16

    def conv(r, carry):
        for u in range(DN_CONV_BLOCKS):
            r0 = pl.multiple_of((r * DN_CONV_BLOCKS + u) * CHUNK, CHUNK)
            for i in range(n):
                acc = None
                for j in range(DN_CONV):
                    tap = xpad_ref[i, pl.ds(r0 + (8 - pad + j), CHUNK), :] * ws[i][j:j + 1, :]
                    acc = tap if acc is None else acc + tap
                emits[i](r0, _silu(acc))
        return carry

    lax.fori_loop(0, nblk // DN_CONV_BLOCKS, conv, 0)


def _block_diag(cat):
    c = cat.shape[0]
    zero = jnp.zeros((c, c), cat.dtype)
    return jnp.concatenate([jnp.concatenate([cat[:, :c], zero], axis=1),
                            jnp.concatenate([zero, cat[:, c:]], axis=1)], axis=0)


def _dn_prep_chunk(a, qs_ref, ks_ref, vs_ref, bg_ref, lvl_ref, wq_ref, ak_ref, u_ref, eg_ref):
    row = pl.multiple_of(a * CHUNK, CHUNK)
    k = ks_ref[pl.ds(row, CHUNK), :]
    q = qs_ref[pl.ds(row, CHUNK), :]
    v = vs_ref[pl.ds(row, CHUNK), :]
    bg = bg_ref[a]
    kf = k.astype(F32)
    ri = lax.broadcasted_iota(jnp.int32, (CHUNK, CHUNK), 0)
    ci = lax.broadcasted_iota(jnp.int32, (CHUNK, CHUNK), 1)
    per_dir = []
    for d in range(2):
        beta = jnp.broadcast_to(bg[d:d + 1, :], (CHUNK, CHUNK)).T
        g_r = jnp.broadcast_to(bg[2 + d:3 + d, :], (CHUNK, CHUNK))
        g_c = g_r.T
        incl = (ri >= ci) if d == 0 else (ri <= ci)
        decay = jnp.exp(jnp.where(incl, g_c - g_r, NEG_BIG))
        per_dir.append((beta, g_c, decay, kf * beta))
    g_c1, g_r1 = per_dir[1][1], per_dir[1][1].T
    decay_1t = jnp.exp(jnp.where(ri >= ci, g_r1 - g_c1, NEG_BIG))
    aq = lax.dot_general(jnp.concatenate([per_dir[0][3].astype(BF16), q], axis=0), k, _NT,
                         preferred_element_type=F32)
    a1t = lax.dot_general(k, per_dir[1][3].astype(BF16), _NT, preferred_element_type=F32)
    yield
    qk = aq[CHUNK:]
    a_cat = jnp.concatenate([jnp.where(ri > ci, aq[:CHUNK] * per_dir[0][2], 0.0),
                             jnp.where(ri > ci, a1t * decay_1t, 0.0)], axis=1)

    rhs = []
    for d in range(2):
        beta, g_c, decay, kb = per_dir[d]
        eg = jnp.exp(g_c)
        g_last = g_c[CHUNK - 1:CHUNK, :] if d == 0 else g_c[0:1, :]
        k_tail = kf * jnp.exp(g_last - g_c)
        ak_ref[d, a] = jnp.concatenate([(qk * decay).astype(BF16), k_tail.T.astype(BF16)], axis=0)
        wq_ref[d, a, CHUNK:, :] = (q.astype(F32) * eg).astype(BF16)
        eg_ref[d, a] = jnp.broadcast_to(jnp.exp(g_last), (8, LANES))
        rhs.append(jnp.concatenate([v * beta, kb * eg], axis=1))

    eye = jnp.where(ri == ci, 1.0, 0.0)
    eye_cat = jnp.concatenate([eye, eye], axis=1)
    pair = jnp.where((ri >> 1) == (ci >> 1), 1.0, 0.0)
    t_cat = eye_cat - a_cat * jnp.concatenate([pair, pair], axis=1)
    for lvl in range(lvl_ref.shape[0]):
        b = 2 << lvl
        a_sub = _block_diag((a_cat * lvl_ref[lvl]).astype(BF16))
        t_bd = _block_diag(t_cat.astype(BF16))
        if b % 8:
            x = _dot(t_cat.astype(BF16), a_sub)
            yield
            t_cat = t_cat - _dot(x.astype(BF16), t_bd)
        else:
            odd = [t_cat[i * b:(i + 1) * b] for i in range(1, CHUNK // b, 2)]
            x = _dot(jnp.concatenate(odd, axis=0).astype(BF16), a_sub)
            yield
            y = _dot(x.astype(BF16), t_bd)
            parts = []
            for n, i in enumerate(range(1, CHUNK // b, 2)):
                parts += [t_cat[(i - 1) * b:i * b], odd[n] - y[n * b:(n + 1) * b]]
            t_cat = jnp.concatenate(parts, axis=0)
        yield
    toff = t_cat - eye_cat
    toffs = (toff[:, :CHUNK], toff[:, CHUNK:].T)

    for d in range(2):
        uw = rhs[d] + _dot(toffs[d].astype(BF16), rhs[d].astype(BF16))
        u_ref[d, a] = uw[:, :DN_DV].astype(BF16)
        wq_ref[d, a, :CHUNK, :] = uw[:, DN_DV:].astype(BF16)


def _dn_prep_kernel(ql_ref, kl_ref, vl_ref, qc_ref, kc_ref, vc_ref, wq_w_ref, wk_w_ref, wv_w_ref, bg_ref,
                    lvl_ref, wq_ref, ak_ref, u_ref, eg_ref, qs_ref, ks_ref, vs_ref, xpad_ref):
    t = ql_ref.shape[0]
    tc = qc_ref.shape[0]

    def emit_qk(dst_ref, base, scale):
        def emit(r0, y):
            yn = y * lax.rsqrt(jnp.sum(y * y, axis=-1, keepdims=True) + EPS)
            dst_ref[pl.ds(base + r0, CHUNK), :] = (yn * scale).astype(BF16)
        return emit

    def emit_v(base):
        def emit(r0, y):
            vs_ref[pl.ds(base + r0, CHUNK), :] = y
        return emit

    w_refs = (wq_w_ref, wk_w_ref, wv_w_ref)
    for srcs, tt, base in (((qc_ref, kc_ref, vc_ref), tc, 0), ((ql_ref, kl_ref, vl_ref), t, tc)):
        _dn_conv_all(srcs, w_refs, xpad_ref, tt,
                     (emit_qk(qs_ref, base, DN_DK ** -0.5), emit_qk(ks_ref, base, 1.0), emit_v(base)))

    refs = (qs_ref, ks_ref, vs_ref, bg_ref, lvl_ref, wq_ref, ak_ref, u_ref, eg_ref)

    def group(gi, carry):
        _lockstep([_dn_prep_chunk(gi * DN_GROUP + j, *refs) for j in range(DN_GROUP)])
        return carry

    lax.fori_loop(0, (tc + t) // (CHUNK * DN_GROUP), group, 0)


def _dn_masks():
    i = np.arange(CHUNK)
    lvls = []
    b = 2
    while b < CHUNK:
        rb, cb = (i // b)[:, None], (i // b)[None, :]
        fwd = ((rb == cb + 1) & (rb % 2 == 1)).astype(np.float32)
        lvls.append(np.concatenate([fwd, fwd], axis=1))
        b *= 2
    return jnp.asarray(np.stack(lvls))


def _dn_prep_call(qkv_l, qkv_c, conv_w, bg_all, lvl_masks):
    bsz, _, t, _ = qkv_l.shape
    tc = qkv_c.shape[2]
    ta = t + tc
    nc = ta // CHUNK

    def head(part, tt):
        return pl.BlockSpec((None, None, tt, LANES), lambda b, h: (b, part * DN_HEADS + h, 0, 0))

    def wspec(part):
        return pl.BlockSpec((None, DN_CONV, LANES), lambda b, h: (part * DN_HEADS + h, 0, 0))

    def out(rows, dt):
        return (jax.ShapeDtypeStruct((bsz, DN_HEADS, 2, nc, rows, LANES), dt),
                pl.BlockSpec((None, None, 2, nc, rows, LANES), lambda b, h: (b, h, 0, 0, 0, 0)))

    outs = [out(2 * CHUNK, BF16), out(2 * CHUNK, BF16), out(CHUNK, BF16), out(8, F32)]
    return pl.pallas_call(
        _dn_prep_kernel,
        out_shape=tuple(o[0] for o in outs),
        grid=(bsz, DN_HEADS),
        in_specs=[head(0, t), head(1, t), head(2, t), head(0, tc), head(1, tc), head(2, tc),
                  wspec(0), wspec(1), wspec(2),
                  pl.BlockSpec((None, None, nc, DN_BG_ROWS, LANES), lambda b, h: (b, h, 0, 0, 0)),
                  _const_spec(lvl_masks.shape)],
        out_specs=tuple(o[1] for o in outs),
        scratch_shapes=[pltpu.VMEM((ta, LANES), BF16), pltpu.VMEM((ta, LANES), BF16),
                        pltpu.VMEM((ta, LANES), F32), pltpu.VMEM((3, t + 16, LANES), F32)],
        compiler_params=_params(2),
        name="dn_prep",
    )(qkv_l, qkv_l, qkv_l, qkv_c, qkv_c, qkv_c, conv_w, conv_w, conv_w, bg_all, lvl_masks)


def _dn_scan_kernel(wq0_ref, wq1_ref, ak0_ref, ak1_ref, u0_ref, u1_ref, eg0_ref, eg1_ref, o0_ref, o1_ref, s_ref):
    @pl.when(pl.program_id(1) == 0)
    def _():
        s_ref[...] = jnp.zeros(s_ref.shape, F32)

    heads = range(DN_HEADS)
    s = [s_ref[h] for h in heads]
    for j in range(DN_SCAN_CHUNKS):
        jb = DN_SCAN_CHUNKS - 1 - j
        r1 = [_dot(jnp.concatenate([wq0_ref[h, j], wq1_ref[h, jb]], axis=1), _block_diag(s[h].astype(BF16)))
              for h in heads]
        v_new = [jnp.concatenate([u0_ref[h, j], u1_ref[h, jb]], axis=1).astype(F32) - r1[h][:CHUNK] for h in heads]
        r2 = [_dot(jnp.concatenate([ak0_ref[h, j], ak1_ref[h, jb]], axis=1), _block_diag(v_new[h].astype(BF16)))
              for h in heads]
        for h in heads:
            o = r1[h][CHUNK:] + r2[h][:CHUNK]
            o0_ref[j * CHUNK:(j + 1) * CHUNK, h * DN_DV:(h + 1) * DN_DV] = o[:, :DN_DV].astype(BF16)
            o1_ref[jb * CHUNK:(jb + 1) * CHUNK, h * DN_DV:(h + 1) * DN_DV] = o[:, DN_DV:].astype(BF16)
            dec = jnp.concatenate([eg0_ref[h, j, 0:1, :], eg1_ref[h, jb, 0:1, :]], axis=1)
            s[h] = s[h] * dec + r2[h][CHUNK:]
    for h in heads:
        s_ref[h] = s[h]


def _dn_scan_call(wq, ak, u, eg, t):
    bsz, nh, _, nc, _, _ = wq.shape
    k = DN_SCAN_CHUNKS
    nb = nc // k
    nb_ctx = (nc - t // CHUNK) // k

    def bwd_block(i):
        return jnp.where(i < nb_ctx, nb_ctx - 1 - i, nb - 1 + nb_ctx - i)

    def spec(rows, d):
        if d == 0:
            return pl.BlockSpec((None, nh, None, k, rows, LANES), lambda b, i: (b, 0, 0, i, 0, 0))
        return pl.BlockSpec((None, nh, None, k, rows, LANES), lambda b, i: (b, 0, 1, bwd_block(i), 0, 0))

    o_shape = jax.ShapeDtypeStruct((bsz, t, nh * DN_DV), BF16)
    o0_spec = pl.BlockSpec((None, k * CHUNK, nh * DN_DV), lambda b, i: (b, jnp.maximum(i - nb_ctx, 0), 0))
    o1_spec = pl.BlockSpec((None, k * CHUNK, nh * DN_DV),
                           lambda b, i: (b, bwd_block(jnp.maximum(i, nb_ctx)) - nb_ctx, 0))
    return pl.pallas_call(
        _dn_scan_kernel,
        out_shape=(o_shape, o_shape),
        grid=(bsz, nb),
        in_specs=[spec(2 * CHUNK, 0), spec(2 * CHUNK, 1), spec(2 * CHUNK, 0), spec(2 * CHUNK, 1),
                  spec(CHUNK, 0), spec(CHUNK, 1), spec(8, 0), spec(8, 1)],
        out_specs=(o0_spec, o1_spec),
        scratch_shapes=[pltpu.VMEM((nh, DN_DK, 2 * DN_DV), F32)],
        compiler_params=_params(2),
        name="dn_scan",
    )(wq, wq, ak, ak, u, u, eg, eg)


def _attn_kernel(q_ref, kl_ref, vtl_ref, kc_ref, vtc_ref, o_ref, kmax_ref, shift_ref):
    tq = q_ref.shape[0]
    n = 2 * tq
    t = kl_ref.shape[0]
    blocks = [(kl_ref, vtl_ref, j * AT_KEYS, AT_KEYS) for j in range(t // AT_KEYS)]
    blocks.append((kc_ref, vtc_ref, 0, kc_ref.shape[0]))

    @pl.when(pl.program_id(2) == 0)
    def _():
        best = jnp.zeros((1, 1), F32)
        for k_ref, _, off, size in blocks:
            kk = k_ref[off:off + size, :].astype(F32)
            best = jnp.maximum(best, jnp.max(jnp.sum(kk * kk, axis=-1, keepdims=True), axis=0, keepdims=True))
        kmax_ref[...] = jnp.broadcast_to(best, kmax_ref.shape)

    q = q_ref[...].astype(F32)
    qt32 = jnp.concatenate([q[:, :AT_HD].T, q[:, AT_HD:].T], axis=1)
    qt = qt32.astype(BF16)
    bound = jnp.sqrt(jnp.sum(qt32 * qt32, axis=0, keepdims=True) * kmax_ref[0:1, 0:1])
    shift_ref[...] = bound

    def scores(blk):
        k_ref, _, off, size = blk
        return _dot(k_ref[off:off + size, :], qt)

    @pl.when(jnp.max(bound) > AT_SAFE_LOG2)
    def _():
        m = jnp.full((1, n), NEG_BIG, F32)
        for blk in blocks:
            m = jnp.maximum(m, jnp.max(scores(blk), axis=0, keepdims=True))
        shift_ref[...] = m

    shift = shift_ref[...]
    den = jnp.zeros((1, n), F32)
    acc = jnp.zeros((AT_HD, n), F32)
    s_next = scores(blocks[0])
    for j, (_, vt_ref, off, size) in enumerate(blocks):
        s = s_next
        if j + 1 < len(blocks):
            s_next = scores(blocks[j + 1])
        p = jnp.exp2(s - shift)
        den = den + jnp.sum(p, axis=0, keepdims=True)
        acc = acc + _dot(vt_ref[:, off:off + size], p.astype(BF16))
    o = (acc / den).T
    o_ref[:, :AT_HD] = o[:tq].astype(BF16)
    o_ref[:, AT_HD:] = o[tq:].astype(BF16)


def _attn_call(aq, ak_l, avt_l, ak_c, avt_c, tq):
    bsz, hkv, t, _ = aq.shape
    tc = ak_c.shape[2]
    kspec = lambda tt: pl.BlockSpec((None, None, tt, AT_HD), lambda b, j, i: (b, j, 0, 0))
    vspec = lambda tt: pl.BlockSpec((None, None, AT_HD, tt), lambda b, j, i: (b, j, 0, 0))
    return pl.pallas_call(
        _attn_kernel,
        out_shape=jax.ShapeDtypeStruct((bsz, t, AT_WIDTH), BF16),
        grid=(bsz, hkv, t // tq),
        in_specs=[pl.BlockSpec((None, None, tq, 2 * AT_HD), lambda b, j, i: (b, j, i, 0)),
                  kspec(t), vspec(t), kspec(tc), vspec(tc)],
        out_specs=pl.BlockSpec((None, tq, 2 * AT_HD), lambda b, j, i: (b, i, j)),
        scratch_shapes=[pltpu.VMEM((8, LANES), F32), pltpu.VMEM((1, 2 * tq), F32)],
        compiler_params=_params(3),
        name="attn",
    )(aq, ak_l, avt_l, ak_c, avt_c)


def _tail_kernel(x_ref, o0_ref, o1_ref, z_ref, at_ref, mod_ref, gn_ref, wo_ref, g_ref, w1_ref, w3_ref, w2_ref,
                 gf_ref, o_ref, act_ref, dn_ref):
    def rows_gen(rows):
        for hd in range(DN_HEADS):
            sl = slice(hd * DN_DV, (hd + 1) * DN_DV)
            o = o0_ref[rows, sl].astype(F32) + o1_ref[rows, sl].astype(F32)
            dn_ref[rows, sl] = (_rms(o, gn_ref[...]) * _silu(z_ref[rows, sl].astype(F32))).astype(BF16)
        mix = _dot(dn_ref[rows, :], wo_ref[:DN_WIDTH, :]) + _dot(at_ref[rows, :], wo_ref[DN_WIDTH:, :])
        yield
        x = x_ref[rows, :] + mod_ref[5:6, :] * mix
        x = yield from _swiglu_update(x, mod_ref, 6, g_ref, w1_ref, w3_ref, w2_ref, act_ref, rows)
        o_ref[rows, :] = _rms(x, gf_ref[...])

    _lockstep([rows_gen(rows) for rows in _row_slices(x_ref.shape[0])])


def _tail_call(x2d, o0, o1, z2d, at2d, mod3, mod_row, gn, wo, g, w1, w3, w2, gf, tm):
    n, d = x2d.shape
    return pl.pallas_call(
        _tail_kernel,
        out_shape=jax.ShapeDtypeStruct((n, d), F32),
        grid=(n // tm,),
        in_specs=[pl.BlockSpec((tm, d), lambda i: (i, 0)),
                  pl.BlockSpec((tm, DN_WIDTH), lambda i: (i, 0)),
                  pl.BlockSpec((tm, DN_WIDTH), lambda i: (i, 0)),
                  pl.BlockSpec((tm, DN_WIDTH), lambda i: (i, 0)),
                  pl.BlockSpec((tm, AT_WIDTH), lambda i: (i, 0)),
                  pl.BlockSpec((None, N_MOD, d), lambda i: (mod_row(i), 0, 0)),
                  _const_spec((1, LANES)),
                  _const_spec(wo.shape), _const_spec((1, d)),
                  _const_spec(w1.shape), _const_spec(w3.shape), _const_spec(w2.shape),
                  _const_spec((1, d))],
        out_specs=pl.BlockSpec((tm, d), lambda i: (i, 0)),
        scratch_shapes=[pltpu.VMEM((tm, D_FF), BF16), pltpu.VMEM((tm, DN_WIDTH), BF16)],
        compiler_params=_params(1),
        name="tail",
    )(x2d, o0, o1, z2d, at2d, mod3, gn, wo, g, w1, w3, w2, gf)


def _rope_tables(n):
    rows = n // GRID_W
    row = jnp.repeat(jnp.arange(rows, dtype=jnp.int32), GRID_W).astype(F32)
    col = jnp.tile(jnp.arange(GRID_W, dtype=jnp.int32), rows).astype(F32)
    freqs = 1.0 / (ROPE_THETA ** (jnp.arange(0, ROPE_AXIS_DIM, 2, dtype=F32) / ROPE_AXIS_DIM))
    ang = jnp.concatenate([row[:, None] * freqs, col[:, None] * freqs], axis=-1)
    cos = jnp.repeat(jnp.cos(ang), 2, axis=-1)
    sin = jnp.repeat(jnp.sin(ang), 2, axis=-1)
    sign = jnp.tile(jnp.array([-1.0, 1.0], F32), AT_HD // 2)
    return cos, sin * sign


def _head_kind_lanes(b_part, a_part):
    lead = b_part.shape[:-1]
    kinds = jnp.stack([b_part.reshape(*lead, 2, DN_HEADS), a_part.reshape(*lead, 2, DN_HEADS)], axis=-3)
    kinds = jnp.moveaxis(kinds.reshape(*lead, 4, DN_HEADS), -1, -2)
    kinds = jnp.pad(kinds, [(0, 0)] * len(lead) + [(0, 0), (0, DN_BG_ROWS - 4)])
    flat = kinds.reshape(*lead, DN_HEADS * DN_BG_ROWS)
    return jnp.pad(flat, [(0, 0)] * len(lead) + [(0, LANES - DN_HEADS * DN_BG_ROWS)])


def _reorder_w_in(w_in):
    ba = _head_kind_lanes(w_in[:, OFF_DN_B:OFF_DN_A], w_in[:, OFF_DN_A:OFF_AT_Q])
    return jnp.concatenate([w_in[:, :OFF_DN_B], w_in[:, OFF_AT_Q:], ba], axis=1).astype(BF16)


def kernel(x, c, ctx, c_ctx, w_mod, b_mod, g_ffn1, ffn1_w1, ffn1_w3, ffn1_w2, g_mix, w_in, dn_conv, dn_a_log,
           dn_dt_bias, dn_norm, q_norm, k_norm, w_out, g_ffn2, ffn2_w1, ffn2_w3, ffn2_w2, g_final):
    bsz, t, d = x.shape
    tc = ctx.shape[1]
    assert w_mod.shape[0] == 1, "single-layer block"
    tm = 512
    ctx_row = bsz

    cc = jnp.zeros((8, d), F32).at[:bsz].set(c).at[ctx_row].set(c_ctx)
    mod3 = _mod_call(cc, w_mod[0], b_mod[0][None]).reshape(8, N_MOD, d)

    lat_row = lambda i: i // (t // tm)
    ctx_rowf = lambda i: ctx_row
    row = lambda v: v.reshape(1, -1)

    w1a, w3a, w2a = ffn1_w1[0].astype(BF16), ffn1_w3[0].astype(BF16), ffn1_w2[0].astype(BF16)
    x1 = _ffn_call(x.reshape(bsz * t, d), mod3, lat_row, row(g_ffn1[0]), w1a, w3a, w2a, tm, "ffn1_lat")
    c1 = _ffn_call(ctx.reshape(bsz * tc, d), mod3, ctx_rowf, row(g_ffn1[0]), w1a, w3a, w2a, tc, "ffn1_ctx")

    w_r = _reorder_w_in(w_in[0])
    no_b = jnp.zeros((1, 2 * DN_HEADS), F32)
    alog_row = _head_kind_lanes(no_b, dn_a_log[0].reshape(1, -1))
    dtb_row = _head_kind_lanes(no_b, dn_dt_bias[0].reshape(1, -1))
    cos, sin = _rope_tables(t)
    qn, kn = row(q_norm[0]), row(k_norm[0])
    qkv_l, z_l, bg_l, aq_l, ak_l, av_l = _inproj_call(
        x1, mod3, lat_row, row(g_mix[0]), w_r, alog_row, dtb_row, qn, kn, cos, sin, bsz, t, tm, True, "inproj_lat")
    qkv_c, _, bg_c, _, ak_c, av_c = _inproj_call(
        c1, mod3, ctx_rowf, row(g_mix[0]), w_r, alog_row, dtb_row, qn, kn, cos[:tc], sin[:tc], bsz, tc, tc, False,
        "inproj_ctx")

    bg_all = jnp.concatenate([bg_c, bg_l], axis=2)
    bg_all = bg_all.reshape(bsz, DN_HEADS, DN_BG_ROWS, (tc + t) // CHUNK, CHUNK).transpose(0, 1, 3, 2, 4)
    conv_w = dn_conv[0].reshape(DN_CONV, 3 * DN_HEADS, LANES).transpose(1, 0, 2)
    dn_wq, dn_ak, dn_u, dn_eg = _dn_prep_call(qkv_l, qkv_c, conv_w, bg_all, _dn_masks())
    o_fwd, o_bwd = _dn_scan_call(dn_wq, dn_ak, dn_u, dn_eg, t)
    at_lat = _attn_call(aq_l, ak_l, av_l, ak_c, av_c, 512)

    out = _tail_call(x1, o_fwd.reshape(bsz * t, DN_WIDTH), o_bwd.reshape(bsz * t, DN_WIDTH), z_l,
                     at_lat.reshape(bsz * t, AT_WIDTH), mod3, lat_row, row(dn_norm[0]), w_out[0].astype(BF16),
                     row(g_ffn2[0]), ffn2_w1[0].astype(BF16), ffn2_w3[0].astype(BF16), ffn2_w2[0].astype(BF16),
                     row(g_final), tm)
    return out.reshape(bsz, t, d)
```

```python
import functools

import numpy as np
import jax
import jax.numpy as jnp
from jax import lax
from jax.experimental import pallas as pl
from jax.experimental.pallas import tpu as pltpu

F32 = jnp.float32
BF16 = jnp.bfloat16

D_MODEL = 1024
CTX_LEN = 256
GRID_W = 64
EPS = 1e-6
N_MOD = 9
D_FF = 2816
DN_HEADS = 4
DN_DK = 128
DN_DV = 128
DN_WIDTH = DN_HEADS * DN_DV
DN_CONV = 5
AT_HEADS = 4
AT_KV_HEADS = 2
AT_HD = 128
AT_WIDTH = AT_HEADS * AT_HD
ATT_SCALE = AT_HD ** -0.5
ROPE_AXIS_DIM = AT_HD // 2
ROPE_THETA = 10000.0
LEN_DN_QKV = 3 * DN_WIDTH
OFF_DN_Z = LEN_DN_QKV
OFF_DN_B = OFF_DN_Z + DN_WIDTH
OFF_DN_A = OFF_DN_B + 2 * DN_HEADS
OFF_AT_Q = OFF_DN_A + 2 * DN_HEADS
OFF_AT_K = OFF_AT_Q + AT_WIDTH
OFF_AT_V = OFF_AT_K + AT_KV_HEADS * AT_HD
P_IN = OFF_AT_V + AT_KV_HEADS * AT_HD

LANES = 128
CHUNK = 128
FF_TILE = 256
ROW_SPLIT = 2
DN_GROUP = 17
DN_BG_ROWS = 8
DN_SCAN_CHUNKS = 2
DN_CONV_BLOCKS = 2
AT_KEYS = 1024
LOG2E = 1.4426950408889634
AT_SAFE_LOG2 = 60.0
VMEM_LIMIT = 56 * 1024 * 1024
NEG_BIG = -1e30

_NT = (((1,), (1,)), ((), ()))


def _sigmoid(x):
    return 1.0 / (1.0 + jnp.exp(-x))


def _silu(x):
    return x * _sigmoid(x)


def _rms(x, gain):
    ms = jnp.mean(x * x, axis=-1, keepdims=True)
    return x * lax.rsqrt(ms + EPS) * gain


def _dot(a, b):
    return jnp.dot(a, b, preferred_element_type=F32)


def _const_spec(shape):
    nd = len(shape)
    return pl.BlockSpec(shape, lambda *_: (0,) * nd, pipeline_mode=pl.Buffered(1))


def _params(n_axes):
    return pltpu.CompilerParams(dimension_semantics=("arbitrary",) * n_axes,
                                vmem_limit_bytes=VMEM_LIMIT)


def _mod_kernel(c_ref, w_ref, b_ref, o_ref):
    s = _silu(c_ref[...]).astype(BF16)
    o_ref[...] = _dot(s, w_ref[...].astype(BF16)) + b_ref[...]


def _mod_call(cc, w_mod, b_mod):
    d = cc.shape[1]
    n = w_mod.shape[1]
    return pl.pallas_call(
        _mod_kernel,
        out_shape=jax.ShapeDtypeStruct((cc.shape[0], n), F32),
        grid=(n // d,),
        in_specs=[pl.BlockSpec(cc.shape, lambda j: (0, 0)),
                  pl.BlockSpec((d, d), lambda j: (0, j)),
                  pl.BlockSpec((1, d), lambda j: (0, j))],
        out_specs=pl.BlockSpec((cc.shape[0], d), lambda j: (0, j)),
        compiler_params=_params(1),
        name="mod",
    )(cc, w_mod, b_mod)


def _lockstep(gens):
    results = [None] * len(gens)
    alive = list(range(len(gens)))
    while alive:
        still = []
        for i in alive:
            try:
                next(gens[i])
                still.append(i)
            except StopIteration as done:
                results[i] = done.value
        alive = still
    return results


def _row_slices(tm):
    sub = tm // ROW_SPLIT
    return [slice(r * sub, (r + 1) * sub) for r in range(ROW_SPLIT)]


def _swiglu_update(x, mod_ref, j0, g_ref, w1_ref, w3_ref, w2_ref, act_ref, rows):
    h = (_rms(x, g_ref[...]) * (1.0 + mod_ref[j0 + 1:j0 + 2, :]) + mod_ref[j0:j0 + 1, :]).astype(BF16)
    for c in range(D_FF // FF_TILE):
        sl = slice(c * FF_TILE, (c + 1) * FF_TILE)
        a = _dot(h, w1_ref[:, sl])
        b = _dot(h, w3_ref[:, sl])
        yield
        act_ref[rows, sl] = (_silu(a) * b).astype(BF16)
    y = _dot(act_ref[rows, :], w2_ref[...])
    yield
    return x + (0.5 * mod_ref[j0 + 2:j0 + 3, :]) * y


def _ffn_kernel(x_ref, mod_ref, g_ref, w1_ref, w3_ref, w2_ref, o_ref, act_ref):
    slices = _row_slices(x_ref.shape[0])
    outs = _lockstep([_swiglu_update(x_ref[rows, :], mod_ref, 0, g_ref, w1_ref, w3_ref, w2_ref, act_ref, rows)
                      for rows in slices])
    for rows, out in zip(slices, outs):
        o_ref[rows, :] = out


def _ffn_call(x2d, mod3, mod_row, g, w1, w3, w2, tm, name):
    n, d = x2d.shape
    return pl.pallas_call(
        _ffn_kernel,
        out_shape=jax.ShapeDtypeStruct((n, d), F32),
        grid=(n // tm,),
        in_specs=[pl.BlockSpec((tm, d), lambda i: (i, 0)),
                  pl.BlockSpec((None, N_MOD, d), lambda i: (mod_row(i), 0, 0)),
                  _const_spec((1, d)),
                  _const_spec(w1.shape), _const_spec(w3.shape), _const_spec(w2.shape)],
        out_specs=pl.BlockSpec((tm, d), lambda i: (i, 0)),
        scratch_shapes=[pltpu.VMEM((tm, D_FF), BF16)],
        compiler_params=_params(1),
        name=name,
    )(x2d, mod3, g, w1, w3, w2)


def _rope(x, cos, sin_signed, even):
    nxt = pltpu.roll(x, LANES - 1, axis=1)
    prv = pltpu.roll(x, 1, axis=1)
    return x * cos + jnp.where(even, nxt, prv) * sin_signed


def _inproj_kernel(x_ref, mod_ref, g_ref, w_ref, alog_ref, dtb_ref, qn_ref, kn_ref, cos_ref, sin_ref,
                   qkv_ref, z_ref, bg_ref, aq_ref, ak_ref, av_ref, *, rope):
    tm = x_ref.shape[0]
    x = x_ref[...]
    h = (_rms(x, g_ref[...]) * (1.0 + mod_ref[4:5, :]) + mod_ref[3:4, :]).astype(BF16)

    off_z = LEN_DN_QKV
    off_q = off_z + DN_WIDTH
    off_kv = off_q + AT_WIDTH
    off_ba = off_kv + 2 * AT_KV_HEADS * AT_HD
    p_ba = _dot(h, w_ref[:, off_ba:off_ba + LANES])
    p_q = _dot(h, w_ref[:, off_q:off_q + AT_WIDTH])

    beta = _sigmoid(p_ba)
    t = p_ba + dtb_ref[...]
    softplus = jnp.maximum(t, 0.0) + jnp.log(1.0 + jnp.exp(-jnp.abs(t)))
    g = -jnp.exp(alog_ref[...]) * softplus
    ri = lax.broadcasted_iota(jnp.int32, (CHUNK, CHUNK), 0)
    ci = lax.broadcasted_iota(jnp.int32, (CHUNK, CHUNK), 1)
    ltri = jnp.where(ri >= ci, 1.0, 0.0).astype(BF16)
    kind = lax.broadcasted_iota(jnp.int32, (CHUNK, LANES), 1) % DN_BG_ROWS
    for c in range(tm // CHUNK):
        rows = slice(c * CHUNK, (c + 1) * CHUNK)
        gc = g[rows]
        g1 = gc.astype(BF16)
        r1 = gc - g1.astype(F32)
        g2 = r1.astype(BF16)
        g3 = (r1 - g2.astype(F32)).astype(BF16)
        pre = _dot(ltri, g1) + _dot(ltri, g2) + _dot(ltri, g3)
        suf = pre[CHUNK - 1:CHUNK, :] - pre + gc
        tile = jnp.where(kind < 2, beta[rows], jnp.where(kind == 2, pre, suf))
        bg_ref[:, rows] = tile.T[:DN_HEADS * DN_BG_ROWS]

    p_kv = _dot(h, w_ref[:, off_kv:off_kv + 2 * AT_KV_HEADS * AT_HD])
    lane = lax.broadcasted_iota(jnp.int32, (tm, LANES), 1)
    even = (lane % 2) == 0
    if rope:
        cos = cos_ref[...]
        sin = sin_ref[...]
    for hd in range(AT_HEADS):
        q = _rms(p_q[:, hd * LANES:(hd + 1) * LANES], qn_ref[...])
        if rope:
            q = _rope(q, cos, sin, even)
        aq_ref[hd // 2, :, (hd % 2) * LANES:(hd % 2 + 1) * LANES] = (q * (ATT_SCALE * LOG2E)).astype(BF16)
    for hd in range(AT_KV_HEADS):
        k = _rms(p_kv[:, hd * LANES:(hd + 1) * LANES], kn_ref[...])
        if rope:
            k = _rope(k, cos, sin, even)
        ak_ref[hd] = k.astype(BF16)
        av_ref[hd] = p_kv[:, (AT_KV_HEADS + hd) * LANES:(AT_KV_HEADS + hd + 1) * LANES].T.astype(BF16)

    z_ref[...] = _dot(h, w_ref[:, off_z:off_z + DN_WIDTH]).astype(BF16)
    for part in range(3):
        p = _dot(h, w_ref[:, part * DN_WIDTH:(part + 1) * DN_WIDTH])
        for hd in range(DN_HEADS):
            qkv_ref[part * DN_HEADS + hd] = p[:, hd * LANES:(hd + 1) * LANES].astype(BF16)


def _inproj_call(x2d, mod3, mod_row, g, w_r, alog_row, dtb_row, qn, kn, cos, sin, bsz, t, tm, rope, name):
    n, d = x2d.shape
    tpb = t // tm
    bt = lambda i: (i // tpb, 0, i % tpb, 0)
    out_shape = (
        jax.ShapeDtypeStruct((bsz, 3 * DN_HEADS, t, LANES), BF16),
        jax.ShapeDtypeStruct((n, DN_WIDTH), BF16),
        jax.ShapeDtypeStruct((bsz, DN_HEADS * DN_BG_ROWS, t), F32),
        jax.ShapeDtypeStruct((bsz, AT_KV_HEADS, t, 2 * AT_HD), BF16),
        jax.ShapeDtypeStruct((bsz, AT_KV_HEADS, t, AT_HD), BF16),
        jax.ShapeDtypeStruct((bsz, AT_KV_HEADS, AT_HD, t), BF16),
    )
    out_specs = (
        pl.BlockSpec((None, 3 * DN_HEADS, tm, LANES), bt),
        pl.BlockSpec((tm, DN_WIDTH), lambda i: (i, 0)),
        pl.BlockSpec((None, DN_HEADS * DN_BG_ROWS, tm), lambda i: (i // tpb, 0, i % tpb)),
        pl.BlockSpec((None, AT_KV_HEADS, tm, 2 * AT_HD), bt),
        pl.BlockSpec((None, AT_KV_HEADS, tm, AT_HD), bt),
        pl.BlockSpec((None, AT_KV_HEADS, AT_HD, tm), lambda i: (i // tpb, 0, 0, i % tpb)),
    )
    return pl.pallas_call(
        functools.partial(_inproj_kernel, rope=rope),
        out_shape=out_shape,
        grid=(n // tm,),
        in_specs=[pl.BlockSpec((tm, d), lambda i: (i, 0)),
                  pl.BlockSpec((None, N_MOD, d), lambda i: (mod_row(i), 0, 0)),
                  _const_spec((1, d)),
                  _const_spec(w_r.shape),
                  _const_spec((1, LANES)), _const_spec((1, LANES)),
                  _const_spec((1, LANES)), _const_spec((1, LANES)),
                  pl.BlockSpec((tm, LANES), lambda i: (i % tpb, 0)),
                  pl.BlockSpec((tm, LANES), lambda i: (i % tpb, 0))],
        out_specs=out_specs,
        compiler_params=_params(1),
        name=name,
    )(x2d, mod3, g, w_r, alog_row, dtb_row, qn, kn, cos, sin)


def _dn_conv_all(src_refs, w_refs, xpad_ref, t, emits):
    nblk = t // CHUNK
    n = len(src_refs)
    for i in range(n):
        xpad_ref[i, 0:8, :] = jnp.zeros((8, LANES), F32)
        xpad_ref[i, 8 + t:16 + t, :] = jnp.zeros((8, LANES), F32)

    def fill(r, carry):
        r0 = pl.multiple_of(r * CHUNK, CHUNK)
        for i in range(n):
            xpad_ref[i, pl.ds(r0 + 8, CHUNK), :] = src_refs[i][pl.ds(r0, CHUNK), :].astype(F32)
        return carry

    lax.fori_loop(0, nblk, fill, 0)
    ws = [w_ref[...] for w_ref in w_refs]
    pad = (DN_CONV - 1) // 2
    win = CHUNK + 16

    def conv(r, carry):
        for u in range(DN_CONV_BLOCKS):
            r0 = pl.multiple_of((r * DN_CONV_BLOCKS + u) * CHUNK, CHUNK)
            for i in range(n):
                acc = None
                for j in range(DN_CONV):
                    tap = xpad_ref[i, pl.ds(r0 + (8 - pad + j), CHUNK), :] * ws[i][j:j + 1, :]
                    acc = tap if acc is None else acc + tap
                emits[i](r0, _silu(acc))
        return carry

    lax.fori_loop(0, nblk // DN_CONV_BLOCKS, conv, 0)


def _block_diag(cat):
    c = cat.shape[0]
    zero = jnp.zeros((c, c), cat.dtype)
    return jnp.concatenate([jnp.concatenate([cat[:, :c], zero], axis=1),
                            jnp.concatenate([zero, cat[:, c:]], axis=1)], axis=0)


def _dn_prep_chunk(a, qs_ref, ks_ref, vs_ref, bg_ref, lvl_ref, wq_ref, ak_ref, u_ref, eg_ref):
    row = pl.multiple_of(a * CHUNK, CHUNK)
    k = ks_ref[pl.ds(row, CHUNK), :]
    q = qs_ref[pl.ds(row, CHUNK), :]
    v = vs_ref[pl.ds(row, CHUNK), :]
    bg = bg_ref[a]
    kf = k.astype(F32)
    ri = lax.broadcasted_iota(jnp.int32, (CHUNK, CHUNK), 0)
    ci = lax.broadcasted_iota(jnp.int32, (CHUNK, CHUNK), 1)
    per_dir = []
    for d in range(2):
        beta = jnp.broadcast_to(bg[d:d + 1, :], (CHUNK, CHUNK)).T
        g_r = jnp.broadcast_to(bg[2 + d:3 + d, :], (CHUNK, CHUNK))
        g_c = g_r.T
        incl = (ri >= ci) if d == 0 else (ri <= ci)
        decay = jnp.exp(jnp.where(incl, g_c - g_r, NEG_BIG))
        per_dir.append((beta, g_c, decay, kf * beta))
    aq = lax.dot_general(jnp.concatenate([per_dir[0][3].astype(BF16), per_dir[1][3].astype(BF16), q], axis=0),
                         k, _NT, preferred_element_type=F32)
    yield
    qk = aq[2 * CHUNK:]
    a_cat = jnp.concatenate([jnp.where(ri > ci, aq[:CHUNK] * per_dir[0][2], 0.0),
                             jnp.where(ri < ci, aq[CHUNK:2 * CHUNK] * per_dir[1][2], 0.0)], axis=1)

    rhs = []
    for d in range(2):
        beta, g_c, decay, kb = per_dir[d]
        eg = jnp.exp(g_c)
        g_last = g_c[CHUNK - 1:CHUNK, :] if d == 0 else g_c[0:1, :]
        k_tail = kf * jnp.exp(g_last - g_c)
        ak_ref[d, a] = jnp.concatenate([(qk * decay).astype(BF16), k_tail.T.astype(BF16)], axis=0)
        wq_ref[d, a, CHUNK:, :] = (q.astype(F32) * eg).astype(BF16)
        eg_ref[d, a] = jnp.broadcast_to(jnp.exp(g_last), (8, LANES))
        rhs.append(jnp.concatenate([v * beta, kb * eg], axis=1))

    eye = jnp.where(ri == ci, 1.0, 0.0)
    eye_cat = jnp.concatenate([eye, eye], axis=1)
    pair = jnp.where((ri >> 1) == (ci >> 1), 1.0, 0.0)
    t_cat = (eye_cat - a_cat * jnp.concatenate([pair, pair], axis=1)).astype(BF16)
    a_bf = a_cat.astype(BF16)
    for lvl in range(lvl_ref.shape[0]):
        x = _dot(t_cat, _block_diag(a_bf * lvl_ref[lvl]))
        yield
        t_cat = t_cat - _dot(x.astype(BF16), _block_diag(t_cat)).astype(BF16)
        yield
    toff = t_cat - eye_cat.astype(BF16)

    for d in range(2):
        uw = rhs[d] + _dot(toff[:, d * CHUNK:(d + 1) * CHUNK], rhs[d].astype(BF16))
        u_ref[d, a] = uw[:, :DN_DV].astype(BF16)
        wq_ref[d, a, :CHUNK, :] = uw[:, DN_DV:].astype(BF16)


def _dn_prep_kernel(ql_ref, kl_ref, vl_ref, qc_ref, kc_ref, vc_ref, wq_w_ref, wk_w_ref, wv_w_ref, bg_ref,
                    lvl_ref, wq_ref, ak_ref, u_ref, eg_ref, qs_ref, ks_ref, vs_ref, xpad_ref):
    t = ql_ref.shape[0]
    tc = qc_ref.shape[0]

    def emit_qk(dst_ref, base, scale):
        def emit(r0, y):
            yn = y * lax.rsqrt(jnp.sum(y * y, axis=-1, keepdims=True) + EPS)
            dst_ref[pl.ds(base + r0, CHUNK), :] = (yn * scale).astype(BF16)
        return emit

    def emit_v(base):
        def emit(r0, y):
            vs_ref[pl.ds(base + r0, CHUNK), :] = y
        return emit

    w_refs = (wq_w_ref, wk_w_ref, wv_w_ref)
    for srcs, tt, base in (((qc_ref, kc_ref, vc_ref), tc, 0), ((ql_ref, kl_ref, vl_ref), t, tc)):
        _dn_conv_all(srcs, w_refs, xpad_ref, tt,
                     (emit_qk(qs_ref, base, DN_DK ** -0.5), emit_qk(ks_ref, base, 1.0), emit_v(base)))

    refs = (qs_ref, ks_ref, vs_ref, bg_ref, lvl_ref, wq_ref, ak_ref, u_ref, eg_ref)

    def group(gi, carry):
        _lockstep([_dn_prep_chunk(gi * DN_GROUP + j, *refs) for j in range(DN_GROUP)])
        return carry

    lax.fori_loop(0, (tc + t) // (CHUNK * DN_GROUP), group, 0)


def _dn_masks():
    i = np.arange(CHUNK)
    lvls = []
    b = 2
    while b < CHUNK:
        rb, cb = (i // b)[:, None], (i // b)[None, :]
        fwd = ((rb == cb + 1) & (rb % 2 == 1)).astype(np.float32)
        lvls.append(np.concatenate([fwd, fwd.T], axis=1))
        b *= 2
    return jnp.asarray(np.stack(lvls), dtype=BF16)


def _dn_prep_call(qkv_l, qkv_c, conv_w, bg_all, lvl_masks):
    bsz, _, t, _ = qkv_l.shape
    tc = qkv_c.shape[2]
    ta = t + tc
    nc = ta // CHUNK

    def head(part, tt):
        return pl.BlockSpec((None, None, tt, LANES), lambda b, h: (b, part * DN_HEADS + h, 0, 0))

    def wspec(part):
        return pl.BlockSpec((None, DN_CONV, LANES), lambda b, h: (part * DN_HEADS + h, 0, 0))

    def out(rows, dt):
        return (jax.ShapeDtypeStruct((bsz, DN_HEADS, 2, nc, rows, LANES), dt),
                pl.BlockSpec((None, None, 2, nc, rows, LANES), lambda b, h: (b, h, 0, 0, 0, 0)))

    outs = [out(2 * CHUNK, BF16), out(2 * CHUNK, BF16), out(CHUNK, BF16), out(8, F32)]
    return pl.pallas_call(
        _dn_prep_kernel,
        out_shape=tuple(o[0] for o in outs),
        grid=(bsz, DN_HEADS),
        in_specs=[head(0, t), head(1, t), head(2, t), head(0, tc), head(1, tc), head(2, tc),
                  wspec(0), wspec(1), wspec(2),
                  pl.BlockSpec((None, None, nc, DN_BG_ROWS, LANES), lambda b, h: (b, h, 0, 0, 0)),
                  _const_spec(lvl_masks.shape)],
        out_specs=tuple(o[1] for o in outs),
        scratch_shapes=[pltpu.VMEM((ta, LANES), BF16), pltpu.VMEM((ta, LANES), BF16),
                        pltpu.VMEM((ta, LANES), F32), pltpu.VMEM((3, t + 16, LANES), F32)],
        compiler_params=_params(2),
        name="dn_prep",
    )(qkv_l, qkv_l, qkv_l, qkv_c, qkv_c, qkv_c, conv_w, conv_w, conv_w, bg_all, lvl_masks)


def _dn_scan_kernel(wq0_ref, wq1_ref, ak0_ref, ak1_ref, u0_ref, u1_ref, eg0_ref, eg1_ref, o0_ref, o1_ref, s_ref):
    @pl.when(pl.program_id(1) == 0)
    def _():
        s_ref[...] = jnp.zeros(s_ref.shape, F32)

    heads = range(DN_HEADS)
    s = [s_ref[h] for h in heads]
    for j in range(DN_SCAN_CHUNKS):
        jb = DN_SCAN_CHUNKS - 1 - j
        r1 = [_dot(jnp.concatenate([wq0_ref[h, j], wq1_ref[h, jb]], axis=1), _block_diag(s[h].astype(BF16)))
              for h in heads]
        v_new = [jnp.concatenate([u0_ref[h, j], u1_ref[h, jb]], axis=1).astype(F32) - r1[h][:CHUNK] for h in heads]
        r2 = [_dot(jnp.concatenate([ak0_ref[h, j], ak1_ref[h, jb]], axis=1), _block_diag(v_new[h].astype(BF16)))
              for h in heads]
        for h in heads:
            o = r1[h][CHUNK:] + r2[h][:CHUNK]
            o0_ref[j * CHUNK:(j + 1) * CHUNK, h * DN_DV:(h + 1) * DN_DV] = o[:, :DN_DV].astype(BF16)
            o1_ref[jb * CHUNK:(jb + 1) * CHUNK, h * DN_DV:(h + 1) * DN_DV] = o[:, DN_DV:].astype(BF16)
            dec = jnp.concatenate([eg0_ref[h, j, 0:1, :], eg1_ref[h, jb, 0:1, :]], axis=1)
            s[h] = s[h] * dec + r2[h][CHUNK:]
    for h in heads:
        s_ref[h] = s[h]


def _dn_scan_call(wq, ak, u, eg, t):
    bsz, nh, _, nc, _, _ = wq.shape
    k = DN_SCAN_CHUNKS
    nb = nc // k
    nb_ctx = (nc - t // CHUNK) // k

    def bwd_block(i):
        return jnp.where(i < nb_ctx, nb_ctx - 1 - i, nb - 1 + nb_ctx - i)

    def spec(rows, d):
        if d == 0:
            return pl.BlockSpec((None, nh, None, k, rows, LANES), lambda b, i: (b, 0, 0, i, 0, 0))
        return pl.BlockSpec((None, nh, None, k, rows, LANES), lambda b, i: (b, 0, 1, bwd_block(i), 0, 0))

    o_shape = jax.ShapeDtypeStruct((bsz, t, nh * DN_DV), BF16)
    o0_spec = pl.BlockSpec((None, k * CHUNK, nh * DN_DV), lambda b, i: (b, jnp.maximum(i - nb_ctx, 0), 0))
    o1_spec = pl.BlockSpec((None, k * CHUNK, nh * DN_DV),
                           lambda b, i: (b, bwd_block(jnp.maximum(i, nb_ctx)) - nb_ctx, 0))
    return pl.pallas_call(
        _dn_scan_kernel,
        out_shape=(o_shape, o_shape),
        grid=(bsz, nb),
        in_specs=[spec(2 * CHUNK, 0), spec(2 * CHUNK, 1), spec(2 * CHUNK, 0), spec(2 * CHUNK, 1),
                  spec(CHUNK, 0), spec(CHUNK, 1), spec(8, 0), spec(8, 1)],
        out_specs=(o0_spec, o1_spec),
        scratch_shapes=[pltpu.VMEM((nh, DN_DK, 2 * DN_DV), F32)],
        compiler_params=_params(2),
        name="dn_scan",
    )(wq, wq, ak, ak, u, u, eg, eg)


def _attn_kernel(q_ref, kl_ref, vtl_ref, kc_ref, vtc_ref, o_ref, kmax_ref, shift_ref):
    t = kl_ref.shape[0]
    blocks = [(kl_ref, vtl_ref, j * AT_KEYS, AT_KEYS) for j in range(t // AT_KEYS)]
    blocks.append((kc_ref, vtc_ref, 0, kc_ref.shape[0]))

    @pl.when(pl.program_id(2) == 0)
    def _():
        best = jnp.zeros((1, 1), F32)
        for k_ref, _, off, size in blocks:
            kk = k_ref[off:off + size, :].astype(F32)
            best = jnp.maximum(best, jnp.max(jnp.sum(kk * kk, axis=-1, keepdims=True), axis=0, keepdims=True))
        kmax_ref[...] = jnp.broadcast_to(best, kmax_ref.shape)

    def rows_gen(r, rows):
        tr = rows.stop - rows.start
        n = 2 * tr
        q = q_ref[rows, :].astype(F32)
        qt32 = jnp.concatenate([q[:, :AT_HD].T, q[:, AT_HD:].T], axis=1)
        qt = qt32.astype(BF16)
        bound = jnp.sqrt(jnp.sum(qt32 * qt32, axis=0, keepdims=True) * kmax_ref[0:1, 0:1])
        shift_ref[r] = bound

        def scores(blk):
            k_ref, _, off, size = blk
            return _dot(k_ref[off:off + size, :], qt)

        @pl.when(jnp.max(bound) > AT_SAFE_LOG2)
        def _():
            m = jnp.full((1, n), NEG_BIG, F32)
            for blk in blocks:
                m = jnp.maximum(m, jnp.max(scores(blk), axis=0, keepdims=True))
            shift_ref[r] = m

        shift = shift_ref[r]
        den = jnp.zeros((1, n), F32)
        acc = jnp.zeros((AT_HD, n), F32)
        s_next = scores(blocks[0])
        yield
        for j, (_, vt_ref, off, size) in enumerate(blocks):
            s = s_next
            if j + 1 < len(blocks):
                s_next = scores(blocks[j + 1])
            p = jnp.exp2(s - shift)
            den = den + jnp.sum(p, axis=0, keepdims=True)
            acc = acc + _dot(vt_ref[:, off:off + size], p.astype(BF16))
            yield
        o = (acc / den).T
        o_ref[rows, :AT_HD] = o[:tr].astype(BF16)
        o_ref[rows, AT_HD:] = o[tr:].astype(BF16)

    _lockstep([rows_gen(0, slice(0, q_ref.shape[0]))])


def _attn_call(aq, ak_l, avt_l, ak_c, avt_c, tq):
    bsz, hkv, t, _ = aq.shape
    tc = ak_c.shape[2]
    kspec = lambda tt: pl.BlockSpec((None, None, tt, AT_HD), lambda b, j, i: (b, j, 0, 0))
    vspec = lambda tt: pl.BlockSpec((None, None, AT_HD, tt), lambda b, j, i: (b, j, 0, 0))
    return pl.pallas_call(
        _attn_kernel,
        out_shape=jax.ShapeDtypeStruct((bsz, t, AT_WIDTH), BF16),
        grid=(bsz, hkv, t // tq),
        in_specs=[pl.BlockSpec((None, None, tq, 2 * AT_HD), lambda b, j, i: (b, j, i, 0)),
                  kspec(t), vspec(t), kspec(tc), vspec(tc)],
        out_specs=pl.BlockSpec((None, tq, 2 * AT_HD), lambda b, j, i: (b, i, j)),
        scratch_shapes=[pltpu.VMEM((8, LANES), F32), pltpu.VMEM((1, 1, 2 * tq), F32)],
        compiler_params=_params(3),
        name="attn",
    )(aq, ak_l, avt_l, ak_c, avt_c)


def _tail_kernel(x_ref, o0_ref, o1_ref, z_ref, at_ref, mod_ref, gn_ref, wo_ref, g_ref, w1_ref, w3_ref, w2_ref,
                 gf_ref, o_ref, act_ref, dn_ref):
    def rows_gen(rows):
        for hd in range(DN_HEADS):
            sl = slice(hd * DN_DV, (hd + 1) * DN_DV)
            o = o0_ref[rows, sl].astype(F32) + o1_ref[rows, sl].astype(F32)
            dn_ref[rows, sl] = (_rms(o, gn_ref[...]) * _silu(z_ref[rows, sl].astype(F32))).astype(BF16)
        mix = _dot(dn_ref[rows, :], wo_ref[:DN_WIDTH, :]) + _dot(at_ref[rows, :], wo_ref[DN_WIDTH:, :])
        yield
        x = x_ref[rows, :] + mod_ref[5:6, :] * mix
        x = yield from _swiglu_update(x, mod_ref, 6, g_ref, w1_ref, w3_ref, w2_ref, act_ref, rows)
        o_ref[rows, :] = _rms(x, gf_ref[...])

    _lockstep([rows_gen(rows) for rows in _row_slices(x_ref.shape[0])])


def _tail_call(x2d, o0, o1, z2d, at2d, mod3, mod_row, gn, wo, g, w1, w3, w2, gf, tm):
    n, d = x2d.shape
    return pl.pallas_call(
        _tail_kernel,
        out_shape=jax.ShapeDtypeStruct((n, d), F32),
        grid=(n // tm,),
        in_specs=[pl.BlockSpec((tm, d), lambda i: (i, 0)),
                  pl.BlockSpec((tm, DN_WIDTH), lambda i: (i, 0)),
                  pl.BlockSpec((tm, DN_WIDTH), lambda i: (i, 0)),
                  pl.BlockSpec((tm, DN_WIDTH), lambda i: (i, 0)),
                  pl.BlockSpec((tm, AT_WIDTH), lambda i: (i, 0)),
                  pl.BlockSpec((None, N_MOD, d), lambda i: (mod_row(i), 0, 0)),
                  _const_spec((1, LANES)),
                  _const_spec(wo.shape), _const_spec((1, d)),
                  _const_spec(w1.shape), _const_spec(w3.shape), _const_spec(w2.shape),
                  _const_spec((1, d))],
        out_specs=pl.BlockSpec((tm, d), lambda i: (i, 0)),
        scratch_shapes=[pltpu.VMEM((tm, D_FF), BF16), pltpu.VMEM((tm, DN_WIDTH), BF16)],
        compiler_params=_params(1),
        name="tail",
    )(x2d, o0, o1, z2d, at2d, mod3, gn, wo, g, w1, w3, w2, gf)


def _rope_tables(n):
    rows = n // GRID_W
    row = jnp.repeat(jnp.arange(rows, dtype=jnp.int32), GRID_W).astype(F32)
    col = jnp.tile(jnp.arange(GRID_W, dtype=jnp.int32), rows).astype(F32)
    freqs = 1.0 / (ROPE_THETA ** (jnp.arange(0, ROPE_AXIS_DIM, 2, dtype=F32) / ROPE_AXIS_DIM))
    ang = jnp.concatenate([row[:, None] * freqs, col[:, None] * freqs], axis=-1)
    cos = jnp.repeat(jnp.cos(ang), 2, axis=-1)
    sin = jnp.repeat(jnp.sin(ang), 2, axis=-1)
    sign = jnp.tile(jnp.array([-1.0, 1.0], F32), AT_HD // 2)
    return cos, sin * sign


def _head_kind_lanes(b_part, a_part):
    lead = b_part.shape[:-1]
    kinds = jnp.stack([b_part.reshape(*lead, 2, DN_HEADS), a_part.reshape(*lead, 2, DN_HEADS)], axis=-3)
    kinds = jnp.moveaxis(kinds.reshape(*lead, 4, DN_HEADS), -1, -2)
    kinds = jnp.pad(kinds, [(0, 0)] * len(lead) + [(0, 0), (0, DN_BG_ROWS - 4)])
    flat = kinds.reshape(*lead, DN_HEADS * DN_BG_ROWS)
    return jnp.pad(flat, [(0, 0)] * len(lead) + [(0, LANES - DN_HEADS * DN_BG_ROWS)])


def _reorder_w_in(w_in):
    ba = _head_kind_lanes(w_in[:, OFF_DN_B:OFF_DN_A], w_in[:, OFF_DN_A:OFF_AT_Q])
    return jnp.concatenate([w_in[:, :OFF_DN_B], w_in[:, OFF_AT_Q:], ba], axis=1).astype(BF16)


def kernel(x, c, ctx, c_ctx, w_mod, b_mod, g_ffn1, ffn1_w1, ffn1_w3, ffn1_w2, g_mix, w_in, dn_conv, dn_a_log,
           dn_dt_bias, dn_norm, q_norm, k_norm, w_out, g_ffn2, ffn2_w1, ffn2_w3, ffn2_w2, g_final):
    bsz, t, d = x.shape
    tc = ctx.shape[1]
    assert w_mod.shape[0] == 1, "single-layer block"
    tm = 512
    ctx_row = bsz

    cc = jnp.zeros((8, d), F32).at[:bsz].set(c).at[ctx_row].set(c_ctx)
    mod3 = _mod_call(cc, w_mod[0], b_mod[0][None]).reshape(8, N_MOD, d)

    lat_row = lambda i: i // (t // tm)
    ctx_rowf = lambda i: ctx_row
    row = lambda v: v.reshape(1, -1)

    w1a, w3a, w2a = ffn1_w1[0].astype(BF16), ffn1_w3[0].astype(BF16), ffn1_w2[0].astype(BF16)
    x1 = _ffn_call(x.reshape(bsz * t, d), mod3, lat_row, row(g_ffn1[0]), w1a, w3a, w2a, tm, "ffn1_lat")
    c1 = _ffn_call(ctx.reshape(bsz * tc, d), mod3, ctx_rowf, row(g_ffn1[0]), w1a, w3a, w2a, tc, "ffn1_ctx")

    w_r = _reorder_w_in(w_in[0])
    no_b = jnp.zeros((1, 2 * DN_HEADS), F32)
    alog_row = _head_kind_lanes(no_b, dn_a_log[0].reshape(1, -1))
    dtb_row = _head_kind_lanes(no_b, dn_dt_bias[0].reshape(1, -1))
    cos, sin = _rope_tables(t)
    qn, kn = row(q_norm[0]), row(k_norm[0])
    qkv_l, z_l, bg_l, aq_l, ak_l, av_l = _inproj_call(
        x1, mod3, lat_row, row(g_mix[0]), w_r, alog_row, dtb_row, qn, kn, cos, sin, bsz, t, tm, True, "inproj_lat")
    qkv_c, _, bg_c, _, ak_c, av_c = _inproj_call(
        c1, mod3, ctx_rowf, row(g_mix[0]), w_r, alog_row, dtb_row, qn, kn, cos[:tc], sin[:tc], bsz, tc, tc, False,
        "inproj_ctx")

    bg_all = jnp.concatenate([bg_c, bg_l], axis=2)
    bg_all = bg_all.reshape(bsz, DN_HEADS, DN_BG_ROWS, (tc + t) // CHUNK, CHUNK).transpose(0, 1, 3, 2, 4)
    conv_w = dn_conv[0].reshape(DN_CONV, 3 * DN_HEADS, LANES).transpose(1, 0, 2)
    dn_wq, dn_ak, dn_u, dn_eg = _dn_prep_call(qkv_l, qkv_c, conv_w, bg_all, _dn_masks())
    o_fwd, o_bwd = _dn_scan_call(dn_wq, dn_ak, dn_u, dn_eg, t)
    at_lat = _attn_call(aq_l, ak_l, av_l, ak_c, av_c, 1024)

    out = _tail_call(x1, o_fwd.reshape(bsz * t, DN_WIDTH), o_bwd.reshape(bsz * t, DN_WIDTH), z_l,
                     at_lat.reshape(bsz * t, AT_WIDTH), mod3, lat_row, row(dn_norm[0]), w_out[0].astype(BF16),
                     row(g_ffn2[0]), ffn2_w1[0].astype(BF16), ffn2_w3[0].astype(BF16), ffn2_w2[0].astype(BF16),
                     row(g_final), tm)
    return out.reshape(bsz, t, d)
```

```python
import functools

import numpy as np
import jax
import jax.numpy as jnp
from jax import lax
from jax.experimental import pallas as pl
from jax.experimental.pallas import tpu as pltpu

F32 = jnp.float32
BF16 = jnp.bfloat16

D_MODEL = 1024
CTX_LEN = 256
GRID_W = 64
EPS = 1e-6
N_MOD = 9
D_FF = 2816
DN_HEADS = 4
DN_DK = 128
DN_DV = 128
DN_WIDTH = DN_HEADS * DN_DV
DN_CONV = 5
AT_HEADS = 4
AT_KV_HEADS = 2
AT_HD = 128
AT_WIDTH = AT_HEADS * AT_HD
ATT_SCALE = AT_HD ** -0.5
ROPE_AXIS_DIM = AT_HD // 2
ROPE_THETA = 10000.0
LEN_DN_QKV = 3 * DN_WIDTH
OFF_DN_Z = LEN_DN_QKV
OFF_DN_B = OFF_DN_Z + DN_WIDTH
OFF_DN_A = OFF_DN_B + 2 * DN_HEADS
OFF_AT_Q = OFF_DN_A + 2 * DN_HEADS
OFF_AT_K = OFF_AT_Q + AT_WIDTH
OFF_AT_V = OFF_AT_K + AT_KV_HEADS * AT_HD
P_IN = OFF_AT_V + AT_KV_HEADS * AT_HD

LANES = 128
CHUNK = 128
TM_FFN = 512
TM_INPROJ = 512
TM_TAIL = 512
FF_TILE = 256
ROW_SPLIT = 2
DN_GROUP = 17
DN_BG_ROWS = 8
DN_SCAN_CHUNKS = 2
DN_CONV_BLOCKS = 2
AT_KEYS = 1024
LOG2E = 1.4426950408889634
AT_SAFE_LOG2 = 60.0
VMEM_LIMIT = 56 * 1024 * 1024
NEG_BIG = -1e30

_NT = (((1,), (1,)), ((), ()))


def _sigmoid(x):
    return 1.0 / (1.0 + jnp.exp(-x))


def _silu(x):
    return x * _sigmoid(x)


def _rms(x, gain):
    ms = jnp.mean(x * x, axis=-1, keepdims=True)
    return x * lax.rsqrt(ms + EPS) * gain


def _dot(a, b):
    return jnp.dot(a, b, preferred_element_type=F32)


def _const_spec(shape):
    nd = len(shape)
    return pl.BlockSpec(shape, lambda *_: (0,) * nd, pipeline_mode=pl.Buffered(1))


def _params(n_axes):
    return pltpu.CompilerParams(dimension_semantics=("arbitrary",) * n_axes,
                                vmem_limit_bytes=VMEM_LIMIT)


def _mod_kernel(c_ref, w_ref, b_ref, o_ref):
    s = _silu(c_ref[...]).astype(BF16)
    o_ref[...] = _dot(s, w_ref[...].astype(BF16)) + b_ref[...]


def _mod_call(cc, w_mod, b_mod):
    d = cc.shape[1]
    n = w_mod.shape[1]
    return pl.pallas_call(
        _mod_kernel,
        out_shape=jax.ShapeDtypeStruct((cc.shape[0], n), F32),
        grid=(n // d,),
        in_specs=[pl.BlockSpec(cc.shape, lambda j: (0, 0)),
                  pl.BlockSpec((d, d), lambda j: (0, j)),
                  pl.BlockSpec((1, d), lambda j: (0, j))],
        out_specs=pl.BlockSpec((cc.shape[0], d), lambda j: (0, j)),
        compiler_params=_params(1),
        name="mod",
    )(cc, w_mod, b_mod)


def _lockstep(gens):
    results = [None] * len(gens)
    alive = list(range(len(gens)))
    while alive:
        still = []
        for i in alive:
            try:
                next(gens[i])
                still.append(i)
            except StopIteration as done:
                results[i] = done.value
        alive = still
    return results


def _row_slices(tm):
    sub = tm // ROW_SPLIT
    return [slice(r * sub, (r + 1) * sub) for r in range(ROW_SPLIT)]


def _swiglu_update(x, mod_ref, j0, g_ref, w1_ref, w3_ref, w2_ref, act_ref, rows):
    h = (_rms(x, g_ref[...]) * (1.0 + mod_ref[j0 + 1:j0 + 2, :]) + mod_ref[j0:j0 + 1, :]).astype(BF16)
    for c in range(D_FF // FF_TILE):
        sl = slice(c * FF_TILE, (c + 1) * FF_TILE)
        a = _dot(h, w1_ref[:, sl])
        b = _dot(h, w3_ref[:, sl])
        yield
        act_ref[rows, sl] = (_silu(a) * b).astype(BF16)
    y = _dot(act_ref[rows, :], w2_ref[...])
    yield
    return x + (0.5 * mod_ref[j0 + 2:j0 + 3, :]) * y


def _ffn_kernel(x_ref, mod_ref, g_ref, w1_ref, w3_ref, w2_ref, o_ref, act_ref):
    slices = _row_slices(x_ref.shape[0])
    outs = _lockstep([_swiglu_update(x_ref[rows, :], mod_ref, 0, g_ref, w1_ref, w3_ref, w2_ref, act_ref, rows)
                      for rows in slices])
    for rows, out in zip(slices, outs):
        o_ref[rows, :] = out


def _ffn_call(x2d, mod3, mod_row, g, w1, w3, w2, tm, name):
    n, d = x2d.shape
    return pl.pallas_call(
        _ffn_kernel,
        out_shape=jax.ShapeDtypeStruct((n, d), F32),
        grid=(n // tm,),
        in_specs=[pl.BlockSpec((tm, d), lambda i: (i, 0)),
                  pl.BlockSpec((None, N_MOD, d), lambda i: (mod_row(i), 0, 0)),
                  _const_spec((1, d)),
                  _const_spec(w1.shape), _const_spec(w3.shape), _const_spec(w2.shape)],
        out_specs=pl.BlockSpec((tm, d), lambda i: (i, 0)),
        scratch_shapes=[pltpu.VMEM((tm, D_FF), BF16)],
        compiler_params=_params(1),
        name=name,
    )(x2d, mod3, g, w1, w3, w2)


def _rope(x, cos, sin_signed, even):
    nxt = pltpu.roll(x, LANES - 1, axis=1)
    prv = pltpu.roll(x, 1, axis=1)
    return x * cos + jnp.where(even, nxt, prv) * sin_signed


def _inproj_kernel(x_ref, mod_ref, g_ref, wdn_ref, wat_ref, wba_ref, alog_ref, dtb_ref, qn_ref, kn_ref, cos_ref, sin_ref,
                   qkv_ref, z_ref, bg_ref, aq_ref, ak_ref, av_ref, *, rope):
    tm = x_ref.shape[0]
    x = x_ref[...]
    h = (_rms(x, g_ref[...]) * (1.0 + mod_ref[4:5, :]) + mod_ref[3:4, :]).astype(BF16)

    p_ba = _dot(h, wba_ref[...])
    p_q = _dot(h, wat_ref[:, :AT_WIDTH])

    beta = _sigmoid(p_ba)
    t = p_ba + dtb_ref[...]
    softplus = jnp.maximum(t, 0.0) + jnp.log(1.0 + jnp.exp(-jnp.abs(t)))
    g = -jnp.exp(alog_ref[...]) * softplus
    ri = lax.broadcasted_iota(jnp.int32, (CHUNK, CHUNK), 0)
    ci = lax.broadcasted_iota(jnp.int32, (CHUNK, CHUNK), 1)
    ltri = jnp.where(ri >= ci, 1.0, 0.0).astype(BF16)
    kind = lax.broadcasted_iota(jnp.int32, (CHUNK, LANES), 1) % DN_BG_ROWS
    for c in range(tm // CHUNK):
        rows = slice(c * CHUNK, (c + 1) * CHUNK)
        gc = g[rows]
        g1 = gc.astype(BF16)
        r1 = gc - g1.astype(F32)
        g2 = r1.astype(BF16)
        g3 = (r1 - g2.astype(F32)).astype(BF16)
        pre = _dot(ltri, g1) + _dot(ltri, g2) + _dot(ltri, g3)
        suf = pre[CHUNK - 1:CHUNK, :] - pre + gc
        tile = jnp.where(kind < 2, beta[rows], jnp.where(kind == 2, pre, suf))
        bg_ref[:, rows] = tile.T[:DN_HEADS * DN_BG_ROWS]

    p_kv = _dot(h, wat_ref[:, AT_WIDTH:])
    lane = lax.broadcasted_iota(jnp.int32, (tm, LANES), 1)
    even = (lane % 2) == 0
    if rope:
        cos = cos_ref[...]
        sin = sin_ref[...]
    for hd in range(AT_HEADS):
        q = _rms(p_q[:, hd * LANES:(hd + 1) * LANES], qn_ref[...])
        if rope:
            q = _rope(q, cos, sin, even)
        aq_ref[hd // 2, :, (hd % 2) * LANES:(hd % 2 + 1) * LANES] = (q * (ATT_SCALE * LOG2E)).astype(BF16)
    for hd in range(AT_KV_HEADS):
        k = _rms(p_kv[:, hd * LANES:(hd + 1) * LANES], kn_ref[...])
        if rope:
            k = _rope(k, cos, sin, even)
        ak_ref[hd] = k.astype(BF16)
        av_ref[hd] = p_kv[:, (AT_KV_HEADS + hd) * LANES:(AT_KV_HEADS + hd + 1) * LANES].T.astype(BF16)

    z_ref[...] = _dot(h, wdn_ref[:, LEN_DN_QKV:]).astype(BF16)
    for part in range(3):
        p = _dot(h, wdn_ref[:, part * DN_WIDTH:(part + 1) * DN_WIDTH])
        for hd in range(DN_HEADS):
            qkv_ref[part * DN_HEADS + hd] = p[:, hd * LANES:(hd + 1) * LANES].astype(BF16)


def _inproj_call(x2d, mod3, mod_row, g, w_parts, alog_row, dtb_row, qn, kn, cos, sin, bsz, t, tm, rope, name):
    n, d = x2d.shape
    tpb = t // tm
    bt = lambda i: (i // tpb, 0, i % tpb, 0)
    out_shape = (
        jax.ShapeDtypeStruct((bsz, 3 * DN_HEADS, t, LANES), BF16),
        jax.ShapeDtypeStruct((n, DN_WIDTH), BF16),
        jax.ShapeDtypeStruct((bsz, DN_HEADS * DN_BG_ROWS, t), F32),
        jax.ShapeDtypeStruct((bsz, AT_KV_HEADS, t, 2 * AT_HD), BF16),
        jax.ShapeDtypeStruct((bsz, AT_KV_HEADS, t, AT_HD), BF16),
        jax.ShapeDtypeStruct((bsz, AT_KV_HEADS, AT_HD, t), BF16),
    )
    out_specs = (
        pl.BlockSpec((None, 3 * DN_HEADS, tm, LANES), bt),
        pl.BlockSpec((tm, DN_WIDTH), lambda i: (i, 0)),
        pl.BlockSpec((None, DN_HEADS * DN_BG_ROWS, tm), lambda i: (i // tpb, 0, i % tpb)),
        pl.BlockSpec((None, AT_KV_HEADS, tm, 2 * AT_HD), bt),
        pl.BlockSpec((None, AT_KV_HEADS, tm, AT_HD), bt),
        pl.BlockSpec((None, AT_KV_HEADS, AT_HD, tm), lambda i: (i // tpb, 0, 0, i % tpb)),
    )
    return pl.pallas_call(
        functools.partial(_inproj_kernel, rope=rope),
        out_shape=out_shape,
        grid=(n // tm,),
        in_specs=[pl.BlockSpec((tm, d), lambda i: (i, 0)),
                  pl.BlockSpec((None, N_MOD, d), lambda i: (mod_row(i), 0, 0)),
                  _const_spec((1, d)),
                  *[_const_spec(w.shape) for w in w_parts],
                  _const_spec((1, LANES)), _const_spec((1, LANES)),
                  _const_spec((1, LANES)), _const_spec((1, LANES)),
                  pl.BlockSpec((tm, LANES), lambda i: (i % tpb, 0)),
                  pl.BlockSpec((tm, LANES), lambda i: (i % tpb, 0))],
        out_specs=out_specs,
        compiler_params=_params(1),
        name=name,
    )(x2d, mod3, g, *w_parts, alog_row, dtb_row, qn, kn, cos, sin)


def _dn_conv_all(src_refs, w_refs, xpad_ref, t, emits):
    nblk = t // CHUNK
    n = len(src_refs)
    for i in range(n):
        xpad_ref[i, 0:8, :] = jnp.zeros((8, LANES), F32)
        xpad_ref[i, 8 + t:16 + t, :] = jnp.zeros((8, LANES), F32)

    def fill(r, carry):
        r0 = pl.multiple_of(r * CHUNK, CHUNK)
        for i in range(n):
            xpad_ref[i, pl.ds(r0 + 8, CHUNK), :] = src_refs[i][pl.ds(r0, CHUNK), :].astype(F32)
        return carry

    lax.fori_loop(0, nblk, fill, 0)
    ws = [w_ref[...] for w_ref in w_refs]
    pad = (DN_CONV - 1) // 2

    def conv(r, carry):
        for u in range(DN_CONV_BLOCKS):
            r0 = pl.multiple_of((r * DN_CONV_BLOCKS + u) * CHUNK, CHUNK)
            for i in range(n):
                acc = None
                for j in range(DN_CONV):
                    tap = xpad_ref[i, pl.ds(r0 + (8 - pad + j), CHUNK), :] * ws[i][j:j + 1, :]
                    acc = tap if acc is None else acc + tap
                emits[i](r0, _silu(acc))
        return carry

    lax.fori_loop(0, nblk // DN_CONV_BLOCKS, conv, 0)


def _block_diag(cat):
    c = cat.shape[0]
    zero = jnp.zeros((c, c), cat.dtype)
    return jnp.concatenate([jnp.concatenate([cat[:, :c], zero], axis=1),
                            jnp.concatenate([zero, cat[:, c:]], axis=1)], axis=0)


def _dn_prep_chunk(a, qs_ref, ks_ref, vs_ref, bg_ref, lvl_ref, wq_ref, ak_ref, u_ref, eg_ref):
    row = pl.multiple_of(a * CHUNK, CHUNK)
    k = ks_ref[pl.ds(row, CHUNK), :]
    q = qs_ref[pl.ds(row, CHUNK), :]
    v = vs_ref[pl.ds(row, CHUNK), :]
    bg = bg_ref[a]
    kf = k.astype(F32)
    ri = lax.broadcasted_iota(jnp.int32, (CHUNK, CHUNK), 0)
    ci = lax.broadcasted_iota(jnp.int32, (CHUNK, CHUNK), 1)
    per_dir = []
    for d in range(2):
        beta = jnp.broadcast_to(bg[d:d + 1, :], (CHUNK, CHUNK)).T
        g_r = jnp.broadcast_to(bg[2 + d:3 + d, :], (CHUNK, CHUNK))
        g_c = g_r.T
        incl = (ri >= ci) if d == 0 else (ri <= ci)
        decay = jnp.exp(jnp.where(incl, g_c - g_r, NEG_BIG))
        per_dir.append((beta, g_c, decay, kf * beta))
    aq = lax.dot_general(jnp.concatenate([per_dir[0][3].astype(BF16), per_dir[1][3].astype(BF16), q], axis=0),
                         k, _NT, preferred_element_type=F32)
    yield
    qk = aq[2 * CHUNK:]
    a_cat = jnp.concatenate([jnp.where(ri > ci, aq[:CHUNK] * per_dir[0][2], 0.0),
                             jnp.where(ri < ci, aq[CHUNK:2 * CHUNK] * per_dir[1][2], 0.0)], axis=1)

    rhs = []
    for d in range(2):
        beta, g_c, decay, kb = per_dir[d]
        eg = jnp.exp(g_c)
        g_last = g_c[CHUNK - 1:CHUNK, :] if d == 0 else g_c[0:1, :]
        k_tail = kf * jnp.exp(g_last - g_c)
        ak_ref[d, a] = jnp.concatenate([(qk * decay).astype(BF16), k_tail.T.astype(BF16)], axis=0)
        wq_ref[d, a, CHUNK:, :] = (q.astype(F32) * eg).astype(BF16)
        eg_ref[d, a] = jnp.broadcast_to(jnp.exp(g_last), (8, LANES))
        rhs.append(jnp.concatenate([v * beta, kb * eg], axis=1))

    eye = jnp.where(ri == ci, 1.0, 0.0)
    eye_cat = jnp.concatenate([eye, eye], axis=1)
    pair = jnp.where((ri >> 1) == (ci >> 1), 1.0, 0.0)
    t_cat = (eye_cat - a_cat * jnp.concatenate([pair, pair], axis=1)).astype(BF16)
    a_bf = a_cat.astype(BF16)
    for lvl in range(lvl_ref.shape[0]):
        x = _dot(t_cat, _block_diag(a_bf * lvl_ref[lvl]))
        yield
        t_cat = t_cat - _dot(x.astype(BF16), _block_diag(t_cat)).astype(BF16)
        yield
    toff = t_cat - eye_cat.astype(BF16)

    for d in range(2):
        uw = rhs[d] + _dot(toff[:, d * CHUNK:(d + 1) * CHUNK], rhs[d].astype(BF16))
        u_ref[d, a] = uw[:, :DN_DV].astype(BF16)
        wq_ref[d, a, :CHUNK, :] = uw[:, DN_DV:].astype(BF16)


def _dn_prep_kernel(ql_ref, kl_ref, vl_ref, qc_ref, kc_ref, vc_ref, wq_w_ref, wk_w_ref, wv_w_ref, bg_ref,
                    lvl_ref, wq_ref, ak_ref, u_ref, eg_ref, qs_ref, ks_ref, vs_ref, xpad_ref):
    t = ql_ref.shape[0]
    tc = qc_ref.shape[0]

    def emit_qk(dst_ref, base, scale):
        def emit(r0, y):
            yn = y * lax.rsqrt(jnp.sum(y * y, axis=-1, keepdims=True) + EPS)
            dst_ref[pl.ds(base + r0, CHUNK), :] = (yn * scale).astype(BF16)
        return emit

    def emit_v(base):
        def emit(r0, y):
            vs_ref[pl.ds(base + r0, CHUNK), :] = y
        return emit

    w_refs = (wq_w_ref, wk_w_ref, wv_w_ref)
    for srcs, tt, base in (((qc_ref, kc_ref, vc_ref), tc, 0), ((ql_ref, kl_ref, vl_ref), t, tc)):
        _dn_conv_all(srcs, w_refs, xpad_ref, tt,
                     (emit_qk(qs_ref, base, DN_DK ** -0.5), emit_qk(ks_ref, base, 1.0), emit_v(base)))

    refs = (qs_ref, ks_ref, vs_ref, bg_ref, lvl_ref, wq_ref, ak_ref, u_ref, eg_ref)

    def group(gi, carry):
        _lockstep([_dn_prep_chunk(gi * DN_GROUP + j, *refs) for j in range(DN_GROUP)])
        return carry

    lax.fori_loop(0, (tc + t) // (CHUNK * DN_GROUP), group, 0)


def _dn_masks():
    i = np.arange(CHUNK)
    lvls = []
    b = 2
    while b < CHUNK:
        rb, cb = (i // b)[:, None], (i // b)[None, :]
        fwd = ((rb == cb + 1) & (rb % 2 == 1)).astype(np.float32)
        lvls.append(np.concatenate([fwd, fwd.T], axis=1))
        b *= 2
    return jnp.asarray(np.stack(lvls), dtype=BF16)


def _dn_prep_call(qkv_l, qkv_c, conv_w, bg_all, lvl_masks):
    bsz, _, t, _ = qkv_l.shape
    tc = qkv_c.shape[2]
    ta = t + tc
    nc = ta // CHUNK

    def head(part, tt):
        return pl.BlockSpec((None, None, tt, LANES), lambda b, h: (b, part * DN_HEADS + h, 0, 0))

    def wspec(part):
        return pl.BlockSpec((None, DN_CONV, LANES), lambda b, h: (part * DN_HEADS + h, 0, 0))

    def out(rows, dt):
        return (jax.ShapeDtypeStruct((bsz, DN_HEADS, 2, nc, rows, LANES), dt),
                pl.BlockSpec((None, None, 2, nc, rows, LANES), lambda b, h: (b, h, 0, 0, 0, 0)))

    outs = [out(2 * CHUNK, BF16), out(2 * CHUNK, BF16), out(CHUNK, BF16), out(8, F32)]
    return pl.pallas_call(
        _dn_prep_kernel,
        out_shape=tuple(o[0] for o in outs),
        grid=(bsz, DN_HEADS),
        in_specs=[head(0, t), head(1, t), head(2, t), head(0, tc), head(1, tc), head(2, tc),
                  wspec(0), wspec(1), wspec(2),
                  pl.BlockSpec((None, None, nc, DN_BG_ROWS, LANES), lambda b, h: (b, h, 0, 0, 0)),
                  _const_spec(lvl_masks.shape)],
        out_specs=tuple(o[1] for o in outs),
        scratch_shapes=[pltpu.VMEM((ta, LANES), BF16), pltpu.VMEM((ta, LANES), BF16),
                        pltpu.VMEM((ta, LANES), F32), pltpu.VMEM((3, t + 16, LANES), F32)],
        compiler_params=_params(2),
        name="dn_prep",
    )(qkv_l, qkv_l, qkv_l, qkv_c, qkv_c, qkv_c, conv_w, conv_w, conv_w, bg_all, lvl_masks)


def _dn_scan_kernel(wq0_ref, wq1_ref, ak0_ref, ak1_ref, u0_ref, u1_ref, eg0_ref, eg1_ref, o0_ref, o1_ref, s_ref):
    @pl.when(pl.program_id(1) == 0)
    def _():
        s_ref[...] = jnp.zeros(s_ref.shape, F32)

    heads = range(DN_HEADS)
    s = [s_ref[h] for h in heads]
    for j in range(DN_SCAN_CHUNKS):
        jb = DN_SCAN_CHUNKS - 1 - j
        r1 = [_dot(jnp.concatenate([wq0_ref[h, j], wq1_ref[h, jb]], axis=1), _block_diag(s[h].astype(BF16)))
              for h in heads]
        v_new = [jnp.concatenate([u0_ref[h, j], u1_ref[h, jb]], axis=1).astype(F32) - r1[h][:CHUNK] for h in heads]
        r2 = [_dot(jnp.concatenate([ak0_ref[h, j], ak1_ref[h, jb]], axis=1), _block_diag(v_new[h].astype(BF16)))
              for h in heads]
        for h in heads:
            o = r1[h][CHUNK:] + r2[h][:CHUNK]
            o0_ref[j * CHUNK:(j + 1) * CHUNK, h * DN_DV:(h + 1) * DN_DV] = o[:, :DN_DV].astype(BF16)
            o1_ref[jb * CHUNK:(jb + 1) * CHUNK, h * DN_DV:(h + 1) * DN_DV] = o[:, DN_DV:].astype(BF16)
            dec = jnp.concatenate([eg0_ref[h, j, 0:1, :], eg1_ref[h, jb, 0:1, :]], axis=1)
            s[h] = s[h] * dec + r2[h][CHUNK:]
    for h in heads:
        s_ref[h] = s[h]


def _dn_scan_call(wq, ak, u, eg, t):
    bsz, nh, _, nc, _, _ = wq.shape
    k = DN_SCAN_CHUNKS
    nb = nc // k
    nb_ctx = (nc - t // CHUNK) // k

    def bwd_block(i):
        return jnp.where(i < nb_ctx, nb_ctx - 1 - i, nb - 1 + nb_ctx - i)

    def spec(rows, d):
        if d == 0:
            return pl.BlockSpec((None, nh, None, k, rows, LANES), lambda b, i: (b, 0, 0, i, 0, 0))
        return pl.BlockSpec((None, nh, None, k, rows, LANES), lambda b, i: (b, 0, 1, bwd_block(i), 0, 0))

    o_shape = jax.ShapeDtypeStruct((bsz, t, nh * DN_DV), BF16)
    o0_spec = pl.BlockSpec((None, k * CHUNK, nh * DN_DV), lambda b, i: (b, jnp.maximum(i - nb_ctx, 0), 0))
    o1_spec = pl.BlockSpec((None, k * CHUNK, nh * DN_DV),
                           lambda b, i: (b, bwd_block(jnp.maximum(i, nb_ctx)) - nb_ctx, 0))
    return pl.pallas_call(
        _dn_scan_kernel,
        out_shape=(o_shape, o_shape),
        grid=(bsz, nb),
        in_specs=[spec(2 * CHUNK, 0), spec(2 * CHUNK, 1), spec(2 * CHUNK, 0), spec(2 * CHUNK, 1),
                  spec(CHUNK, 0), spec(CHUNK, 1), spec(8, 0), spec(8, 1)],
        out_specs=(o0_spec, o1_spec),
        scratch_shapes=[pltpu.VMEM((nh, DN_DK, 2 * DN_DV), F32)],
        compiler_params=_params(2),
        name="dn_scan",
    )(wq, wq, ak, ak, u, u, eg, eg)


def _attn_kernel(q_ref, kl_ref, vtl_ref, kc_ref, vtc_ref, o_ref, kmax_ref, shift_ref):
    t = kl_ref.shape[0]
    blocks = [(kl_ref, vtl_ref, j * AT_KEYS, AT_KEYS) for j in range(t // AT_KEYS)]
    blocks.append((kc_ref, vtc_ref, 0, kc_ref.shape[0]))

    @pl.when(pl.program_id(2) == 0)
    def _():
        best = jnp.zeros((1, 1), F32)
        for k_ref, _, off, size in blocks:
            kk = k_ref[off:off + size, :].astype(F32)
            best = jnp.maximum(best, jnp.max(jnp.sum(kk * kk, axis=-1, keepdims=True), axis=0, keepdims=True))
        kmax_ref[...] = jnp.broadcast_to(best, kmax_ref.shape)

    def rows_gen(r, rows):
        tr = rows.stop - rows.start
        n = 2 * tr
        q = q_ref[rows, :].astype(F32)
        qt32 = jnp.concatenate([q[:, :AT_HD].T, q[:, AT_HD:].T], axis=1)
        qt = qt32.astype(BF16)
        bound = jnp.sqrt(jnp.sum(qt32 * qt32, axis=0, keepdims=True) * kmax_ref[0:1, 0:1])
        shift_ref[r] = bound

        def scores(blk):
            k_ref, _, off, size = blk
            return _dot(k_ref[off:off + size, :], qt)

        @pl.when(jnp.max(bound) > AT_SAFE_LOG2)
        def _():
            m = jnp.full((1, n), NEG_BIG, F32)
            for blk in blocks:
                m = jnp.maximum(m, jnp.max(scores(blk), axis=0, keepdims=True))
            shift_ref[r] = m

        shift = shift_ref[r]
        den = jnp.zeros((1, n), F32)
        acc = jnp.zeros((AT_HD, n), F32)
        s_next = scores(blocks[0])
        yield
        for j, (_, vt_ref, off, size) in enumerate(blocks):
            s = s_next
            if j + 1 < len(blocks):
                s_next = scores(blocks[j + 1])
            p = jnp.exp2(s - shift)
            den = den + jnp.sum(p, axis=0, keepdims=True)
            acc = acc + _dot(vt_ref[:, off:off + size], p.astype(BF16))
            yield
        o = (acc / den).T
        o_ref[rows, :AT_HD] = o[:tr].astype(BF16)
        o_ref[rows, AT_HD:] = o[tr:].astype(BF16)

    _lockstep([rows_gen(0, slice(0, q_ref.shape[0]))])


def _attn_call(aq, ak_l, avt_l, ak_c, avt_c, tq):
    bsz, hkv, t, _ = aq.shape
    tc = ak_c.shape[2]
    kspec = lambda tt: pl.BlockSpec((None, None, tt, AT_HD), lambda b, j, i: (b, j, 0, 0))
    vspec = lambda tt: pl.BlockSpec((None, None, AT_HD, tt), lambda b, j, i: (b, j, 0, 0))
    return pl.pallas_call(
        _attn_kernel,
        out_shape=jax.ShapeDtypeStruct((bsz, t, AT_WIDTH), BF16),
        grid=(bsz, hkv, t // tq),
        in_specs=[pl.BlockSpec((None, None, tq, 2 * AT_HD), lambda b, j, i: (b, j, i, 0)),
                  kspec(t), vspec(t), kspec(tc), vspec(tc)],
        out_specs=pl.BlockSpec((None, tq, 2 * AT_HD), lambda b, j, i: (b, i, j)),
        scratch_shapes=[pltpu.VMEM((8, LANES), F32), pltpu.VMEM((1, 1, 2 * tq), F32)],
        compiler_params=_params(3),
        name="attn",
    )(aq, ak_l, avt_l, ak_c, avt_c)


def _tail_kernel(x_ref, o0_ref, o1_ref, z_ref, at_ref, mod_ref, gn_ref, wo_ref, g_ref, w1_ref, w3_ref, w2_ref,
                 gf_ref, o_ref, act_ref, dn_ref):
    def rows_gen(rows):
        for hd in range(DN_HEADS):
            sl = slice(hd * DN_DV, (hd + 1) * DN_DV)
            o = o0_ref[rows, sl].astype(F32) + o1_ref[rows, sl].astype(F32)
            dn_ref[rows, sl] = (_rms(o, gn_ref[...]) * _silu(z_ref[rows, sl].astype(F32))).astype(BF16)
        mix = _dot(dn_ref[rows, :], wo_ref[:DN_WIDTH, :]) + _dot(at_ref[rows, :], wo_ref[DN_WIDTH:, :])
        yield
        x = x_ref[rows, :] + mod_ref[5:6, :] * mix
        x = yield from _swiglu_update(x, mod_ref, 6, g_ref, w1_ref, w3_ref, w2_ref, act_ref, rows)
        o_ref[rows, :] = _rms(x, gf_ref[...])

    _lockstep([rows_gen(rows) for rows in _row_slices(x_ref.shape[0])])


def _tail_call(x2d, o0, o1, z2d, at2d, mod3, mod_row, gn, wo, g, w1, w3, w2, gf, tm):
    n, d = x2d.shape
    return pl.pallas_call(
        _tail_kernel,
        out_shape=jax.ShapeDtypeStruct((n, d), F32),
        grid=(n // tm,),
        in_specs=[pl.BlockSpec((tm, d), lambda i: (i, 0)),
                  pl.BlockSpec((tm, DN_WIDTH), lambda i: (i, 0)),
                  pl.BlockSpec((tm, DN_WIDTH), lambda i: (i, 0)),
                  pl.BlockSpec((tm, DN_WIDTH), lambda i: (i, 0)),
                  pl.BlockSpec((tm, AT_WIDTH), lambda i: (i, 0)),
                  pl.BlockSpec((None, N_MOD, d), lambda i: (mod_row(i), 0, 0)),
                  _const_spec((1, LANES)),
                  _const_spec(wo.shape), _const_spec((1, d)),
                  _const_spec(w1.shape), _const_spec(w3.shape), _const_spec(w2.shape),
                  _const_spec((1, d))],
        out_specs=pl.BlockSpec((tm, d), lambda i: (i, 0)),
        scratch_shapes=[pltpu.VMEM((tm, D_FF), BF16), pltpu.VMEM((tm, DN_WIDTH), BF16)],
        compiler_params=_params(1),
        name="tail",
    )(x2d, o0, o1, z2d, at2d, mod3, gn, wo, g, w1, w3, w2, gf)


def _rope_tables(n):
    pos = np.arange(n)
    freqs = 1.0 / (ROPE_THETA ** (np.arange(0, ROPE_AXIS_DIM, 2, dtype=np.float64) / ROPE_AXIS_DIM))
    ang = np.concatenate([(pos // GRID_W)[:, None] * freqs, (pos % GRID_W)[:, None] * freqs], axis=-1)
    sign = np.tile(np.array([-1.0, 1.0]), AT_HD // 2)
    cos = np.repeat(np.cos(ang), 2, axis=-1)
    sin = np.repeat(np.sin(ang), 2, axis=-1) * sign
    return jnp.asarray(cos, F32), jnp.asarray(sin, F32)


def _head_kind_lanes(b_part, a_part):
    lead = b_part.shape[:-1]
    kinds = jnp.stack([b_part.reshape(*lead, 2, DN_HEADS), a_part.reshape(*lead, 2, DN_HEADS)], axis=-3)
    kinds = jnp.moveaxis(kinds.reshape(*lead, 4, DN_HEADS), -1, -2)
    kinds = jnp.pad(kinds, [(0, 0)] * len(lead) + [(0, 0), (0, DN_BG_ROWS - 4)])
    flat = kinds.reshape(*lead, DN_HEADS * DN_BG_ROWS)
    return jnp.pad(flat, [(0, 0)] * len(lead) + [(0, LANES - DN_HEADS * DN_BG_ROWS)])


def _split_w_in(w_in):
    ba = _head_kind_lanes(w_in[:, OFF_DN_B:OFF_DN_A], w_in[:, OFF_DN_A:OFF_AT_Q])
    return w_in[:, :OFF_DN_B].astype(BF16), w_in[:, OFF_AT_Q:].astype(BF16), ba.astype(BF16)


def kernel(x, c, ctx, c_ctx, w_mod, b_mod, g_ffn1, ffn1_w1, ffn1_w3, ffn1_w2, g_mix, w_in, dn_conv, dn_a_log,
           dn_dt_bias, dn_norm, q_norm, k_norm, w_out, g_ffn2, ffn2_w1, ffn2_w3, ffn2_w2, g_final):
    bsz, t, d = x.shape
    tc = ctx.shape[1]
    assert w_mod.shape[0] == 1, "single-layer block"
    ctx_row = bsz

    cc = jnp.zeros((8, d), F32).at[:bsz].set(c).at[ctx_row].set(c_ctx)
    mod3 = _mod_call(cc, w_mod[0], b_mod[0][None]).reshape(8, N_MOD, d)

    lat_row = lambda tile: (lambda i: i // (t // tile))
    ctx_rowf = lambda i: ctx_row
    row = lambda v: v.reshape(1, -1)

    w1a, w3a, w2a = ffn1_w1[0].astype(BF16), ffn1_w3[0].astype(BF16), ffn1_w2[0].astype(BF16)
    x1 = _ffn_call(x.reshape(bsz * t, d), mod3, lat_row(TM_FFN), row(g_ffn1[0]), w1a, w3a, w2a, TM_FFN, "ffn1_lat")
    c1 = _ffn_call(ctx.reshape(bsz * tc, d), mod3, ctx_rowf, row(g_ffn1[0]), w1a, w3a, w2a, TM_FFN, "ffn1_ctx")

    w_parts = _split_w_in(w_in[0])
    no_b = jnp.zeros((1, 2 * DN_HEADS), F32)
    alog_row = _head_kind_lanes(no_b, dn_a_log[0].reshape(1, -1))
    dtb_row = _head_kind_lanes(no_b, dn_dt_bias[0].reshape(1, -1))
    cos, sin = _rope_tables(t)
    qn, kn = row(q_norm[0]), row(k_norm[0])
    qkv_l, z_l, bg_l, aq_l, ak_l, av_l = _inproj_call(
        x1, mod3, lat_row(TM_INPROJ), row(g_mix[0]), w_parts, alog_row, dtb_row, qn, kn, cos, sin, bsz, t, TM_INPROJ, True,
        "inproj_lat")
    qkv_c, _, bg_c, _, ak_c, av_c = _inproj_call(
        c1, mod3, ctx_rowf, row(g_mix[0]), w_parts, alog_row, dtb_row, qn, kn, cos[:tc], sin[:tc], bsz, tc, tc, False,
        "inproj_ctx")

    bg_all = jnp.concatenate([bg_c, bg_l], axis=2)
    bg_all = bg_all.reshape(bsz, DN_HEADS, DN_BG_ROWS, (tc + t) // CHUNK, CHUNK).transpose(0, 1, 3, 2, 4)
    conv_w = dn_conv[0].reshape(DN_CONV, 3 * DN_HEADS, LANES).transpose(1, 0, 2)
    dn_wq, dn_ak, dn_u, dn_eg = _dn_prep_call(qkv_l, qkv_c, conv_w, bg_all, _dn_masks())
    o_fwd, o_bwd = _dn_scan_call(dn_wq, dn_ak, dn_u, dn_eg, t)
    at_lat = _attn_call(aq_l, ak_l, av_l, ak_c, av_c, 1024)

    out = _tail_call(x1, o_fwd.reshape(bsz * t, DN_WIDTH), o_bwd.reshape(bsz * t, DN_WIDTH), z_l,
                     at_lat.reshape(bsz * t, AT_WIDTH), mod3, lat_row(TM_TAIL), row(dn_norm[0]), w_out[0].astype(BF16),
                     row(g_ffn2[0]), ffn2_w1[0].astype(BF16), ffn2_w3[0].astype(BF16), ffn2_w2[0].astype(BF16),
                     row(g_final), TM_TAIL)
    return out.reshape(bsz, t, d)
```

```python
import functools

import numpy as np
import jax
import jax.numpy as jnp
from jax import lax
from jax.experimental import pallas as pl
from jax.experimental.pallas import tpu as pltpu

F32 = jnp.float32
BF16 = jnp.bfloat16

D_MODEL = 1024
CTX_LEN = 256
GRID_W = 64
EPS = 1e-6
N_MOD = 9
D_FF = 2816
DN_HEADS = 4
DN_DK = 128
DN_DV = 128
DN_WIDTH = DN_HEADS * DN_DV
DN_CONV = 5
AT_HEADS = 4
AT_KV_HEADS = 2
AT_HD = 128
AT_WIDTH = AT_HEADS * AT_HD
ATT_SCALE = AT_HD ** -0.5
ROPE_AXIS_DIM = AT_HD // 2
ROPE_THETA = 10000.0
LEN_DN_QKV = 3 * DN_WIDTH
OFF_DN_Z = LEN_DN_QKV
OFF_DN_B = OFF_DN_Z + DN_WIDTH
OFF_DN_A = OFF_DN_B + 2 * DN_HEADS
OFF_AT_Q = OFF_DN_A + 2 * DN_HEADS
OFF_AT_K = OFF_AT_Q + AT_WIDTH
OFF_AT_V = OFF_AT_K + AT_KV_HEADS * AT_HD
P_IN = OFF_AT_V + AT_KV_HEADS * AT_HD

LANES = 128
CHUNK = 128
TM_FFN = 512
TM_INPROJ = 512
TM_TAIL = 512
FF_TILE = 256
W_STAGE_CHUNKS = 8
ROW_SPLIT = 2
DN_GROUP = 17
DN_BG_ROWS = 8
DN_SCAN_CHUNKS = 2
DN_CONV_BLOCKS = 2
AT_KEYS = 1024
LOG2E = 1.4426950408889634
AT_SAFE_LOG2 = 60.0
VMEM_LIMIT = 56 * 1024 * 1024
NEG_BIG = -1e30

_NT = (((1,), (1,)), ((), ()))


def _sigmoid(x):
    return 1.0 / (1.0 + jnp.exp(-x))


def _silu(x):
    return x * _sigmoid(x)


def _rms(x, gain):
    ms = jnp.mean(x * x, axis=-1, keepdims=True)
    return x * lax.rsqrt(ms + EPS) * gain


def _dot(a, b):
    return jnp.dot(a, b, preferred_element_type=F32)


def _const_spec(shape):
    nd = len(shape)
    return pl.BlockSpec(shape, lambda *_: (0,) * nd, pipeline_mode=pl.Buffered(1))


def _params(n_axes):
    return pltpu.CompilerParams(dimension_semantics=("arbitrary",) * n_axes,
                                vmem_limit_bytes=VMEM_LIMIT)


def _mod_kernel(c_ref, w_ref, b_ref, o_ref):
    s = _silu(c_ref[...]).astype(BF16)
    o_ref[...] = _dot(s, w_ref[...].astype(BF16)) + b_ref[...]


def _mod_call(cc, w_mod, b_mod):
    d = cc.shape[1]
    n = w_mod.shape[1]
    return pl.pallas_call(
        _mod_kernel,
        out_shape=jax.ShapeDtypeStruct((cc.shape[0], n), F32),
        grid=(n // d,),
        in_specs=[pl.BlockSpec(cc.shape, lambda j: (0, 0)),
                  pl.BlockSpec((d, d), lambda j: (0, j)),
                  pl.BlockSpec((1, d), lambda j: (0, j))],
        out_specs=pl.BlockSpec((cc.shape[0], d), lambda j: (0, j)),
        compiler_params=_params(1),
        name="mod",
    )(cc, w_mod, b_mod)


def _lockstep(gens):
    results = [None] * len(gens)
    alive = list(range(len(gens)))
    while alive:
        still = []
        for i in alive:
            try:
                next(gens[i])
                still.append(i)
            except StopIteration as done:
                results[i] = done.value
        alive = still
    return results


def _row_slices(tm):
    sub = tm // ROW_SPLIT
    return [slice(r * sub, (r + 1) * sub) for r in range(ROW_SPLIT)]


def _swiglu_update(x, mod_ref, j0, g_ref, w1_ref, w3_ref, w2_ref, act_ref, rows):
    h = (_rms(x, g_ref[...]) * (1.0 + mod_ref[j0 + 1:j0 + 2, :]) + mod_ref[j0:j0 + 1, :]).astype(BF16)
    for c in range(D_FF // FF_TILE):
        sl = slice(c * FF_TILE, (c + 1) * FF_TILE)
        a = _dot(h, w1_ref[:, sl])
        b = _dot(h, w3_ref[:, sl])
        yield
        act_ref[rows, sl] = (_silu(a) * b).astype(BF16)
    y = _dot(act_ref[rows, :], w2_ref[...])
    yield
    return x + (0.5 * mod_ref[j0 + 2:j0 + 3, :]) * y


def _ffn_kernel(x_ref, mod_ref, g_ref, w1_ref, w3_ref, w2_ref, o_ref, act_ref):
    slices = _row_slices(x_ref.shape[0])
    outs = _lockstep([_swiglu_update(x_ref[rows, :], mod_ref, 0, g_ref, w1_ref, w3_ref, w2_ref, act_ref, rows)
                      for rows in slices])
    for rows, out in zip(slices, outs):
        o_ref[rows, :] = out


def _stream_cast(w_hbm, w_vmem, stage_ref, sem):
    rows = stage_ref.shape[1]
    n = w_hbm.shape[0] // rows

    def chunk_copy(i):
        return pltpu.make_async_copy(w_hbm.at[pl.ds(i * rows, rows), :], stage_ref.at[i % 2], sem.at[i % 2])

    chunk_copy(0).start()
    for i in range(n):
        if i + 1 < n:
            chunk_copy(i + 1).start()
        chunk_copy(i).wait()
        w_vmem[i * rows:(i + 1) * rows, :] = stage_ref[i % 2].astype(BF16)


def _load_ffn_weights(w_hbm, w_vmem, stage_a, stage_b, sem):
    _stream_cast(w_hbm[0], w_vmem[0], stage_a, sem)
    _stream_cast(w_hbm[1], w_vmem[1], stage_a, sem)
    _stream_cast(w_hbm[2], w_vmem[2], stage_b, sem)


def _ffn_cast_kernel(x_ref, mod_ref, g_ref, w1_hbm, w3_hbm, w2_hbm, o_ref, w1b_hbm, w3b_hbm, w2b_hbm,
                     act_ref, w1_v, w3_v, w2_v, stage_a, stage_b, sem, out_sem):
    w_v = (w1_v, w3_v, w2_v)
    exports = [pltpu.make_async_copy(v, h, out_sem.at[j]) for j, (v, h) in enumerate(zip(w_v, (w1b_hbm, w3b_hbm, w2b_hbm)))]

    @pl.when(pl.program_id(0) == 0)
    def _():
        _load_ffn_weights((w1_hbm, w3_hbm, w2_hbm), w_v, stage_a, stage_b, sem)
        for e in exports:
            e.start()

    _ffn_kernel(x_ref, mod_ref, g_ref, w1_v, w3_v, w2_v, o_ref, act_ref)

    @pl.when(pl.program_id(0) == pl.num_programs(0) - 1)
    def _():
        for e in exports:
            e.wait()


def _ffn_weight_scratch(w1, w2):
    return [pltpu.VMEM(w1.shape, BF16), pltpu.VMEM(w1.shape, BF16), pltpu.VMEM(w2.shape, BF16),
            pltpu.VMEM((2, w1.shape[0] // W_STAGE_CHUNKS, w1.shape[1]), F32),
            pltpu.VMEM((2, w2.shape[0] // W_STAGE_CHUNKS, w2.shape[1]), F32),
            pltpu.SemaphoreType.DMA((2,))]


def _ffn_cast_call(x2d, mod3, mod_row, g, w1, w3, w2, tm, name):
    n, d = x2d.shape
    any_spec = pl.BlockSpec(memory_space=pl.ANY)
    return pl.pallas_call(
        _ffn_cast_kernel,
        out_shape=(jax.ShapeDtypeStruct((n, d), F32), jax.ShapeDtypeStruct(w1.shape, BF16),
                   jax.ShapeDtypeStruct(w3.shape, BF16), jax.ShapeDtypeStruct(w2.shape, BF16)),
        grid=(n // tm,),
        in_specs=[pl.BlockSpec((tm, d), lambda i: (i, 0)),
                  pl.BlockSpec((None, N_MOD, d), lambda i: (mod_row(i), 0, 0)),
                  _const_spec((1, d)), any_spec, any_spec, any_spec],
        out_specs=(pl.BlockSpec((tm, d), lambda i: (i, 0)), any_spec, any_spec, any_spec),
        scratch_shapes=[pltpu.VMEM((tm, D_FF), BF16)] + _ffn_weight_scratch(w1, w2) + [pltpu.SemaphoreType.DMA((3,))],
        compiler_params=_params(1),
        name=name,
    )(x2d, mod3, g, w1, w3, w2)


def _ffn_call(x2d, mod3, mod_row, g, w1, w3, w2, tm, name):
    n, d = x2d.shape
    return pl.pallas_call(
        _ffn_kernel,
        out_shape=jax.ShapeDtypeStruct((n, d), F32),
        grid=(n // tm,),
        in_specs=[pl.BlockSpec((tm, d), lambda i: (i, 0)),
                  pl.BlockSpec((None, N_MOD, d), lambda i: (mod_row(i), 0, 0)),
                  _const_spec((1, d)),
                  _const_spec(w1.shape), _const_spec(w3.shape), _const_spec(w2.shape)],
        out_specs=pl.BlockSpec((tm, d), lambda i: (i, 0)),
        scratch_shapes=[pltpu.VMEM((tm, D_FF), BF16)],
        compiler_params=_params(1),
        name=name,
    )(x2d, mod3, g, w1, w3, w2)


def _rope(x, cos, sin_signed, even):
    nxt = pltpu.roll(x, LANES - 1, axis=1)
    prv = pltpu.roll(x, 1, axis=1)
    return x * cos + jnp.where(even, nxt, prv) * sin_signed


def _inproj_kernel(x_ref, mod_ref, g_ref, wdn_ref, wat_ref, wba_ref, alog_ref, dtb_ref, qn_ref, kn_ref, cos_ref, sin_ref,
                   qkv_ref, z_ref, bg_ref, aq_ref, ak_ref, av_ref, *, rope):
    tm = x_ref.shape[0]
    x = x_ref[...]
    h = (_rms(x, g_ref[...]) * (1.0 + mod_ref[4:5, :]) + mod_ref[3:4, :]).astype(BF16)

    p_ba = _dot(h, wba_ref[...])
    p_q = _dot(h, wat_ref[:, :AT_WIDTH])

    beta = _sigmoid(p_ba)
    t = p_ba + dtb_ref[...]
    softplus = jnp.maximum(t, 0.0) + jnp.log(1.0 + jnp.exp(-jnp.abs(t)))
    g = -jnp.exp(alog_ref[...]) * softplus
    ri = lax.broadcasted_iota(jnp.int32, (CHUNK, CHUNK), 0)
    ci = lax.broadcasted_iota(jnp.int32, (CHUNK, CHUNK), 1)
    ltri = jnp.where(ri >= ci, 1.0, 0.0).astype(BF16)
    kind = lax.broadcasted_iota(jnp.int32, (CHUNK, LANES), 1) % DN_BG_ROWS
    for c in range(tm // CHUNK):
        rows = slice(c * CHUNK, (c + 1) * CHUNK)
        gc = g[rows]
        g1 = gc.astype(BF16)
        r1 = gc - g1.astype(F32)
        g2 = r1.astype(BF16)
        g3 = (r1 - g2.astype(F32)).astype(BF16)
        pre = _dot(ltri, g1) + _dot(ltri, g2) + _dot(ltri, g3)
        suf = pre[CHUNK - 1:CHUNK, :] - pre + gc
        tile = jnp.where(kind < 2, beta[rows], jnp.where(kind == 2, pre, suf))
        bg_ref[:, rows] = tile.T[:DN_HEADS * DN_BG_ROWS]

    p_kv = _dot(h, wat_ref[:, AT_WIDTH:])
    lane = lax.broadcasted_iota(jnp.int32, (tm, LANES), 1)
    even = (lane % 2) == 0
    if rope:
        cos = cos_ref[...]
        sin = sin_ref[...]
    for hd in range(AT_HEADS):
        q = _rms(p_q[:, hd * LANES:(hd + 1) * LANES], qn_ref[...])
        if rope:
            q = _rope(q, cos, sin, even)
        aq_ref[hd // 2, :, (hd % 2) * LANES:(hd % 2 + 1) * LANES] = (q * (ATT_SCALE * LOG2E)).astype(BF16)
    for hd in range(AT_KV_HEADS):
        k = _rms(p_kv[:, hd * LANES:(hd + 1) * LANES], kn_ref[...])
        if rope:
            k = _rope(k, cos, sin, even)
        ak_ref[hd] = k.astype(BF16)
        av_ref[hd] = p_kv[:, (AT_KV_HEADS + hd) * LANES:(AT_KV_HEADS + hd + 1) * LANES].T.astype(BF16)

    z_ref[...] = _dot(h, wdn_ref[:, LEN_DN_QKV:]).astype(BF16)
    for part in range(3):
        p = _dot(h, wdn_ref[:, part * DN_WIDTH:(part + 1) * DN_WIDTH])
        for hd in range(DN_HEADS):
            qkv_ref[part * DN_HEADS + hd] = p[:, hd * LANES:(hd + 1) * LANES].astype(BF16)


def _inproj_call(x2d, mod3, mod_row, g, w_parts, alog_row, dtb_row, qn, kn, cos, sin, bsz, t, tm, rope, name):
    n, d = x2d.shape
    tpb = t // tm
    bt = lambda i: (i // tpb, 0, i % tpb, 0)
    out_shape = (
        jax.ShapeDtypeStruct((bsz, 3 * DN_HEADS, t, LANES), BF16),
        jax.ShapeDtypeStruct((n, DN_WIDTH), BF16),
        jax.ShapeDtypeStruct((bsz, DN_HEADS * DN_BG_ROWS, t), F32),
        jax.ShapeDtypeStruct((bsz, AT_KV_HEADS, t, 2 * AT_HD), BF16),
        jax.ShapeDtypeStruct((bsz, AT_KV_HEADS, t, AT_HD), BF16),
        jax.ShapeDtypeStruct((bsz, AT_KV_HEADS, AT_HD, t), BF16),
    )
    out_specs = (
        pl.BlockSpec((None, 3 * DN_HEADS, tm, LANES), bt),
        pl.BlockSpec((tm, DN_WIDTH), lambda i: (i, 0)),
        pl.BlockSpec((None, DN_HEADS * DN_BG_ROWS, tm), lambda i: (i // tpb, 0, i % tpb)),
        pl.BlockSpec((None, AT_KV_HEADS, tm, 2 * AT_HD), bt),
        pl.BlockSpec((None, AT_KV_HEADS, tm, AT_HD), bt),
        pl.BlockSpec((None, AT_KV_HEADS, AT_HD, tm), lambda i: (i // tpb, 0, 0, i % tpb)),
    )
    return pl.pallas_call(
        functools.partial(_inproj_kernel, rope=rope),
        out_shape=out_shape,
        grid=(n // tm,),
        in_specs=[pl.BlockSpec((tm, d), lambda i: (i, 0)),
                  pl.BlockSpec((None, N_MOD, d), lambda i: (mod_row(i), 0, 0)),
                  _const_spec((1, d)),
                  *[_const_spec(w.shape) for w in w_parts],
                  _const_spec((1, LANES)), _const_spec((1, LANES)),
                  _const_spec((1, LANES)), _const_spec((1, LANES)),
                  pl.BlockSpec((tm, LANES), lambda i: (i % tpb, 0)),
                  pl.BlockSpec((tm, LANES), lambda i: (i % tpb, 0))],
        out_specs=out_specs,
        compiler_params=_params(1),
        name=name,
    )(x2d, mod3, g, *w_parts, alog_row, dtb_row, qn, kn, cos, sin)


def _dn_conv_all(src_refs, w_refs, xpad_ref, t, emits):
    nblk = t // CHUNK
    n = len(src_refs)
    for i in range(n):
        xpad_ref[i, 0:8, :] = jnp.zeros((8, LANES), F32)
        xpad_ref[i, 8 + t:16 + t, :] = jnp.zeros((8, LANES), F32)

    def fill(r, carry):
        r0 = pl.multiple_of(r * CHUNK, CHUNK)
        for i in range(n):
            xpad_ref[i, pl.ds(r0 + 8, CHUNK), :] = src_refs[i][pl.ds(r0, CHUNK), :].astype(F32)
        return carry

    lax.fori_loop(0, nblk, fill, 0)
    ws = [w_ref[...] for w_ref in w_refs]
    pad = (DN_CONV - 1) // 2

    def conv(r, carry):
        for u in range(DN_CONV_BLOCKS):
            r0 = pl.multiple_of((r * DN_CONV_BLOCKS + u) * CHUNK, CHUNK)
            for i in range(n):
                acc = None
                for j in range(DN_CONV):
                    tap = xpad_ref[i, pl.ds(r0 + (8 - pad + j), CHUNK), :] * ws[i][j:j + 1, :]
                    acc = tap if acc is None else acc + tap
                emits[i](r0, _silu(acc))
        return carry

    lax.fori_loop(0, nblk // DN_CONV_BLOCKS, conv, 0)


def _block_diag(cat):
    c = cat.shape[0]
    zero = jnp.zeros((c, c), cat.dtype)
    return jnp.concatenate([jnp.concatenate([cat[:, :c], zero], axis=1),
                            jnp.concatenate([zero, cat[:, c:]], axis=1)], axis=0)


def _dn_prep_chunk(a, qs_ref, ks_ref, vs_ref, bg_ref, lvl_ref, wq_ref, ak_ref, u_ref, eg_ref):
    row = pl.multiple_of(a * CHUNK, CHUNK)
    k = ks_ref[pl.ds(row, CHUNK), :]
    q = qs_ref[pl.ds(row, CHUNK), :]
    v = vs_ref[pl.ds(row, CHUNK), :]
    bg = bg_ref[a]
    kf = k.astype(F32)
    ri = lax.broadcasted_iota(jnp.int32, (CHUNK, CHUNK), 0)
    ci = lax.broadcasted_iota(jnp.int32, (CHUNK, CHUNK), 1)
    per_dir = []
    for d in range(2):
        beta = jnp.broadcast_to(bg[d:d + 1, :], (CHUNK, CHUNK)).T
        g_r = jnp.broadcast_to(bg[2 + d:3 + d, :], (CHUNK, CHUNK))
        g_c = g_r.T
        incl = (ri >= ci) if d == 0 else (ri <= ci)
        decay = jnp.exp(jnp.where(incl, g_c - g_r, NEG_BIG))
        per_dir.append((beta, g_c, decay, kf * beta))
    aq = lax.dot_general(jnp.concatenate([per_dir[0][3].astype(BF16), per_dir[1][3].astype(BF16), q], axis=0),
                         k, _NT, preferred_element_type=F32)
    yield
    qk = aq[2 * CHUNK:]
    a_cat = jnp.concatenate([jnp.where(ri > ci, aq[:CHUNK] * per_dir[0][2], 0.0),
                             jnp.where(ri < ci, aq[CHUNK:2 * CHUNK] * per_dir[1][2], 0.0)], axis=1)

    rhs = []
    for d in range(2):
        beta, g_c, decay, kb = per_dir[d]
        eg = jnp.exp(g_c)
        g_last = g_c[CHUNK - 1:CHUNK, :] if d == 0 else g_c[0:1, :]
        k_tail = kf * jnp.exp(g_last - g_c)
        ak_ref[d, a] = jnp.concatenate([(qk * decay).astype(BF16), k_tail.T.astype(BF16)], axis=0)
        wq_ref[d, a, CHUNK:, :] = (q.astype(F32) * eg).astype(BF16)
        eg_ref[d, a] = jnp.broadcast_to(jnp.exp(g_last), (8, LANES))
        rhs.append(jnp.concatenate([v * beta, kb * eg], axis=1))

    eye = jnp.where(ri == ci, 1.0, 0.0)
    eye_cat = jnp.concatenate([eye, eye], axis=1)
    pair = jnp.where((ri >> 1) == (ci >> 1), 1.0, 0.0)
    t_cat = (eye_cat - a_cat * jnp.concatenate([pair, pair], axis=1)).astype(BF16)
    a_bf = a_cat.astype(BF16)
    for lvl in range(lvl_ref.shape[0]):
        x = _dot(t_cat, _block_diag(a_bf * lvl_ref[lvl]))
        yield
        t_cat = t_cat - _dot(x.astype(BF16), _block_diag(t_cat)).astype(BF16)
        yield
    toff = t_cat - eye_cat.astype(BF16)

    for d in range(2):
        uw = rhs[d] + _dot(toff[:, d * CHUNK:(d + 1) * CHUNK], rhs[d].astype(BF16))
        u_ref[d, a] = uw[:, :DN_DV].astype(BF16)
        wq_ref[d, a, :CHUNK, :] = uw[:, DN_DV:].astype(BF16)


def _dn_prep_kernel(ql_ref, kl_ref, vl_ref, qc_ref, kc_ref, vc_ref, wq_w_ref, wk_w_ref, wv_w_ref, bg_ref,
                    lvl_ref, wq_ref, ak_ref, u_ref, eg_ref, qs_ref, ks_ref, vs_ref, xpad_ref):
    t = ql_ref.shape[0]
    tc = qc_ref.shape[0]

    def emit_qk(dst_ref, base, scale):
        def emit(r0, y):
            yn = y * lax.rsqrt(jnp.sum(y * y, axis=-1, keepdims=True) + EPS)
            dst_ref[pl.ds(base + r0, CHUNK), :] = (yn * scale).astype(BF16)
        return emit

    def emit_v(base):
        def emit(r0, y):
            vs_ref[pl.ds(base + r0, CHUNK), :] = y
        return emit

    w_refs = (wq_w_ref, wk_w_ref, wv_w_ref)
    for srcs, tt, base in (((qc_ref, kc_ref, vc_ref), tc, 0), ((ql_ref, kl_ref, vl_ref), t, tc)):
        _dn_conv_all(srcs, w_refs, xpad_ref, tt,
                     (emit_qk(qs_ref, base, DN_DK ** -0.5), emit_qk(ks_ref, base, 1.0), emit_v(base)))

    refs = (qs_ref, ks_ref, vs_ref, bg_ref, lvl_ref, wq_ref, ak_ref, u_ref, eg_ref)

    def group(gi, carry):
        _lockstep([_dn_prep_chunk(gi * DN_GROUP + j, *refs) for j in range(DN_GROUP)])
        return carry

    lax.fori_loop(0, (tc + t) // (CHUNK * DN_GROUP), group, 0)


def _dn_masks():
    i = np.arange(CHUNK)
    lvls = []
    b = 2
    while b < CHUNK:
        rb, cb = (i // b)[:, None], (i // b)[None, :]
        fwd = ((rb == cb + 1) & (rb % 2 == 1)).astype(np.float32)
        lvls.append(np.concatenate([fwd, fwd.T], axis=1))
        b *= 2
    return jnp.asarray(np.stack(lvls), dtype=BF16)


def _dn_prep_call(qkv_l, qkv_c, conv_w, bg_all, lvl_masks):
    bsz, _, t, _ = qkv_l.shape
    tc = qkv_c.shape[2]
    ta = t + tc
    nc = ta // CHUNK

    def head(part, tt):
        return pl.BlockSpec((None, None, tt, LANES), lambda b, h: (b, part * DN_HEADS + h, 0, 0))

    def wspec(part):
        return pl.BlockSpec((None, DN_CONV, LANES), lambda b, h: (part * DN_HEADS + h, 0, 0))

    def out(rows, dt):
        return (jax.ShapeDtypeStruct((bsz, DN_HEADS, 2, nc, rows, LANES), dt),
                pl.BlockSpec((None, None, 2, nc, rows, LANES), lambda b, h: (b, h, 0, 0, 0, 0)))

    outs = [out(2 * CHUNK, BF16), out(2 * CHUNK, BF16), out(CHUNK, BF16), out(8, F32)]
    return pl.pallas_call(
        _dn_prep_kernel,
        out_shape=tuple(o[0] for o in outs),
        grid=(bsz, DN_HEADS),
        in_specs=[head(0, t), head(1, t), head(2, t), head(0, tc), head(1, tc), head(2, tc),
                  wspec(0), wspec(1), wspec(2),
                  pl.BlockSpec((None, None, nc, DN_BG_ROWS, LANES), lambda b, h: (b, h, 0, 0, 0)),
                  _const_spec(lvl_masks.shape)],
        out_specs=tuple(o[1] for o in outs),
        scratch_shapes=[pltpu.VMEM((ta, LANES), BF16), pltpu.VMEM((ta, LANES), BF16),
                        pltpu.VMEM((ta, LANES), F32), pltpu.VMEM((3, t + 16, LANES), F32)],
        compiler_params=_params(2),
        name="dn_prep",
    )(qkv_l, qkv_l, qkv_l, qkv_c, qkv_c, qkv_c, conv_w, conv_w, conv_w, bg_all, lvl_masks)


def _dn_scan_kernel(wq0_ref, wq1_ref, ak0_ref, ak1_ref, u0_ref, u1_ref, eg0_ref, eg1_ref, o0_ref, o1_ref, s_ref):
    @pl.when(pl.program_id(1) == 0)
    def _():
        s_ref[...] = jnp.zeros(s_ref.shape, F32)

    heads = range(DN_HEADS)
    s = [s_ref[h] for h in heads]
    for j in range(DN_SCAN_CHUNKS):
        jb = DN_SCAN_CHUNKS - 1 - j
        r1 = [_dot(jnp.concatenate([wq0_ref[h, j], wq1_ref[h, jb]], axis=1), _block_diag(s[h].astype(BF16)))
              for h in heads]
        v_new = [jnp.concatenate([u0_ref[h, j], u1_ref[h, jb]], axis=1).astype(F32) - r1[h][:CHUNK] for h in heads]
        r2 = [_dot(jnp.concatenate([ak0_ref[h, j], ak1_ref[h, jb]], axis=1), _block_diag(v_new[h].astype(BF16)))
              for h in heads]
        for h in heads:
            o = r1[h][CHUNK:] + r2[h][:CHUNK]
            o0_ref[j * CHUNK:(j + 1) * CHUNK, h * DN_DV:(h + 1) * DN_DV] = o[:, :DN_DV].astype(BF16)
            o1_ref[jb * CHUNK:(jb + 1) * CHUNK, h * DN_DV:(h + 1) * DN_DV] = o[:, DN_DV:].astype(BF16)
            dec = jnp.concatenate([eg0_ref[h, j, 0:1, :], eg1_ref[h, jb, 0:1, :]], axis=1)
            s[h] = s[h] * dec + r2[h][CHUNK:]
    for h in heads:
        s_ref[h] = s[h]


def _dn_scan_call(wq, ak, u, eg, t):
    bsz, nh, _, nc, _, _ = wq.shape
    k = DN_SCAN_CHUNKS
    nb = nc // k
    nb_ctx = (nc - t // CHUNK) // k

    def bwd_block(i):
        return jnp.where(i < nb_ctx, nb_ctx - 1 - i, nb - 1 + nb_ctx - i)

    def spec(rows, d):
        if d == 0:
            return pl.BlockSpec((None, nh, None, k, rows, LANES), lambda b, i: (b, 0, 0, i, 0, 0))
        return pl.BlockSpec((None, nh, None, k, rows, LANES), lambda b, i: (b, 0, 1, bwd_block(i), 0, 0))

    o_shape = jax.ShapeDtypeStruct((bsz, t, nh * DN_DV), BF16)
    o0_spec = pl.BlockSpec((None, k * CHUNK, nh * DN_DV), lambda b, i: (b, jnp.maximum(i - nb_ctx, 0), 0))
    o1_spec = pl.BlockSpec((None, k * CHUNK, nh * DN_DV),
                           lambda b, i: (b, bwd_block(jnp.maximum(i, nb_ctx)) - nb_ctx, 0))
    return pl.pallas_call(
        _dn_scan_kernel,
        out_shape=(o_shape, o_shape),
        grid=(bsz, nb),
        in_specs=[spec(2 * CHUNK, 0), spec(2 * CHUNK, 1), spec(2 * CHUNK, 0), spec(2 * CHUNK, 1),
                  spec(CHUNK, 0), spec(CHUNK, 1), spec(8, 0), spec(8, 1)],
        out_specs=(o0_spec, o1_spec),
        scratch_shapes=[pltpu.VMEM((nh, DN_DK, 2 * DN_DV), F32)],
        compiler_params=_params(2),
        name="dn_scan",
    )(wq, wq, ak, ak, u, u, eg, eg)


def _attn_kernel(q_ref, kl_ref, vtl_ref, kc_ref, vtc_ref, o_ref, kmax_ref, shift_ref):
    t = kl_ref.shape[0]
    blocks = [(kl_ref, vtl_ref, j * AT_KEYS, AT_KEYS) for j in range(t // AT_KEYS)]
    blocks.append((kc_ref, vtc_ref, 0, kc_ref.shape[0]))

    @pl.when(pl.program_id(2) == 0)
    def _():
        best = jnp.zeros((1, 1), F32)
        for k_ref, _, off, size in blocks:
            kk = k_ref[off:off + size, :].astype(F32)
            best = jnp.maximum(best, jnp.max(jnp.sum(kk * kk, axis=-1, keepdims=True), axis=0, keepdims=True))
        kmax_ref[...] = jnp.broadcast_to(best, kmax_ref.shape)

    def rows_gen(r, rows):
        tr = rows.stop - rows.start
        n = 2 * tr
        q = q_ref[rows, :].astype(F32)
        qt32 = jnp.concatenate([q[:, :AT_HD].T, q[:, AT_HD:].T], axis=1)
        qt = qt32.astype(BF16)
        bound = jnp.sqrt(jnp.sum(qt32 * qt32, axis=0, keepdims=True) * kmax_ref[0:1, 0:1])
        shift_ref[r] = bound

        def scores(blk):
            k_ref, _, off, size = blk
            return _dot(k_ref[off:off + size, :], qt)

        @pl.when(jnp.max(bound) > AT_SAFE_LOG2)
        def _():
            m = jnp.full((1, n), NEG_BIG, F32)
            for blk in blocks:
                m = jnp.maximum(m, jnp.max(scores(blk), axis=0, keepdims=True))
            shift_ref[r] = m

        shift = shift_ref[r]
        den = jnp.zeros((1, n), F32)
        acc = jnp.zeros((AT_HD, n), F32)
        s_next = scores(blocks[0])
        yield
        for j, (_, vt_ref, off, size) in enumerate(blocks):
            s = s_next
            if j + 1 < len(blocks):
                s_next = scores(blocks[j + 1])
            p = jnp.exp2(s - shift)
            den = den + jnp.sum(p, axis=0, keepdims=True)
            acc = acc + _dot(vt_ref[:, off:off + size], p.astype(BF16))
            yield
        o = (acc / den).T
        o_ref[rows, :AT_HD] = o[:tr].astype(BF16)
        o_ref[rows, AT_HD:] = o[tr:].astype(BF16)

    _lockstep([rows_gen(0, slice(0, q_ref.shape[0]))])


def _attn_call(aq, ak_l, avt_l, ak_c, avt_c, tq):
    bsz, hkv, t, _ = aq.shape
    tc = ak_c.shape[2]
    kspec = lambda tt: pl.BlockSpec((None, None, tt, AT_HD), lambda b, j, i: (b, j, 0, 0))
    vspec = lambda tt: pl.BlockSpec((None, None, AT_HD, tt), lambda b, j, i: (b, j, 0, 0))
    return pl.pallas_call(
        _attn_kernel,
        out_shape=jax.ShapeDtypeStruct((bsz, t, AT_WIDTH), BF16),
        grid=(bsz, hkv, t // tq),
        in_specs=[pl.BlockSpec((None, None, tq, 2 * AT_HD), lambda b, j, i: (b, j, i, 0)),
                  kspec(t), vspec(t), kspec(tc), vspec(tc)],
        out_specs=pl.BlockSpec((None, tq, 2 * AT_HD), lambda b, j, i: (b, i, j)),
        scratch_shapes=[pltpu.VMEM((8, LANES), F32), pltpu.VMEM((1, 1, 2 * tq), F32)],
        compiler_params=_params(3),
        name="attn",
    )(aq, ak_l, avt_l, ak_c, avt_c)


def _tail_kernel(x_ref, o0_ref, o1_ref, z_ref, at_ref, mod_ref, gn_ref, wo_ref, g_ref, w1_hbm, w3_hbm, w2_hbm,
                 gf_ref, o_ref, act_ref, dn_ref, w1_ref, w3_ref, w2_ref, stage_a, stage_b, sem):
    @pl.when(pl.program_id(0) == 0)
    def _():
        _load_ffn_weights((w1_hbm, w3_hbm, w2_hbm), (w1_ref, w3_ref, w2_ref), stage_a, stage_b, sem)

    def rows_gen(rows):
        for hd in range(DN_HEADS):
            sl = slice(hd * DN_DV, (hd + 1) * DN_DV)
            o = o0_ref[rows, sl].astype(F32) + o1_ref[rows, sl].astype(F32)
            dn_ref[rows, sl] = (_rms(o, gn_ref[...]) * _silu(z_ref[rows, sl].astype(F32))).astype(BF16)
        mix = _dot(dn_ref[rows, :], wo_ref[:DN_WIDTH, :]) + _dot(at_ref[rows, :], wo_ref[DN_WIDTH:, :])
        yield
        x = x_ref[rows, :] + mod_ref[5:6, :] * mix
        x = yield from _swiglu_update(x, mod_ref, 6, g_ref, w1_ref, w3_ref, w2_ref, act_ref, rows)
        o_ref[rows, :] = _rms(x, gf_ref[...])

    _lockstep([rows_gen(rows) for rows in _row_slices(x_ref.shape[0])])


def _tail_call(x2d, o0, o1, z2d, at2d, mod3, mod_row, gn, wo, g, w1, w3, w2, gf, tm):
    n, d = x2d.shape
    return pl.pallas_call(
        _tail_kernel,
        out_shape=jax.ShapeDtypeStruct((n, d), F32),
        grid=(n // tm,),
        in_specs=[pl.BlockSpec((tm, d), lambda i: (i, 0)),
                  pl.BlockSpec((tm, DN_WIDTH), lambda i: (i, 0)),
                  pl.BlockSpec((tm, DN_WIDTH), lambda i: (i, 0)),
                  pl.BlockSpec((tm, DN_WIDTH), lambda i: (i, 0)),
                  pl.BlockSpec((tm, AT_WIDTH), lambda i: (i, 0)),
                  pl.BlockSpec((None, N_MOD, d), lambda i: (mod_row(i), 0, 0)),
                  _const_spec((1, LANES)),
                  _const_spec(wo.shape), _const_spec((1, d)),
                  pl.BlockSpec(memory_space=pl.ANY), pl.BlockSpec(memory_space=pl.ANY), pl.BlockSpec(memory_space=pl.ANY),
                  _const_spec((1, d))],
        out_specs=pl.BlockSpec((tm, d), lambda i: (i, 0)),
        scratch_shapes=[pltpu.VMEM((tm, D_FF), BF16), pltpu.VMEM((tm, DN_WIDTH), BF16)] + _ffn_weight_scratch(w1, w2),
        compiler_params=_params(1),
        name="tail",
    )(x2d, o0, o1, z2d, at2d, mod3, gn, wo, g, w1, w3, w2, gf)


def _rope_tables(n):
    pos = np.arange(n)
    freqs = 1.0 / (ROPE_THETA ** (np.arange(0, ROPE_AXIS_DIM, 2, dtype=np.float64) / ROPE_AXIS_DIM))
    ang = np.concatenate([(pos // GRID_W)[:, None] * freqs, (pos % GRID_W)[:, None] * freqs], axis=-1)
    sign = np.tile(np.array([-1.0, 1.0]), AT_HD // 2)
    cos = np.repeat(np.cos(ang), 2, axis=-1)
    sin = np.repeat(np.sin(ang), 2, axis=-1) * sign
    return jnp.asarray(cos, F32), jnp.asarray(sin, F32)


def _head_kind_lanes(b_part, a_part):
    lead = b_part.shape[:-1]
    kinds = jnp.stack([b_part.reshape(*lead, 2, DN_HEADS), a_part.reshape(*lead, 2, DN_HEADS)], axis=-3)
    kinds = jnp.moveaxis(kinds.reshape(*lead, 4, DN_HEADS), -1, -2)
    kinds = jnp.pad(kinds, [(0, 0)] * len(lead) + [(0, 0), (0, DN_BG_ROWS - 4)])
    flat = kinds.reshape(*lead, DN_HEADS * DN_BG_ROWS)
    return jnp.pad(flat, [(0, 0)] * len(lead) + [(0, LANES - DN_HEADS * DN_BG_ROWS)])


def _split_w_in(w_in):
    ba = _head_kind_lanes(w_in[:, OFF_DN_B:OFF_DN_A], w_in[:, OFF_DN_A:OFF_AT_Q])
    return w_in[:, :OFF_DN_B].astype(BF16), w_in[:, OFF_AT_Q:].astype(BF16), ba.astype(BF16)


def kernel(x, c, ctx, c_ctx, w_mod, b_mod, g_ffn1, ffn1_w1, ffn1_w3, ffn1_w2, g_mix, w_in, dn_conv, dn_a_log,
           dn_dt_bias, dn_norm, q_norm, k_norm, w_out, g_ffn2, ffn2_w1, ffn2_w3, ffn2_w2, g_final):
    bsz, t, d = x.shape
    tc = ctx.shape[1]
    assert w_mod.shape[0] == 1, "single-layer block"
    ctx_row = bsz

    cc = jnp.zeros((8, d), F32).at[:bsz].set(c).at[ctx_row].set(c_ctx)
    mod3 = _mod_call(cc, w_mod[0], b_mod[0][None]).reshape(8, N_MOD, d)

    lat_row = lambda tile: (lambda i: i // (t // tile))
    ctx_rowf = lambda i: ctx_row
    row = lambda v: v.reshape(1, -1)

    x1, w1a, w3a, w2a = _ffn_cast_call(x.reshape(bsz * t, d), mod3, lat_row(TM_FFN), row(g_ffn1[0]),
                                       ffn1_w1[0], ffn1_w3[0], ffn1_w2[0], TM_FFN, "ffn1_lat")
    c1 = _ffn_call(ctx.reshape(bsz * tc, d), mod3, ctx_rowf, row(g_ffn1[0]), w1a, w3a, w2a, TM_FFN, "ffn1_ctx")

    w_parts = _split_w_in(w_in[0])
    no_b = jnp.zeros((1, 2 * DN_HEADS), F32)
    alog_row = _head_kind_lanes(no_b, dn_a_log[0].reshape(1, -1))
    dtb_row = _head_kind_lanes(no_b, dn_dt_bias[0].reshape(1, -1))
    cos, sin = _rope_tables(t)
    qn, kn = row(q_norm[0]), row(k_norm[0])
    qkv_l, z_l, bg_l, aq_l, ak_l, av_l = _inproj_call(
        x1, mod3, lat_row(TM_INPROJ), row(g_mix[0]), w_parts, alog_row, dtb_row, qn, kn, cos, sin, bsz, t, TM_INPROJ, True,
        "inproj_lat")
    qkv_c, _, bg_c, _, ak_c, av_c = _inproj_call(
        c1, mod3, ctx_rowf, row(g_mix[0]), w_parts, alog_row, dtb_row, qn, kn, cos[:tc], sin[:tc], bsz, tc, tc, False,
        "inproj_ctx")

    bg_all = jnp.concatenate([bg_c, bg_l], axis=2)
    bg_all = bg_all.reshape(bsz, DN_HEADS, DN_BG_ROWS, (tc + t) // CHUNK, CHUNK).transpose(0, 1, 3, 2, 4)
    conv_w = dn_conv[0].reshape(DN_CONV, 3 * DN_HEADS, LANES).transpose(1, 0, 2)
    dn_wq, dn_ak, dn_u, dn_eg = _dn_prep_call(qkv_l, qkv_c, conv_w, bg_all, _dn_masks())
    o_fwd, o_bwd = _dn_scan_call(dn_wq, dn_ak, dn_u, dn_eg, t)
    at_lat = _attn_call(aq_l, ak_l, av_l, ak_c, av_c, 1024)

    out = _tail_call(x1, o_fwd.reshape(bsz * t, DN_WIDTH), o_bwd.reshape(bsz * t, DN_WIDTH), z_l,
                     at_lat.reshape(bsz * t, AT_WIDTH), mod3, lat_row(TM_TAIL), row(dn_norm[0]), w_out[0].astype(BF16),
                     row(g_ffn2[0]), ffn2_w1[0], ffn2_w3[0], ffn2_w2[0],
                     row(g_final), TM_TAIL)
    return out.reshape(bsz, t, d)
```

```python
import functools

import numpy as np
import jax
import jax.numpy as jnp
from jax import lax
from jax.experimental import pallas as pl
from jax.experimental.pallas import tpu as pltpu

F32 = jnp.float32
BF16 = jnp.bfloat16

D_MODEL = 1024
CTX_LEN = 256
GRID_W = 64
EPS = 1e-6
N_MOD = 9
D_FF = 2816
DN_HEADS = 4
DN_DK = 128
DN_DV = 128
DN_WIDTH = DN_HEADS * DN_DV
DN_CONV = 5
AT_HEADS = 4
AT_KV_HEADS = 2
AT_HD = 128
AT_WIDTH = AT_HEADS * AT_HD
ATT_SCALE = AT_HD ** -0.5
ROPE_AXIS_DIM = AT_HD // 2
ROPE_THETA = 10000.0
LEN_DN_QKV = 3 * DN_WIDTH
OFF_DN_Z = LEN_DN_QKV
OFF_DN_B = OFF_DN_Z + DN_WIDTH
OFF_DN_A = OFF_DN_B + 2 * DN_HEADS
OFF_AT_Q = OFF_DN_A + 2 * DN_HEADS
OFF_AT_K = OFF_AT_Q + AT_WIDTH
OFF_AT_V = OFF_AT_K + AT_KV_HEADS * AT_HD
P_IN = OFF_AT_V + AT_KV_HEADS * AT_HD

LANES = 128
CHUNK = 128
TM_FFN = 512
TM_INPROJ = 512
TM_TAIL = 512
FF_TILE = 256
W_STAGE_CHUNKS = 16
W_STAGE_SLOTS = 4
ROW_SPLIT = 2
DN_GROUP = 17
DN_BG_ROWS = 8
DN_SCAN_CHUNKS = 2
DN_CONV_BLOCKS = 2
AT_KEYS = 1024
LOG2E = 1.4426950408889634
AT_SAFE_LOG2 = 60.0
VMEM_LIMIT = 56 * 1024 * 1024
NEG_BIG = -1e30

_NT = (((1,), (1,)), ((), ()))


def _sigmoid(x):
    return 1.0 / (1.0 + jnp.exp(-x))


def _silu(x):
    return x * _sigmoid(x)


def _rms(x, gain):
    ms = jnp.mean(x * x, axis=-1, keepdims=True)
    return x * lax.rsqrt(ms + EPS) * gain


def _dot(a, b):
    return jnp.dot(a, b, preferred_element_type=F32)


def _const_spec(shape):
    nd = len(shape)
    return pl.BlockSpec(shape, lambda *_: (0,) * nd, pipeline_mode=pl.Buffered(1))


def _params(n_axes):
    return pltpu.CompilerParams(dimension_semantics=("arbitrary",) * n_axes,
                                vmem_limit_bytes=VMEM_LIMIT)


def _mod_kernel(c_ref, w_ref, b_ref, o_ref):
    s = _silu(c_ref[...]).astype(BF16)
    o_ref[...] = _dot(s, w_ref[...].astype(BF16)) + b_ref[...]


def _mod_call(cc, w_mod, b_mod):
    d = cc.shape[1]
    n = w_mod.shape[1]
    return pl.pallas_call(
        _mod_kernel,
        out_shape=jax.ShapeDtypeStruct((cc.shape[0], n), F32),
        grid=(n // d,),
        in_specs=[pl.BlockSpec(cc.shape, lambda j: (0, 0)),
                  pl.BlockSpec((d, d), lambda j: (0, j)),
                  pl.BlockSpec((1, d), lambda j: (0, j))],
        out_specs=pl.BlockSpec((cc.shape[0], d), lambda j: (0, j)),
        compiler_params=_params(1),
        name="mod",
    )(cc, w_mod, b_mod)


def _lockstep(gens):
    results = [None] * len(gens)
    alive = list(range(len(gens)))
    while alive:
        still = []
        for i in alive:
            try:
                next(gens[i])
                still.append(i)
            except StopIteration as done:
                results[i] = done.value
        alive = still
    return results


def _row_slices(tm):
    sub = tm // ROW_SPLIT
    return [slice(r * sub, (r + 1) * sub) for r in range(ROW_SPLIT)]


def _swiglu_update(x, mod_ref, j0, g_ref, w1_ref, w3_ref, w2_ref, act_ref, rows):
    h = (_rms(x, g_ref[...]) * (1.0 + mod_ref[j0 + 1:j0 + 2, :]) + mod_ref[j0:j0 + 1, :]).astype(BF16)
    for c in range(D_FF // FF_TILE):
        sl = slice(c * FF_TILE, (c + 1) * FF_TILE)
        a = _dot(h, w1_ref[:, sl])
        b = _dot(h, w3_ref[:, sl])
        yield
        act_ref[rows, sl] = (_silu(a) * b).astype(BF16)
    y = _dot(act_ref[rows, :], w2_ref[...])
    yield
    return x + (0.5 * mod_ref[j0 + 2:j0 + 3, :]) * y


def _stream_cast(w_hbm, w_vmem, stage_ref, sem):
    slots, rows = stage_ref.shape[:2]
    n = w_hbm.shape[0] // rows

    def chunk_copy(i):
        return pltpu.make_async_copy(w_hbm.at[pl.ds(i * rows, rows), :], stage_ref.at[i % slots], sem.at[i % slots])

    for i in range(min(slots - 1, n)):
        chunk_copy(i).start()
    for i in range(n):
        if i + slots - 1 < n:
            chunk_copy(i + slots - 1).start()
        chunk_copy(i).wait()
        w_vmem[i * rows:(i + 1) * rows, :] = stage_ref[i % slots].astype(BF16)


def _load_ffn_weights(w_hbm, w_vmem, stage_a, stage_b, sem):
    _stream_cast(w_hbm[0], w_vmem[0], stage_a, sem)
    _stream_cast(w_hbm[1], w_vmem[1], stage_a, sem)
    _stream_cast(w_hbm[2], w_vmem[2], stage_b, sem)


def _ffn_kernel(x_ref, c_ref, mod_ref, g_ref, w1_hbm, w3_hbm, w2_hbm, o_ref, oc_ref,
                act_ref, w1_v, w3_v, w2_v, stage_a, stage_b, sem, *, n_lat):
    @pl.when(pl.program_id(0) == 0)
    def _():
        _load_ffn_weights((w1_hbm, w3_hbm, w2_hbm), (w1_v, w3_v, w2_v), stage_a, stage_b, sem)

    is_lat = pl.program_id(0) < n_lat
    slices = _row_slices(x_ref.shape[0])
    outs = _lockstep([_swiglu_update(jnp.where(is_lat, x_ref[rows, :], c_ref[rows, :]), mod_ref, 0, g_ref,
                                     w1_v, w3_v, w2_v, act_ref, rows) for rows in slices])

    @pl.when(is_lat)
    def _():
        for rows, out in zip(slices, outs):
            o_ref[rows, :] = out

    @pl.when(jnp.logical_not(is_lat))
    def _():
        for rows, out in zip(slices, outs):
            oc_ref[rows, :] = out


def _ffn_weight_scratch(w1, w2):
    return [pltpu.VMEM(w1.shape, BF16), pltpu.VMEM(w1.shape, BF16), pltpu.VMEM(w2.shape, BF16),
            pltpu.VMEM((W_STAGE_SLOTS, w1.shape[0] // W_STAGE_CHUNKS, w1.shape[1]), F32),
            pltpu.VMEM((W_STAGE_SLOTS, w2.shape[0] // W_STAGE_CHUNKS, w2.shape[1]), F32),
            pltpu.SemaphoreType.DMA((W_STAGE_SLOTS,))]


def _ffn_call(x2d, c2d, mod3, lat_row, ctx_row, g, w1, w3, w2, tm):
    n, d = x2d.shape
    n_lat, n_ctx = n // tm, c2d.shape[0] // tm
    lat_blk = lambda i: (jnp.minimum(i, n_lat - 1), 0)
    ctx_blk = lambda i: (jnp.maximum(i - n_lat, 0), 0)
    any_spec = pl.BlockSpec(memory_space=pl.ANY)
    return pl.pallas_call(
        functools.partial(_ffn_kernel, n_lat=n_lat),
        out_shape=(jax.ShapeDtypeStruct((n, d), F32), jax.ShapeDtypeStruct(c2d.shape, F32)),
        grid=(n_lat + n_ctx,),
        in_specs=[pl.BlockSpec((tm, d), lat_blk), pl.BlockSpec((tm, d), ctx_blk),
                  pl.BlockSpec((None, N_MOD, d), lambda i: (jnp.where(i < n_lat, lat_row(i), ctx_row), 0, 0)),
                  _const_spec((1, d)), any_spec, any_spec, any_spec],
        out_specs=(pl.BlockSpec((tm, d), lat_blk), pl.BlockSpec((tm, d), ctx_blk)),
        scratch_shapes=[pltpu.VMEM((tm, D_FF), BF16)] + _ffn_weight_scratch(w1, w2),
        compiler_params=_params(1),
        name="ffn1",
    )(x2d, c2d, mod3, g, w1, w3, w2)


def _rope(x, cos, sin_signed, even):
    nxt = pltpu.roll(x, LANES - 1, axis=1)
    prv = pltpu.roll(x, 1, axis=1)
    return x * cos + jnp.where(even, nxt, prv) * sin_signed


def _inproj_kernel(x_ref, mod_ref, g_ref, wdn_ref, wat_ref, wba_ref, alog_ref, dtb_ref, qn_ref, kn_ref, cos_ref, sin_ref,
                   qkv_ref, z_ref, bg_ref, aq_ref, ak_ref, av_ref, *, rope):
    tm = x_ref.shape[0]
    x = x_ref[...]
    h = (_rms(x, g_ref[...]) * (1.0 + mod_ref[4:5, :]) + mod_ref[3:4, :]).astype(BF16)

    p_ba = _dot(h, wba_ref[...])
    p_q = _dot(h, wat_ref[:, :AT_WIDTH])

    beta = _sigmoid(p_ba)
    t = p_ba + dtb_ref[...]
    softplus = jnp.maximum(t, 0.0) + jnp.log(1.0 + jnp.exp(-jnp.abs(t)))
    g = -jnp.exp(alog_ref[...]) * softplus
    ri = lax.broadcasted_iota(jnp.int32, (CHUNK, CHUNK), 0)
    ci = lax.broadcasted_iota(jnp.int32, (CHUNK, CHUNK), 1)
    ltri = jnp.where(ri >= ci, 1.0, 0.0).astype(BF16)
    kind = lax.broadcasted_iota(jnp.int32, (CHUNK, LANES), 1) % DN_BG_ROWS
    for c in range(tm // CHUNK):
        rows = slice(c * CHUNK, (c + 1) * CHUNK)
        gc = g[rows]
        g1 = gc.astype(BF16)
        r1 = gc - g1.astype(F32)
        g2 = r1.astype(BF16)
        g3 = (r1 - g2.astype(F32)).astype(BF16)
        pre = _dot(ltri, g1) + _dot(ltri, g2) + _dot(ltri, g3)
        suf = pre[CHUNK - 1:CHUNK, :] - pre + gc
        tile = jnp.where(kind < 2, beta[rows], jnp.where(kind == 2, pre, suf))
        bg_ref[:, rows] = tile.T[:DN_HEADS * DN_BG_ROWS]

    p_kv = _dot(h, wat_ref[:, AT_WIDTH:])
    lane = lax.broadcasted_iota(jnp.int32, (tm, LANES), 1)
    even = (lane % 2) == 0
    if rope:
        cos = cos_ref[...]
        sin = sin_ref[...]
    for hd in range(AT_HEADS):
        q = _rms(p_q[:, hd * LANES:(hd + 1) * LANES], qn_ref[...])
        if rope:
            q = _rope(q, cos, sin, even)
        aq_ref[hd // 2, :, (hd % 2) * LANES:(hd % 2 + 1) * LANES] = (q * (ATT_SCALE * LOG2E)).astype(BF16)
    for hd in range(AT_KV_HEADS):
        k = _rms(p_kv[:, hd * LANES:(hd + 1) * LANES], kn_ref[...])
        if rope:
            k = _rope(k, cos, sin, even)
        ak_ref[hd] = k.astype(BF16)
        av_ref[hd] = p_kv[:, (AT_KV_HEADS + hd) * LANES:(AT_KV_HEADS + hd + 1) * LANES].T.astype(BF16)

    z_ref[...] = _dot(h, wdn_ref[:, LEN_DN_QKV:]).astype(BF16)
    for part in range(3):
        p = _dot(h, wdn_ref[:, part * DN_WIDTH:(part + 1) * DN_WIDTH])
        for hd in range(DN_HEADS):
            qkv_ref[part * DN_HEADS + hd] = p[:, hd * LANES:(hd + 1) * LANES].astype(BF16)


def _inproj_call(x2d, mod3, mod_row, g, w_parts, alog_row, dtb_row, qn, kn, cos, sin, bsz, t, tm, rope, name):
    n, d = x2d.shape
    tpb = t // tm
    bt = lambda i: (i // tpb, 0, i % tpb, 0)
    out_shape = (
        jax.ShapeDtypeStruct((bsz, 3 * DN_HEADS, t, LANES), BF16),
        jax.ShapeDtypeStruct((n, DN_WIDTH), BF16),
        jax.ShapeDtypeStruct((bsz, DN_HEADS * DN_BG_ROWS, t), F32),
        jax.ShapeDtypeStruct((bsz, AT_KV_HEADS, t, 2 * AT_HD), BF16),
        jax.ShapeDtypeStruct((bsz, AT_KV_HEADS, t, AT_HD), BF16),
        jax.ShapeDtypeStruct((bsz, AT_KV_HEADS, AT_HD, t), BF16),
    )
    out_specs = (
        pl.BlockSpec((None, 3 * DN_HEADS, tm, LANES), bt),
        pl.BlockSpec((tm, DN_WIDTH), lambda i: (i, 0)),
        pl.BlockSpec((None, DN_HEADS * DN_BG_ROWS, tm), lambda i: (i // tpb, 0, i % tpb)),
        pl.BlockSpec((None, AT_KV_HEADS, tm, 2 * AT_HD), bt),
        pl.BlockSpec((None, AT_KV_HEADS, tm, AT_HD), bt),
        pl.BlockSpec((None, AT_KV_HEADS, AT_HD, tm), lambda i: (i // tpb, 0, 0, i % tpb)),
    )
    return pl.pallas_call(
        functools.partial(_inproj_kernel, rope=rope),
        out_shape=out_shape,
        grid=(n // tm,),
        in_specs=[pl.BlockSpec((tm, d), lambda i: (i, 0)),
                  pl.BlockSpec((None, N_MOD, d), lambda i: (mod_row(i), 0, 0)),
                  _const_spec((1, d)),
                  *[_const_spec(w.shape) for w in w_parts],
                  _const_spec((1, LANES)), _const_spec((1, LANES)),
                  _const_spec((1, LANES)), _const_spec((1, LANES)),
                  pl.BlockSpec((tm, LANES), lambda i: (i % tpb, 0)),
                  pl.BlockSpec((tm, LANES), lambda i: (i % tpb, 0))],
        out_specs=out_specs,
        compiler_params=_params(1),
        name=name,
    )(x2d, mod3, g, *w_parts, alog_row, dtb_row, qn, kn, cos, sin)


def _dn_conv_all(src_refs, w_refs, xpad_ref, t, emits):
    nblk = t // CHUNK
    n = len(src_refs)
    for i in range(n):
        xpad_ref[i, 0:8, :] = jnp.zeros((8, LANES), F32)
        xpad_ref[i, 8 + t:16 + t, :] = jnp.zeros((8, LANES), F32)

    def fill(r, carry):
        r0 = pl.multiple_of(r * CHUNK, CHUNK)
        for i in range(n):
            xpad_ref[i, pl.ds(r0 + 8, CHUNK), :] = src_refs[i][pl.ds(r0, CHUNK), :].astype(F32)
        return carry

    lax.fori_loop(0, nblk, fill, 0)
    ws = [w_ref[...] for w_ref in w_refs]
    pad = (DN_CONV - 1) // 2

    def conv(r, carry):
        for u in range(DN_CONV_BLOCKS):
            r0 = pl.multiple_of((r * DN_CONV_BLOCKS + u) * CHUNK, CHUNK)
            for i in range(n):
                acc = None
                for j in range(DN_CONV):
                    tap = xpad_ref[i, pl.ds(r0 + (8 - pad + j), CHUNK), :] * ws[i][j:j + 1, :]
                    acc = tap if acc is None else acc + tap
                emits[i](r0, _silu(acc))
        return carry

    lax.fori_loop(0, nblk // DN_CONV_BLOCKS, conv, 0)


def _block_diag(cat):
    c = cat.shape[0]
    zero = jnp.zeros((c, c), cat.dtype)
    return jnp.concatenate([jnp.concatenate([cat[:, :c], zero], axis=1),
                            jnp.concatenate([zero, cat[:, c:]], axis=1)], axis=0)


def _dn_prep_chunk(a, qs_ref, ks_ref, vs_ref, bg_ref, lvl_ref, wq_ref, ak_ref, u_ref, eg_ref):
    row = pl.multiple_of(a * CHUNK, CHUNK)
    k = ks_ref[pl.ds(row, CHUNK), :]
    q = qs_ref[pl.ds(row, CHUNK), :]
    v = vs_ref[pl.ds(row, CHUNK), :]
    bg = bg_ref[a]
    kf = k.astype(F32)
    ri = lax.broadcasted_iota(jnp.int32, (CHUNK, CHUNK), 0)
    ci = lax.broadcasted_iota(jnp.int32, (CHUNK, CHUNK), 1)
    per_dir = []
    for d in range(2):
        beta = jnp.broadcast_to(bg[d:d + 1, :], (CHUNK, CHUNK)).T
        g_r = jnp.broadcast_to(bg[2 + d:3 + d, :], (CHUNK, CHUNK))
        g_c = g_r.T
        incl = (ri >= ci) if d == 0 else (ri <= ci)
        decay = jnp.exp(jnp.where(incl, g_c - g_r, NEG_BIG))
        per_dir.append((beta, g_c, decay, kf * beta))
    aq = lax.dot_general(jnp.concatenate([per_dir[0][3].astype(BF16), per_dir[1][3].astype(BF16), q], axis=0),
                         k, _NT, preferred_element_type=F32)
    yield
    qk = aq[2 * CHUNK:]
    a_cat = jnp.concatenate([jnp.where(ri > ci, aq[:CHUNK] * per_dir[0][2], 0.0),
                             jnp.where(ri < ci, aq[CHUNK:2 * CHUNK] * per_dir[1][2], 0.0)], axis=1)

    rhs = []
    for d in range(2):
        beta, g_c, decay, kb = per_dir[d]
        eg = jnp.exp(g_c)
        g_last = g_c[CHUNK - 1:CHUNK, :] if d == 0 else g_c[0:1, :]
        k_tail = kf * jnp.exp(g_last - g_c)
        ak_ref[d, a] = jnp.concatenate([(qk * decay).astype(BF16), k_tail.T.astype(BF16)], axis=0)
        wq_ref[d, a, CHUNK:, :] = (q.astype(F32) * eg).astype(BF16)
        eg_ref[d, a] = jnp.broadcast_to(jnp.exp(g_last), (8, LANES))
        rhs.append(jnp.concatenate([v * beta, kb * eg], axis=1))

    eye = jnp.where(ri == ci, 1.0, 0.0)
    eye_cat = jnp.concatenate([eye, eye], axis=1)
    pair = jnp.where((ri >> 1) == (ci >> 1), 1.0, 0.0)
    t_cat = (eye_cat - a_cat * jnp.concatenate([pair, pair], axis=1)).astype(BF16)
    a_bf = a_cat.astype(BF16)
    for lvl in range(lvl_ref.shape[0]):
        x = _dot(t_cat, _block_diag(a_bf * lvl_ref[lvl]))
        yield
        t_cat = t_cat - _dot(x.astype(BF16), _block_diag(t_cat)).astype(BF16)
        yield
    toff = t_cat - eye_cat.astype(BF16)

    for d in range(2):
        uw = rhs[d] + _dot(toff[:, d * CHUNK:(d + 1) * CHUNK], rhs[d].astype(BF16))
        u_ref[d, a] = uw[:, :DN_DV].astype(BF16)
        wq_ref[d, a, :CHUNK, :] = uw[:, DN_DV:].astype(BF16)


def _dn_prep_kernel(ql_ref, kl_ref, vl_ref, qc_ref, kc_ref, vc_ref, wq_w_ref, wk_w_ref, wv_w_ref, bg_ref,
                    lvl_ref, wq_ref, ak_ref, u_ref, eg_ref, qs_ref, ks_ref, vs_ref, xpad_ref):
    t = ql_ref.shape[0]
    tc = qc_ref.shape[0]

    def emit_qk(dst_ref, base, scale):
        def emit(r0, y):
            yn = y * lax.rsqrt(jnp.sum(y * y, axis=-1, keepdims=True) + EPS)
            dst_ref[pl.ds(base + r0, CHUNK), :] = (yn * scale).astype(BF16)
        return emit

    def emit_v(base):
        def emit(r0, y):
            vs_ref[pl.ds(base + r0, CHUNK), :] = y
        return emit

    w_refs = (wq_w_ref, wk_w_ref, wv_w_ref)
    for srcs, tt, base in (((qc_ref, kc_ref, vc_ref), tc, 0), ((ql_ref, kl_ref, vl_ref), t, tc)):
        _dn_conv_all(srcs, w_refs, xpad_ref, tt,
                     (emit_qk(qs_ref, base, DN_DK ** -0.5), emit_qk(ks_ref, base, 1.0), emit_v(base)))

    refs = (qs_ref, ks_ref, vs_ref, bg_ref, lvl_ref, wq_ref, ak_ref, u_ref, eg_ref)

    def group(gi, carry):
        _lockstep([_dn_prep_chunk(gi * DN_GROUP + j, *refs) for j in range(DN_GROUP)])
        return carry

    lax.fori_loop(0, (tc + t) // (CHUNK * DN_GROUP), group, 0)


def _dn_masks():
    i = np.arange(CHUNK)
    lvls = []
    b = 2
    while b < CHUNK:
        rb, cb = (i // b)[:, None], (i // b)[None, :]
        fwd = ((rb == cb + 1) & (rb % 2 == 1)).astype(np.float32)
        lvls.append(np.concatenate([fwd, fwd.T], axis=1))
        b *= 2
    return jnp.asarray(np.stack(lvls), dtype=BF16)


def _dn_prep_call(qkv_l, qkv_c, conv_w, bg_all, lvl_masks):
    bsz, _, t, _ = qkv_l.shape
    tc = qkv_c.shape[2]
    ta = t + tc
    nc = ta // CHUNK

    def head(part, tt):
        return pl.BlockSpec((None, None, tt, LANES), lambda b, h: (b, part * DN_HEADS + h, 0, 0))

    def wspec(part):
        return pl.BlockSpec((None, DN_CONV, LANES), lambda b, h: (part * DN_HEADS + h, 0, 0))

    def out(rows, dt):
        return (jax.ShapeDtypeStruct((bsz, DN_HEADS, 2, nc, rows, LANES), dt),
                pl.BlockSpec((None, None, 2, nc, rows, LANES), lambda b, h: (b, h, 0, 0, 0, 0)))

    outs = [out(2 * CHUNK, BF16), out(2 * CHUNK, BF16), out(CHUNK, BF16), out(8, F32)]
    return pl.pallas_call(
        _dn_prep_kernel,
        out_shape=tuple(o[0] for o in outs),
        grid=(bsz, DN_HEADS),
        in_specs=[head(0, t), head(1, t), head(2, t), head(0, tc), head(1, tc), head(2, tc),
                  wspec(0), wspec(1), wspec(2),
                  pl.BlockSpec((None, None, nc, DN_BG_ROWS, LANES), lambda b, h: (b, h, 0, 0, 0)),
                  _const_spec(lvl_masks.shape)],
        out_specs=tuple(o[1] for o in outs),
        scratch_shapes=[pltpu.VMEM((ta, LANES), BF16), pltpu.VMEM((ta, LANES), BF16),
                        pltpu.VMEM((ta, LANES), F32), pltpu.VMEM((3, t + 16, LANES), F32)],
        compiler_params=_params(2),
        name="dn_prep",
    )(qkv_l, qkv_l, qkv_l, qkv_c, qkv_c, qkv_c, conv_w, conv_w, conv_w, bg_all, lvl_masks)


def _dn_scan_kernel(wq0_ref, wq1_ref, ak0_ref, ak1_ref, u0_ref, u1_ref, eg0_ref, eg1_ref, o0_ref, o1_ref, s_ref):
    @pl.when(pl.program_id(1) == 0)
    def _():
        s_ref[...] = jnp.zeros(s_ref.shape, F32)

    heads = range(DN_HEADS)
    s = [s_ref[h] for h in heads]
    for j in range(DN_SCAN_CHUNKS):
        jb = DN_SCAN_CHUNKS - 1 - j
        r1 = [_dot(jnp.concatenate([wq0_ref[h, j], wq1_ref[h, jb]], axis=1), _block_diag(s[h].astype(BF16)))
              for h in heads]
        v_new = [jnp.concatenate([u0_ref[h, j], u1_ref[h, jb]], axis=1).astype(F32) - r1[h][:CHUNK] for h in heads]
        r2 = [_dot(jnp.concatenate([ak0_ref[h, j], ak1_ref[h, jb]], axis=1), _block_diag(v_new[h].astype(BF16)))
              for h in heads]
        for h in heads:
            o = r1[h][CHUNK:] + r2[h][:CHUNK]
            o0_ref[j * CHUNK:(j + 1) * CHUNK, h * DN_DV:(h + 1) * DN_DV] = o[:, :DN_DV].astype(BF16)
            o1_ref[jb * CHUNK:(jb + 1) * CHUNK, h * DN_DV:(h + 1) * DN_DV] = o[:, DN_DV:].astype(BF16)
            dec = jnp.concatenate([eg0_ref[h, j, 0:1, :], eg1_ref[h, jb, 0:1, :]], axis=1)
            s[h] = s[h] * dec + r2[h][CHUNK:]
    for h in heads:
        s_ref[h] = s[h]


def _dn_scan_call(wq, ak, u, eg, t):
    bsz, nh, _, nc, _, _ = wq.shape
    k = DN_SCAN_CHUNKS
    nb = nc // k
    nb_ctx = (nc - t // CHUNK) // k

    def bwd_block(i):
        return jnp.where(i < nb_ctx, nb_ctx - 1 - i, nb - 1 + nb_ctx - i)

    def spec(rows, d):
        if d == 0:
            return pl.BlockSpec((None, nh, None, k, rows, LANES), lambda b, i: (b, 0, 0, i, 0, 0))
        return pl.BlockSpec((None, nh, None, k, rows, LANES), lambda b, i: (b, 0, 1, bwd_block(i), 0, 0))

    o_shape = jax.ShapeDtypeStruct((bsz, t, nh * DN_DV), BF16)
    o0_spec = pl.BlockSpec((None, k * CHUNK, nh * DN_DV), lambda b, i: (b, jnp.maximum(i - nb_ctx, 0), 0))
    o1_spec = pl.BlockSpec((None, k * CHUNK, nh * DN_DV),
                           lambda b, i: (b, bwd_block(jnp.maximum(i, nb_ctx)) - nb_ctx, 0))
    return pl.pallas_call(
        _dn_scan_kernel,
        out_shape=(o_shape, o_shape),
        grid=(bsz, nb),
        in_specs=[spec(2 * CHUNK, 0), spec(2 * CHUNK, 1), spec(2 * CHUNK, 0), spec(2 * CHUNK, 1),
                  spec(CHUNK, 0), spec(CHUNK, 1), spec(8, 0), spec(8, 1)],
        out_specs=(o0_spec, o1_spec),
        scratch_shapes=[pltpu.VMEM((nh, DN_DK, 2 * DN_DV), F32)],
        compiler_params=_params(2),
        name="dn_scan",
    )(wq, wq, ak, ak, u, u, eg, eg)


def _attn_kernel(q_ref, kl_ref, vtl_ref, kc_ref, vtc_ref, o_ref, kmax_ref, shift_ref):
    t = kl_ref.shape[0]
    blocks = [(kl_ref, vtl_ref, j * AT_KEYS, AT_KEYS) for j in range(t // AT_KEYS)]
    blocks.append((kc_ref, vtc_ref, 0, kc_ref.shape[0]))

    @pl.when(pl.program_id(2) == 0)
    def _():
        best = jnp.zeros((1, 1), F32)
        for k_ref, _, off, size in blocks:
            kk = k_ref[off:off + size, :].astype(F32)
            best = jnp.maximum(best, jnp.max(jnp.sum(kk * kk, axis=-1, keepdims=True), axis=0, keepdims=True))
        kmax_ref[...] = jnp.broadcast_to(best, kmax_ref.shape)

    tq = q_ref.shape[0]
    n = 2 * tq
    q = q_ref[...].astype(F32)
    qt32 = jnp.concatenate([q[:, :AT_HD].T, q[:, AT_HD:].T], axis=1)
    qt = qt32.astype(BF16)
    bound = jnp.sqrt(jnp.sum(qt32 * qt32, axis=0, keepdims=True) * kmax_ref[0:1, 0:1])
    shift_ref[...] = bound

    def scores(blk):
        k_ref, _, off, size = blk
        return _dot(k_ref[off:off + size, :], qt)

    @pl.when(jnp.max(bound) > AT_SAFE_LOG2)
    def _():
        m = jnp.full((1, n), NEG_BIG, F32)
        for blk in blocks:
            m = jnp.maximum(m, jnp.max(scores(blk), axis=0, keepdims=True))
        shift_ref[...] = m

    shift = shift_ref[...]
    den = jnp.zeros((1, n), F32)
    acc = jnp.zeros((AT_HD, n), F32)
    s_next = scores(blocks[0])
    for j, (_, vt_ref, off, size) in enumerate(blocks):
        s = s_next
        if j + 1 < len(blocks):
            s_next = scores(blocks[j + 1])
        p = jnp.exp2(s - shift)
        den = den + jnp.sum(p, axis=0, keepdims=True)
        acc = acc + _dot(vt_ref[:, off:off + size], p.astype(BF16))
    o = (acc / den).T
    o_ref[:, :AT_HD] = o[:tq].astype(BF16)
    o_ref[:, AT_HD:] = o[tq:].astype(BF16)


def _attn_call(aq, ak_l, avt_l, ak_c, avt_c, tq):
    bsz, hkv, t, _ = aq.shape
    tc = ak_c.shape[2]
    kspec = lambda tt: pl.BlockSpec((None, None, tt, AT_HD), lambda b, j, i: (b, j, 0, 0))
    vspec = lambda tt: pl.BlockSpec((None, None, AT_HD, tt), lambda b, j, i: (b, j, 0, 0))
    return pl.pallas_call(
        _attn_kernel,
        out_shape=jax.ShapeDtypeStruct((bsz, t, AT_WIDTH), BF16),
        grid=(bsz, hkv, t // tq),
        in_specs=[pl.BlockSpec((None, None, tq, 2 * AT_HD), lambda b, j, i: (b, j, i, 0)),
                  kspec(t), vspec(t), kspec(tc), vspec(tc)],
        out_specs=pl.BlockSpec((None, tq, 2 * AT_HD), lambda b, j, i: (b, i, j)),
        scratch_shapes=[pltpu.VMEM((8, LANES), F32), pltpu.VMEM((1, 2 * tq), F32)],
        compiler_params=_params(3),
        name="attn",
    )(aq, ak_l, avt_l, ak_c, avt_c)


def _tail_kernel(x_ref, o0_ref, o1_ref, z_ref, at_ref, mod_ref, gn_ref, wo_ref, g_ref, w1_hbm, w3_hbm, w2_hbm,
                 gf_ref, o_ref, act_ref, dn_ref, w1_ref, w3_ref, w2_ref, stage_a, stage_b, sem):
    @pl.when(pl.program_id(0) == 0)
    def _():
        _load_ffn_weights((w1_hbm, w3_hbm, w2_hbm), (w1_ref, w3_ref, w2_ref), stage_a, stage_b, sem)

    def rows_gen(rows):
        for hd in range(DN_HEADS):
            sl = slice(hd * DN_DV, (hd + 1) * DN_DV)
            o = o0_ref[rows, sl].astype(F32) + o1_ref[rows, sl].astype(F32)
            dn_ref[rows, sl] = (_rms(o, gn_ref[...]) * _silu(z_ref[rows, sl].astype(F32))).astype(BF16)
        mix = _dot(dn_ref[rows, :], wo_ref[:DN_WIDTH, :]) + _dot(at_ref[rows, :], wo_ref[DN_WIDTH:, :])
        yield
        x = x_ref[rows, :] + mod_ref[5:6, :] * mix
        x = yield from _swiglu_update(x, mod_ref, 6, g_ref, w1_ref, w3_ref, w2_ref, act_ref, rows)
        o_ref[rows, :] = _rms(x, gf_ref[...])

    _lockstep([rows_gen(rows) for rows in _row_slices(x_ref.shape[0])])


def _tail_call(x2d, o0, o1, z2d, at2d, mod3, mod_row, gn, wo, g, w1, w3, w2, gf, tm):
    n, d = x2d.shape
    return pl.pallas_call(
        _tail_kernel,
        out_shape=jax.ShapeDtypeStruct((n, d), F32),
        grid=(n // tm,),
        in_specs=[pl.BlockSpec((tm, d), lambda i: (i, 0)),
                  pl.BlockSpec((tm, DN_WIDTH), lambda i: (i, 0)),
                  pl.BlockSpec((tm, DN_WIDTH), lambda i: (i, 0)),
                  pl.BlockSpec((tm, DN_WIDTH), lambda i: (i, 0)),
                  pl.BlockSpec((tm, AT_WIDTH), lambda i: (i, 0)),
                  pl.BlockSpec((None, N_MOD, d), lambda i: (mod_row(i), 0, 0)),
                  _const_spec((1, LANES)),
                  _const_spec(wo.shape), _const_spec((1, d)),
                  pl.BlockSpec(memory_space=pl.ANY), pl.BlockSpec(memory_space=pl.ANY), pl.BlockSpec(memory_space=pl.ANY),
                  _const_spec((1, d))],
        out_specs=pl.BlockSpec((tm, d), lambda i: (i, 0)),
        scratch_shapes=[pltpu.VMEM((tm, D_FF), BF16), pltpu.VMEM((tm, DN_WIDTH), BF16)] + _ffn_weight_scratch(w1, w2),
        compiler_params=_params(1),
        name="tail",
    )(x2d, o0, o1, z2d, at2d, mod3, gn, wo, g, w1, w3, w2, gf)


def _rope_tables(n):
    pos = np.arange(n)
    freqs = 1.0 / (ROPE_THETA ** (np.arange(0, ROPE_AXIS_DIM, 2, dtype=np.float64) / ROPE_AXIS_DIM))
    ang = np.concatenate([(pos // GRID_W)[:, None] * freqs, (pos % GRID_W)[:, None] * freqs], axis=-1)
    sign = np.tile(np.array([-1.0, 1.0]), AT_HD // 2)
    cos = np.repeat(np.cos(ang), 2, axis=-1)
    sin = np.repeat(np.sin(ang), 2, axis=-1) * sign
    return jnp.asarray(cos, F32), jnp.asarray(sin, F32)


def _head_kind_lanes(b_part, a_part):
    lead = b_part.shape[:-1]
    kinds = jnp.stack([b_part.reshape(*lead, 2, DN_HEADS), a_part.reshape(*lead, 2, DN_HEADS)], axis=-3)
    kinds = jnp.moveaxis(kinds.reshape(*lead, 4, DN_HEADS), -1, -2)
    kinds = jnp.pad(kinds, [(0, 0)] * len(lead) + [(0, 0), (0, DN_BG_ROWS - 4)])
    flat = kinds.reshape(*lead, DN_HEADS * DN_BG_ROWS)
    return jnp.pad(flat, [(0, 0)] * len(lead) + [(0, LANES - DN_HEADS * DN_BG_ROWS)])


def _split_w_in(w_in):
    ba = _head_kind_lanes(w_in[:, OFF_DN_B:OFF_DN_A], w_in[:, OFF_DN_A:OFF_AT_Q])
    return w_in[:, :OFF_DN_B].astype(BF16), w_in[:, OFF_AT_Q:].astype(BF16), ba.astype(BF16)


def kernel(x, c, ctx, c_ctx, w_mod, b_mod, g_ffn1, ffn1_w1, ffn1_w3, ffn1_w2, g_mix, w_in, dn_conv, dn_a_log,
           dn_dt_bias, dn_norm, q_norm, k_norm, w_out, g_ffn2, ffn2_w1, ffn2_w3, ffn2_w2, g_final):
    bsz, t, d = x.shape
    tc = ctx.shape[1]
    assert w_mod.shape[0] == 1, "single-layer block"
    ctx_row = bsz

    cc = jnp.zeros((8, d), F32).at[:bsz].set(c).at[ctx_row].set(c_ctx)
    mod3 = _mod_call(cc, w_mod[0], b_mod[0][None]).reshape(8, N_MOD, d)

    lat_row = lambda tile: (lambda i: i // (t // tile))
    ctx_rowf = lambda i: ctx_row
    row = lambda v: v.reshape(1, -1)

    x1, c1 = _ffn_call(x.reshape(bsz * t, d), ctx.reshape(bsz * tc, d), mod3, lat_row(TM_FFN), ctx_row, row(g_ffn1[0]),
                       ffn1_w1[0], ffn1_w3[0], ffn1_w2[0], TM_FFN)

    w_parts = _split_w_in(w_in[0])
    no_b = jnp.zeros((1, 2 * DN_HEADS), F32)
    alog_row = _head_kind_lanes(no_b, dn_a_log[0].reshape(1, -1))
    dtb_row = _head_kind_lanes(no_b, dn_dt_bias[0].reshape(1, -1))
    cos, sin = _rope_tables(t)
    qn, kn = row(q_norm[0]), row(k_norm[0])
    qkv_l, z_l, bg_l, aq_l, ak_l, av_l = _inproj_call(
        x1, mod3, lat_row(TM_INPROJ), row(g_mix[0]), w_parts, alog_row, dtb_row, qn, kn, cos, sin, bsz, t, TM_INPROJ, True,
        "inproj_lat")
    qkv_c, _, bg_c, _, ak_c, av_c = _inproj_call(
        c1, mod3, ctx_rowf, row(g_mix[0]), w_parts, alog_row, dtb_row, qn, kn, cos[:tc], sin[:tc], bsz, tc, tc, False,
        "inproj_ctx")

    bg_all = jnp.concatenate([bg_c, bg_l], axis=2)
    bg_all = bg_all.reshape(bsz, DN_HEADS, DN_BG_ROWS, (tc + t) // CHUNK, CHUNK).transpose(0, 1, 3, 2, 4)
    conv_w = dn_conv[0].reshape(DN_CONV, 3 * DN_HEADS, LANES).transpose(1, 0, 2)
    dn_wq, dn_ak, dn_u, dn_eg = _dn_prep_call(qkv_l, qkv_c, conv_w, bg_all, _dn_masks())
    o_fwd, o_bwd = _dn_scan_call(dn_wq, dn_ak, dn_u, dn_eg, t)
    at_lat = _attn_call(aq_l, ak_l, av_l, ak_c, av_c, 1024)

    out = _tail_call(x1, o_fwd.reshape(bsz * t, DN_WIDTH), o_bwd.reshape(bsz * t, DN_WIDTH), z_l,
                     at_lat.reshape(bsz * t, AT_WIDTH), mod3, lat_row(TM_TAIL), row(dn_norm[0]), w_out[0].astype(BF16),
                     row(g_ffn2[0]), ffn2_w1[0], ffn2_w3[0], ffn2_w2[0],
                     row(g_final), TM_TAIL)
    return out.reshape(bsz, t, d)
```

```python
import functools

import numpy as np
import jax
import jax.numpy as jnp
from jax import lax
from jax.experimental import pallas as pl
from jax.experimental.pallas import tpu as pltpu

F32 = jnp.float32
BF16 = jnp.bfloat16

D_MODEL = 1024
CTX_LEN = 256
GRID_W = 64
EPS = 1e-6
N_MOD = 9
D_FF = 2816
DN_HEADS = 4
DN_DK = 128
DN_DV = 128
DN_WIDTH = DN_HEADS * DN_DV
DN_CONV = 5
AT_HEADS = 4
AT_KV_HEADS = 2
AT_HD = 128
AT_WIDTH = AT_HEADS * AT_HD
ATT_SCALE = AT_HD ** -0.5
ROPE_AXIS_DIM = AT_HD // 2
ROPE_THETA = 10000.0
LEN_DN_QKV = 3 * DN_WIDTH
OFF_DN_Z = LEN_DN_QKV
OFF_DN_B = OFF_DN_Z + DN_WIDTH
OFF_DN_A = OFF_DN_B + 2 * DN_HEADS
OFF_AT_Q = OFF_DN_A + 2 * DN_HEADS
OFF_AT_K = OFF_AT_Q + AT_WIDTH
OFF_AT_V = OFF_AT_K + AT_KV_HEADS * AT_HD
P_IN = OFF_AT_V + AT_KV_HEADS * AT_HD

LANES = 128
CHUNK = 128
TM_FFN = 512
TM_INPROJ = 512
TM_TAIL = 512
FF_TILE = 256
W_STAGE_CHUNKS = 16
W_STAGE_SLOTS = 4
ROW_SPLIT = 2
DN_GROUP = 17
DN_BG_ROWS = 8
DN_SCAN_CHUNKS = 2
DN_CONV_BLOCKS = 2
AT_KEYS = 1024
LOG2E = 1.4426950408889634
AT_SAFE_LOG2 = 60.0
VMEM_LIMIT = 56 * 1024 * 1024
NEG_BIG = -1e30

_NT = (((1,), (1,)), ((), ()))


def _sigmoid(x):
    return 1.0 / (1.0 + jnp.exp(-x))


def _silu(x):
    return x * _sigmoid(x)


def _rms(x, gain):
    ms = jnp.mean(x * x, axis=-1, keepdims=True)
    return x * lax.rsqrt(ms + EPS) * gain


def _dot(a, b):
    return jnp.dot(a, b, preferred_element_type=F32)


def _const_spec(shape):
    nd = len(shape)
    return pl.BlockSpec(shape, lambda *_: (0,) * nd, pipeline_mode=pl.Buffered(1))


def _params(n_axes):
    return pltpu.CompilerParams(dimension_semantics=("arbitrary",) * n_axes,
                                vmem_limit_bytes=VMEM_LIMIT)


def _mod_kernel(c_ref, w_ref, b_ref, o_ref):
    s = _silu(c_ref[...]).astype(BF16)
    o_ref[...] = _dot(s, w_ref[...].astype(BF16)) + b_ref[...]


def _mod_call(cc, w_mod, b_mod):
    d = cc.shape[1]
    n = w_mod.shape[1]
    return pl.pallas_call(
        _mod_kernel,
        out_shape=jax.ShapeDtypeStruct((cc.shape[0], n), F32),
        grid=(n // d,),
        in_specs=[pl.BlockSpec(cc.shape, lambda j: (0, 0)),
                  pl.BlockSpec((d, d), lambda j: (0, j)),
                  pl.BlockSpec((1, d), lambda j: (0, j))],
        out_specs=pl.BlockSpec((cc.shape[0], d), lambda j: (0, j)),
        compiler_params=_params(1),
        name="mod",
    )(cc, w_mod, b_mod)


def _lockstep(gens):
    results = [None] * len(gens)
    alive = list(range(len(gens)))
    while alive:
        still = []
        for i in alive:
            try:
                next(gens[i])
                still.append(i)
            except StopIteration as done:
                results[i] = done.value
        alive = still
    return results


def _row_slices(tm):
    sub = tm // ROW_SPLIT
    return [slice(r * sub, (r + 1) * sub) for r in range(ROW_SPLIT)]


def _swiglu_update(x, mod_ref, j0, g_ref, w1_ref, w3_ref, w2_ref, act_ref, rows):
    h = (_rms(x, g_ref[...]) * (1.0 + mod_ref[j0 + 1:j0 + 2, :]) + mod_ref[j0:j0 + 1, :]).astype(BF16)
    for c in range(D_FF // FF_TILE):
        sl = slice(c * FF_TILE, (c + 1) * FF_TILE)
        a = _dot(h, w1_ref[:, sl])
        b = _dot(h, w3_ref[:, sl])
        yield
        act_ref[rows, sl] = (_silu(a) * b).astype(BF16)
    y = _dot(act_ref[rows, :], w2_ref[...])
    yield
    return x + (0.5 * mod_ref[j0 + 2:j0 + 3, :]) * y


def _stream_cast(w_hbm, w_vmem, stage_ref, sem):
    slots, rows = stage_ref.shape[:2]
    n = w_hbm.shape[0] // rows

    def chunk_copy(i):
        return pltpu.make_async_copy(w_hbm.at[pl.ds(i * rows, rows), :], stage_ref.at[i % slots], sem.at[i % slots])

    for i in range(min(slots - 1, n)):
        chunk_copy(i).start()
    for i in range(n):
        if i + slots - 1 < n:
            chunk_copy(i + slots - 1).start()
        chunk_copy(i).wait()
        w_vmem[i * rows:(i + 1) * rows, :] = stage_ref[i % slots].astype(BF16)


def _load_ffn_weights(w_hbm, w_vmem, stage_a, stage_b, sem):
    _stream_cast(w_hbm[0], w_vmem[0], stage_a, sem)
    _stream_cast(w_hbm[1], w_vmem[1], stage_a, sem)
    _stream_cast(w_hbm[2], w_vmem[2], stage_b, sem)


def _ffn_kernel(x_ref, c_ref, mod_ref, g_ref, w1_hbm, w3_hbm, w2_hbm, o_ref, oc_ref,
                act_ref, w1_v, w3_v, w2_v, stage_a, stage_b, sem, *, n_lat):
    @pl.when(pl.program_id(0) == 0)
    def _():
        _load_ffn_weights((w1_hbm, w3_hbm, w2_hbm), (w1_v, w3_v, w2_v), stage_a, stage_b, sem)

    is_lat = pl.program_id(0) < n_lat
    slices = _row_slices(x_ref.shape[0])
    outs = _lockstep([_swiglu_update(jnp.where(is_lat, x_ref[rows, :], c_ref[rows, :]), mod_ref, 0, g_ref,
                                     w1_v, w3_v, w2_v, act_ref, rows) for rows in slices])

    @pl.when(is_lat)
    def _():
        for rows, out in zip(slices, outs):
            o_ref[rows, :] = out

    @pl.when(jnp.logical_not(is_lat))
    def _():
        for rows, out in zip(slices, outs):
            oc_ref[rows, :] = out


def _ffn_weight_scratch(w1, w2):
    return [pltpu.VMEM(w1.shape, BF16), pltpu.VMEM(w1.shape, BF16), pltpu.VMEM(w2.shape, BF16),
            pltpu.VMEM((W_STAGE_SLOTS, w1.shape[0] // W_STAGE_CHUNKS, w1.shape[1]), F32),
            pltpu.VMEM((W_STAGE_SLOTS, w2.shape[0] // W_STAGE_CHUNKS, w2.shape[1]), F32),
            pltpu.SemaphoreType.DMA((W_STAGE_SLOTS,))]


def _ffn_call(x2d, c2d, mod3, lat_row, ctx_row, g, w1, w3, w2, tm):
    n, d = x2d.shape
    n_lat, n_ctx = n // tm, c2d.shape[0] // tm
    lat_blk = lambda i: (jnp.minimum(i, n_lat - 1), 0)
    ctx_blk = lambda i: (jnp.maximum(i - n_lat, 0), 0)
    any_spec = pl.BlockSpec(memory_space=pl.ANY)
    return pl.pallas_call(
        functools.partial(_ffn_kernel, n_lat=n_lat),
        out_shape=(jax.ShapeDtypeStruct((n, d), F32), jax.ShapeDtypeStruct(c2d.shape, F32)),
        grid=(n_lat + n_ctx,),
        in_specs=[pl.BlockSpec((tm, d), lat_blk), pl.BlockSpec((tm, d), ctx_blk),
                  pl.BlockSpec((None, N_MOD, d), lambda i: (jnp.where(i < n_lat, lat_row(i), ctx_row), 0, 0)),
                  _const_spec((1, d)), any_spec, any_spec, any_spec],
        out_specs=(pl.BlockSpec((tm, d), lat_blk), pl.BlockSpec((tm, d), ctx_blk)),
        scratch_shapes=[pltpu.VMEM((tm, D_FF), BF16)] + _ffn_weight_scratch(w1, w2),
        compiler_params=_params(1),
        name="ffn1",
    )(x2d, c2d, mod3, g, w1, w3, w2)


def _rope(x, cos, sin_signed, even):
    nxt = pltpu.roll(x, LANES - 1, axis=1)
    prv = pltpu.roll(x, 1, axis=1)
    return x * cos + jnp.where(even, nxt, prv) * sin_signed


def _inproj_kernel(x_ref, mod_ref, g_ref, wdn_ref, wat_ref, wba_ref, alog_ref, dtb_ref, qn_ref, kn_ref, cos_ref, sin_ref,
                   qkv_ref, z_ref, bg_ref, aq_ref, ak_ref, av_ref, *, rope):
    tm = x_ref.shape[0]
    x = x_ref[...]
    h = (_rms(x, g_ref[...]) * (1.0 + mod_ref[4:5, :]) + mod_ref[3:4, :]).astype(BF16)

    p_ba = _dot(h, wba_ref[...])
    p_q = _dot(h, wat_ref[:, :AT_WIDTH])

    beta = _sigmoid(p_ba)
    t = p_ba + dtb_ref[...]
    softplus = jnp.maximum(t, 0.0) + jnp.log(1.0 + jnp.exp(-jnp.abs(t)))
    g = -jnp.exp(alog_ref[...]) * softplus
    ri = lax.broadcasted_iota(jnp.int32, (CHUNK, CHUNK), 0)
    ci = lax.broadcasted_iota(jnp.int32, (CHUNK, CHUNK), 1)
    ltri = jnp.where(ri >= ci, 1.0, 0.0).astype(BF16)
    kind = lax.broadcasted_iota(jnp.int32, (CHUNK, LANES), 1) % DN_BG_ROWS
    for c in range(tm // CHUNK):
        rows = slice(c * CHUNK, (c + 1) * CHUNK)
        gc = g[rows]
        g1 = gc.astype(BF16)
        r1 = gc - g1.astype(F32)
        g2 = r1.astype(BF16)
        g3 = (r1 - g2.astype(F32)).astype(BF16)
        pre = _dot(ltri, g1) + _dot(ltri, g2) + _dot(ltri, g3)
        suf = pre[CHUNK - 1:CHUNK, :] - pre + gc
        tile = jnp.where(kind < 2, beta[rows], jnp.where(kind == 2, pre, suf))
        bg_ref[:, rows] = tile.T[:DN_HEADS * DN_BG_ROWS]

    p_kv = _dot(h, wat_ref[:, AT_WIDTH:])
    lane = lax.broadcasted_iota(jnp.int32, (tm, LANES), 1)
    even = (lane % 2) == 0
    if rope:
        cos = cos_ref[...]
        sin = sin_ref[...]
    for hd in range(AT_HEADS):
        q = _rms(p_q[:, hd * LANES:(hd + 1) * LANES], qn_ref[...])
        if rope:
            q = _rope(q, cos, sin, even)
        aq_ref[hd // 2, :, (hd % 2) * LANES:(hd % 2 + 1) * LANES] = (q * (ATT_SCALE * LOG2E)).astype(BF16)
    for hd in range(AT_KV_HEADS):
        k = _rms(p_kv[:, hd * LANES:(hd + 1) * LANES], kn_ref[...])
        if rope:
            k = _rope(k, cos, sin, even)
        ak_ref[hd] = k.astype(BF16)
        av_ref[hd] = p_kv[:, (AT_KV_HEADS + hd) * LANES:(AT_KV_HEADS + hd + 1) * LANES].T.astype(BF16)

    z_ref[...] = _dot(h, wdn_ref[:, LEN_DN_QKV:]).astype(BF16)
    for part in range(3):
        p = _dot(h, wdn_ref[:, part * DN_WIDTH:(part + 1) * DN_WIDTH])
        for hd in range(DN_HEADS):
            qkv_ref[part * DN_HEADS + hd] = p[:, hd * LANES:(hd + 1) * LANES].astype(BF16)


def _inproj_call(x2d, mod3, mod_row, g, w_parts, alog_row, dtb_row, qn, kn, cos, sin, bsz, t, tm, rope, name):
    n, d = x2d.shape
    tpb = t // tm
    bt = lambda i: (i // tpb, 0, i % tpb, 0)
    out_shape = (
        jax.ShapeDtypeStruct((bsz, 3 * DN_HEADS, t, LANES), BF16),
        jax.ShapeDtypeStruct((n, DN_WIDTH), BF16),
        jax.ShapeDtypeStruct((bsz, DN_HEADS * DN_BG_ROWS, t), F32),
        jax.ShapeDtypeStruct((bsz, AT_KV_HEADS, t, 2 * AT_HD), BF16),
        jax.ShapeDtypeStruct((bsz, AT_KV_HEADS, t, AT_HD), BF16),
        jax.ShapeDtypeStruct((bsz, AT_KV_HEADS, AT_HD, t), BF16),
    )
    out_specs = (
        pl.BlockSpec((None, 3 * DN_HEADS, tm, LANES), bt),
        pl.BlockSpec((tm, DN_WIDTH), lambda i: (i, 0)),
        pl.BlockSpec((None, DN_HEADS * DN_BG_ROWS, tm), lambda i: (i // tpb, 0, i % tpb)),
        pl.BlockSpec((None, AT_KV_HEADS, tm, 2 * AT_HD), bt),
        pl.BlockSpec((None, AT_KV_HEADS, tm, AT_HD), bt),
        pl.BlockSpec((None, AT_KV_HEADS, AT_HD, tm), lambda i: (i // tpb, 0, 0, i % tpb)),
    )
    return pl.pallas_call(
        functools.partial(_inproj_kernel, rope=rope),
        out_shape=out_shape,
        grid=(n // tm,),
        in_specs=[pl.BlockSpec((tm, d), lambda i: (i, 0)),
                  pl.BlockSpec((None, N_MOD, d), lambda i: (mod_row(i), 0, 0)),
                  _const_spec((1, d)),
                  *[_const_spec(w.shape) for w in w_parts],
                  _const_spec((1, LANES)), _const_spec((1, LANES)),
                  _const_spec((1, LANES)), _const_spec((1, LANES)),
                  pl.BlockSpec((tm, LANES), lambda i: (i % tpb, 0)),
                  pl.BlockSpec((tm, LANES), lambda i: (i % tpb, 0))],
        out_specs=out_specs,
        compiler_params=_params(1),
        name=name,
    )(x2d, mod3, g, *w_parts, alog_row, dtb_row, qn, kn, cos, sin)


def _dn_conv_all(src_refs, w_refs, xpad_ref, t, emits):
    nblk = t // CHUNK
    n = len(src_refs)
    for i in range(n):
        xpad_ref[i, 0:8, :] = jnp.zeros((8, LANES), F32)
        xpad_ref[i, 8 + t:16 + t, :] = jnp.zeros((8, LANES), F32)

    def fill(r, carry):
        r0 = pl.multiple_of(r * CHUNK, CHUNK)
        for i in range(n):
            xpad_ref[i, pl.ds(r0 + 8, CHUNK), :] = src_refs[i][pl.ds(r0, CHUNK), :].astype(F32)
        return carry

    lax.fori_loop(0, nblk, fill, 0)
    ws = [w_ref[...] for w_ref in w_refs]
    pad = (DN_CONV - 1) // 2

    def conv(r, carry):
        for u in range(DN_CONV_BLOCKS):
            r0 = pl.multiple_of((r * DN_CONV_BLOCKS + u) * CHUNK, CHUNK)
            for i in range(n):
                acc = None
                for j in range(DN_CONV):
                    tap = xpad_ref[i, pl.ds(r0 + (8 - pad + j), CHUNK), :] * ws[i][j:j + 1, :]
                    acc = tap if acc is None else acc + tap
                emits[i](r0, _silu(acc))
        return carry

    lax.fori_loop(0, nblk // DN_CONV_BLOCKS, conv, 0)


def _block_diag(cat):
    c = cat.shape[0]
    zero = jnp.zeros((c, c), cat.dtype)
    return jnp.concatenate([jnp.concatenate([cat[:, :c], zero], axis=1),
                            jnp.concatenate([zero, cat[:, c:]], axis=1)], axis=0)


def _dn_prep_chunk(a, qs_ref, ks_ref, vs_ref, bg_ref, lvl_ref, wq_ref, ak_ref, u_ref, eg_ref):
    row = pl.multiple_of(a * CHUNK, CHUNK)
    k = ks_ref[pl.ds(row, CHUNK), :]
    q = qs_ref[pl.ds(row, CHUNK), :]
    v = vs_ref[pl.ds(row, CHUNK), :]
    bg = bg_ref[a]
    kf = k.astype(F32)
    ri = lax.broadcasted_iota(jnp.int32, (CHUNK, CHUNK), 0)
    ci = lax.broadcasted_iota(jnp.int32, (CHUNK, CHUNK), 1)
    per_dir = []
    for d in range(2):
        beta = jnp.broadcast_to(bg[d:d + 1, :], (CHUNK, CHUNK)).T
        g_r = jnp.broadcast_to(bg[2 + d:3 + d, :], (CHUNK, CHUNK))
        g_c = g_r.T
        incl = (ri >= ci) if d == 0 else (ri <= ci)
        decay = jnp.exp(jnp.where(incl, g_c - g_r, NEG_BIG))
        per_dir.append((beta, g_c, decay, kf * beta))
    aq = lax.dot_general(jnp.concatenate([per_dir[0][3].astype(BF16), per_dir[1][3].astype(BF16), q], axis=0),
                         k, _NT, preferred_element_type=F32)
    yield
    qk = aq[2 * CHUNK:]
    a_cat = jnp.concatenate([jnp.where(ri > ci, aq[:CHUNK] * per_dir[0][2], 0.0),
                             jnp.where(ri < ci, aq[CHUNK:2 * CHUNK] * per_dir[1][2], 0.0)], axis=1)

    rhs = []
    for d in range(2):
        beta, g_c, decay, kb = per_dir[d]
        eg = jnp.exp(g_c)
        g_last = g_c[CHUNK - 1:CHUNK, :] if d == 0 else g_c[0:1, :]
        k_tail = kf * jnp.exp(g_last - g_c)
        ak_ref[d, a] = jnp.concatenate([(qk * decay).astype(BF16), k_tail.T.astype(BF16)], axis=0)
        wq_ref[d, a, CHUNK:, :] = (q.astype(F32) * eg).astype(BF16)
        eg_ref[d, a] = jnp.broadcast_to(jnp.exp(g_last), (8, LANES))
        rhs.append(jnp.concatenate([v * beta, kb * eg], axis=1))

    eye = jnp.where(ri == ci, 1.0, 0.0)
    eye_cat = jnp.concatenate([eye, eye], axis=1)
    pair = jnp.where((ri >> 1) == (ci >> 1), 1.0, 0.0)
    t_cat = (eye_cat - a_cat * jnp.concatenate([pair, pair], axis=1)).astype(BF16)
    a_bf = a_cat.astype(BF16)
    for lvl in range(lvl_ref.shape[0]):
        x = _dot(t_cat, _block_diag(a_bf * lvl_ref[lvl]))
        yield
        t_cat = t_cat - _dot(x.astype(BF16), _block_diag(t_cat)).astype(BF16)
        yield
    toff = t_cat - eye_cat.astype(BF16)

    for d in range(2):
        uw = rhs[d] + _dot(toff[:, d * CHUNK:(d + 1) * CHUNK], rhs[d].astype(BF16))
        u_ref[d, a] = uw[:, :DN_DV].astype(BF16)
        wq_ref[d, a, :CHUNK, :] = uw[:, DN_DV:].astype(BF16)


def _dn_prep_kernel(ql_ref, kl_ref, vl_ref, qc_ref, kc_ref, vc_ref, wq_w_ref, wk_w_ref, wv_w_ref, bg_ref,
                    lvl_ref, wq_ref, ak_ref, u_ref, eg_ref, qs_ref, ks_ref, vs_ref, xpad_ref):
    t = ql_ref.shape[0]
    tc = qc_ref.shape[0]

    def emit_qk(dst_ref, base, scale):
        def emit(r0, y):
            yn = y * lax.rsqrt(jnp.sum(y * y, axis=-1, keepdims=True) + EPS)
            dst_ref[pl.ds(base + r0, CHUNK), :] = (yn * scale).astype(BF16)
        return emit

    def emit_v(base):
        def emit(r0, y):
            vs_ref[pl.ds(base + r0, CHUNK), :] = y
        return emit

    w_refs = (wq_w_ref, wk_w_ref, wv_w_ref)
    for srcs, tt, base in (((qc_ref, kc_ref, vc_ref), tc, 0), ((ql_ref, kl_ref, vl_ref), t, tc)):
        _dn_conv_all(srcs, w_refs, xpad_ref, tt,
                     (emit_qk(qs_ref, base, DN_DK ** -0.5), emit_qk(ks_ref, base, 1.0), emit_v(base)))

    refs = (qs_ref, ks_ref, vs_ref, bg_ref, lvl_ref, wq_ref, ak_ref, u_ref, eg_ref)

    def group(gi, carry):
        _lockstep([_dn_prep_chunk(gi * DN_GROUP + j, *refs) for j in range(DN_GROUP)])
        return carry

    lax.fori_loop(0, (tc + t) // (CHUNK * DN_GROUP), group, 0)


def _dn_masks():
    i = np.arange(CHUNK)
    lvls = []
    b = 2
    while b < CHUNK:
        rb, cb = (i // b)[:, None], (i // b)[None, :]
        fwd = ((rb == cb + 1) & (rb % 2 == 1)).astype(np.float32)
        lvls.append(np.concatenate([fwd, fwd.T], axis=1))
        b *= 2
    return jnp.asarray(np.stack(lvls), dtype=BF16)


def _dn_prep_call(qkv_l, qkv_c, conv_w, bg_all, lvl_masks):
    bsz, _, t, _ = qkv_l.shape
    tc = qkv_c.shape[2]
    ta = t + tc
    nc = ta // CHUNK

    def head(part, tt):
        return pl.BlockSpec((None, None, tt, LANES), lambda b, h: (b, part * DN_HEADS + h, 0, 0))

    def wspec(part):
        return pl.BlockSpec((None, DN_CONV, LANES), lambda b, h: (part * DN_HEADS + h, 0, 0))

    def out(rows, dt):
        return (jax.ShapeDtypeStruct((bsz, DN_HEADS, 2, nc, rows, LANES), dt),
                pl.BlockSpec((None, None, 2, nc, rows, LANES), lambda b, h: (b, h, 0, 0, 0, 0)))

    outs = [out(2 * CHUNK, BF16), out(2 * CHUNK, BF16), out(CHUNK, BF16), out(8, F32)]
    return pl.pallas_call(
        _dn_prep_kernel,
        out_shape=tuple(o[0] for o in outs),
        grid=(bsz, DN_HEADS),
        in_specs=[head(0, t), head(1, t), head(2, t), head(0, tc), head(1, tc), head(2, tc),
                  wspec(0), wspec(1), wspec(2),
                  pl.BlockSpec((None, None, nc, DN_BG_ROWS, LANES), lambda b, h: (b, h, 0, 0, 0)),
                  _const_spec(lvl_masks.shape)],
        out_specs=tuple(o[1] for o in outs),
        scratch_shapes=[pltpu.VMEM((ta, LANES), BF16), pltpu.VMEM((ta, LANES), BF16),
                        pltpu.VMEM((ta, LANES), F32), pltpu.VMEM((3, t + 16, LANES), F32)],
        compiler_params=_params(2),
        name="dn_prep",
    )(qkv_l, qkv_l, qkv_l, qkv_c, qkv_c, qkv_c, conv_w, conv_w, conv_w, bg_all, lvl_masks)


def _dn_scan_kernel(wq0_ref, wq1_ref, ak0_ref, ak1_ref, u0_ref, u1_ref, eg0_ref, eg1_ref, o0_ref, o1_ref, s_ref):
    @pl.when(pl.program_id(1) == 0)
    def _():
        s_ref[...] = jnp.zeros(s_ref.shape, F32)

    heads = range(DN_HEADS)
    s = [s_ref[h] for h in heads]
    for j in range(DN_SCAN_CHUNKS):
        jb = DN_SCAN_CHUNKS - 1 - j
        r1 = [_dot(jnp.concatenate([wq0_ref[h, j], wq1_ref[h, jb]], axis=1), _block_diag(s[h].astype(BF16)))
              for h in heads]
        v_new = [jnp.concatenate([u0_ref[h, j], u1_ref[h, jb]], axis=1).astype(F32) - r1[h][:CHUNK] for h in heads]
        r2 = [_dot(jnp.concatenate([ak0_ref[h, j], ak1_ref[h, jb]], axis=1), _block_diag(v_new[h].astype(BF16)))
              for h in heads]
        for h in heads:
            o = r1[h][CHUNK:] + r2[h][:CHUNK]
            o0_ref[j * CHUNK:(j + 1) * CHUNK, h * DN_DV:(h + 1) * DN_DV] = o[:, :DN_DV].astype(BF16)
            o1_ref[jb * CHUNK:(jb + 1) * CHUNK, h * DN_DV:(h + 1) * DN_DV] = o[:, DN_DV:].astype(BF16)
            dec = jnp.concatenate([eg0_ref[h, j, 0:1, :], eg1_ref[h, jb, 0:1, :]], axis=1)
            s[h] = s[h] * dec + r2[h][CHUNK:]
    for h in heads:
        s_ref[h] = s[h]


def _dn_scan_call(wq, ak, u, eg, t):
    bsz, nh, _, nc, _, _ = wq.shape
    k = DN_SCAN_CHUNKS
    nb = nc // k
    nb_ctx = (nc - t // CHUNK) // k

    def bwd_block(i):
        return jnp.where(i < nb_ctx, nb_ctx - 1 - i, nb - 1 + nb_ctx - i)

    def spec(rows, d):
        if d == 0:
            return pl.BlockSpec((None, nh, None, k, rows, LANES), lambda b, i: (b, 0, 0, i, 0, 0))
        return pl.BlockSpec((None, nh, None, k, rows, LANES), lambda b, i: (b, 0, 1, bwd_block(i), 0, 0))

    o_shape = jax.ShapeDtypeStruct((bsz, t, nh * DN_DV), BF16)
    o0_spec = pl.BlockSpec((None, k * CHUNK, nh * DN_DV), lambda b, i: (b, jnp.maximum(i - nb_ctx, 0), 0))
    o1_spec = pl.BlockSpec((None, k * CHUNK, nh * DN_DV),
                           lambda b, i: (b, bwd_block(jnp.maximum(i, nb_ctx)) - nb_ctx, 0))
    return pl.pallas_call(
        _dn_scan_kernel,
        out_shape=(o_shape, o_shape),
        grid=(bsz, nb),
        in_specs=[spec(2 * CHUNK, 0), spec(2 * CHUNK, 1), spec(2 * CHUNK, 0), spec(2 * CHUNK, 1),
                  spec(CHUNK, 0), spec(CHUNK, 1), spec(8, 0), spec(8, 1)],
        out_specs=(o0_spec, o1_spec),
        scratch_shapes=[pltpu.VMEM((nh, DN_DK, 2 * DN_DV), F32)],
        compiler_params=_params(2),
        name="dn_scan",
    )(wq, wq, ak, ak, u, u, eg, eg)


def _attn_kernel(q_ref, kl_ref, vtl_ref, kc_ref, vtc_ref, o_ref, kmax_ref, shift_ref):
    t = kl_ref.shape[0]
    blocks = [(kl_ref, vtl_ref, j * AT_KEYS, AT_KEYS) for j in range(t // AT_KEYS)]
    blocks.append((kc_ref, vtc_ref, 0, kc_ref.shape[0]))

    @pl.when(pl.program_id(2) == 0)
    def _():
        best = jnp.zeros((1, 1), F32)
        for k_ref, _, off, size in blocks:
            kk = k_ref[off:off + size, :].astype(F32)
            best = jnp.maximum(best, jnp.max(jnp.sum(kk * kk, axis=-1, keepdims=True), axis=0, keepdims=True))
        kmax_ref[...] = jnp.broadcast_to(best, kmax_ref.shape)

    tq = q_ref.shape[0]
    n = 2 * tq
    q = q_ref[...].astype(F32)
    qt32 = jnp.concatenate([q[:, :AT_HD].T, q[:, AT_HD:].T], axis=1)
    qt = qt32.astype(BF16)
    bound = jnp.sqrt(jnp.sum(qt32 * qt32, axis=0, keepdims=True) * kmax_ref[0:1, 0:1])
    shift_ref[...] = bound

    def scores(blk):
        k_ref, _, off, size = blk
        return _dot(k_ref[off:off + size, :], qt)

    @pl.when(jnp.max(bound) > AT_SAFE_LOG2)
    def _():
        m = jnp.full((1, n), NEG_BIG, F32)
        for blk in blocks:
            m = jnp.maximum(m, jnp.max(scores(blk), axis=0, keepdims=True))
        shift_ref[...] = m

    shift = shift_ref[...]
    den = jnp.zeros((1, n), F32)
    acc = jnp.zeros((AT_HD, n), F32)
    s_next = scores(blocks[0])
    for j, (_, vt_ref, off, size) in enumerate(blocks):
        s = s_next
        if j + 1 < len(blocks):
            s_next = scores(blocks[j + 1])
        p = jnp.exp2(s - shift)
        den = den + jnp.sum(p, axis=0, keepdims=True)
        acc = acc + _dot(vt_ref[:, off:off + size], p.astype(BF16))
    o = (acc / den).T
    o_ref[:, :AT_HD] = o[:tq].astype(BF16)
    o_ref[:, AT_HD:] = o[tq:].astype(BF16)


def _attn_call(aq, ak_l, avt_l, ak_c, avt_c, tq):
    bsz, hkv, t, _ = aq.shape
    tc = ak_c.shape[2]
    kspec = lambda tt: pl.BlockSpec((None, None, tt, AT_HD), lambda b, j, i: (b, j, 0, 0))
    vspec = lambda tt: pl.BlockSpec((None, None, AT_HD, tt), lambda b, j, i: (b, j, 0, 0))
    return pl.pallas_call(
        _attn_kernel,
        out_shape=jax.ShapeDtypeStruct((bsz, t, AT_WIDTH), BF16),
        grid=(bsz, hkv, t // tq),
        in_specs=[pl.BlockSpec((None, None, tq, 2 * AT_HD), lambda b, j, i: (b, j, i, 0)),
                  kspec(t), vspec(t), kspec(tc), vspec(tc)],
        out_specs=pl.BlockSpec((None, tq, 2 * AT_HD), lambda b, j, i: (b, i, j)),
        scratch_shapes=[pltpu.VMEM((8, LANES), F32), pltpu.VMEM((1, 2 * tq), F32)],
        compiler_params=_params(3),
        name="attn",
    )(aq, ak_l, avt_l, ak_c, avt_c)


def _tail_kernel(x_ref, o0_ref, o1_ref, z_ref, at_ref, mod_ref, gn_ref, wo_ref, g_ref, w1_hbm, w3_hbm, w2_hbm,
                 gf_ref, o_ref, act_ref, dn_ref, w1_ref, w3_ref, w2_ref, stage_a, stage_b, sem):
    @pl.when(pl.program_id(0) == 0)
    def _():
        _load_ffn_weights((w1_hbm, w3_hbm, w2_hbm), (w1_ref, w3_ref, w2_ref), stage_a, stage_b, sem)

    def rows_gen(rows):
        for hd in range(DN_HEADS):
            sl = slice(hd * DN_DV, (hd + 1) * DN_DV)
            o = o0_ref[rows, sl].astype(F32) + o1_ref[rows, sl].astype(F32)
            dn_ref[rows, sl] = (_rms(o, gn_ref[...]) * _silu(z_ref[rows, sl].astype(F32))).astype(BF16)
        mix = _dot(dn_ref[rows, :], wo_ref[:DN_WIDTH, :]) + _dot(at_ref[rows, :], wo_ref[DN_WIDTH:, :])
        yield
        x = x_ref[rows, :] + mod_ref[5:6, :] * mix
        x = yield from _swiglu_update(x, mod_ref, 6, g_ref, w1_ref, w3_ref, w2_ref, act_ref, rows)
        o_ref[rows, :] = _rms(x, gf_ref[...])

    _lockstep([rows_gen(rows) for rows in _row_slices(x_ref.shape[0])])


def _tail_call(x2d, o0, o1, z2d, at2d, mod3, mod_row, gn, wo, g, w1, w3, w2, gf, tm):
    n, d = x2d.shape
    return pl.pallas_call(
        _tail_kernel,
        out_shape=jax.ShapeDtypeStruct((n, d), F32),
        grid=(n // tm,),
        in_specs=[pl.BlockSpec((tm, d), lambda i: (i, 0)),
                  pl.BlockSpec((tm, DN_WIDTH), lambda i: (i, 0)),
                  pl.BlockSpec((tm, DN_WIDTH), lambda i: (i, 0)),
                  pl.BlockSpec((tm, DN_WIDTH), lambda i: (i, 0)),
                  pl.BlockSpec((tm, AT_WIDTH), lambda i: (i, 0)),
                  pl.BlockSpec((None, N_MOD, d), lambda i: (mod_row(i), 0, 0)),
                  _const_spec((1, LANES)),
                  _const_spec(wo.shape), _const_spec((1, d)),
                  pl.BlockSpec(memory_space=pl.ANY), pl.BlockSpec(memory_space=pl.ANY), pl.BlockSpec(memory_space=pl.ANY),
                  _const_spec((1, d))],
        out_specs=pl.BlockSpec((tm, d), lambda i: (i, 0)),
        scratch_shapes=[pltpu.VMEM((tm, D_FF), BF16), pltpu.VMEM((tm, DN_WIDTH), BF16)] + _ffn_weight_scratch(w1, w2),
        compiler_params=_params(1),
        name="tail",
    )(x2d, o0, o1, z2d, at2d, mod3, gn, wo, g, w1, w3, w2, gf)


def _rope_tables(n):
    pos = np.arange(n)
    freqs = 1.0 / (ROPE_THETA ** (np.arange(0, ROPE_AXIS_DIM, 2, dtype=np.float64) / ROPE_AXIS_DIM))
    ang = np.concatenate([(pos // GRID_W)[:, None] * freqs, (pos % GRID_W)[:, None] * freqs], axis=-1)
    sign = np.tile(np.array([-1.0, 1.0]), AT_HD // 2)
    cos = np.repeat(np.cos(ang), 2, axis=-1)
    sin = np.repeat(np.sin(ang), 2, axis=-1) * sign
    return jnp.asarray(cos, F32), jnp.asarray(sin, F32)


def _head_kind_lanes(b_part, a_part):
    lead = b_part.shape[:-1]
    kinds = jnp.stack([b_part.reshape(*lead, 2, DN_HEADS), a_part.reshape(*lead, 2, DN_HEADS)], axis=-3)
    kinds = jnp.moveaxis(kinds.reshape(*lead, 4, DN_HEADS), -1, -2)
    kinds = jnp.pad(kinds, [(0, 0)] * len(lead) + [(0, 0), (0, DN_BG_ROWS - 4)])
    flat = kinds.reshape(*lead, DN_HEADS * DN_BG_ROWS)
    return jnp.pad(flat, [(0, 0)] * len(lead) + [(0, LANES - DN_HEADS * DN_BG_ROWS)])


def _split_w_in(w_in):
    ba = _head_kind_lanes(w_in[:, OFF_DN_B:OFF_DN_A], w_in[:, OFF_DN_A:OFF_AT_Q])
    return w_in[:, :OFF_DN_B].astype(BF16), w_in[:, OFF_AT_Q:].astype(BF16), ba.astype(BF16)


def kernel(x, c, ctx, c_ctx, w_mod, b_mod, g_ffn1, ffn1_w1, ffn1_w3, ffn1_w2, g_mix, w_in, dn_conv, dn_a_log,
           dn_dt_bias, dn_norm, q_norm, k_norm, w_out, g_ffn2, ffn2_w1, ffn2_w3, ffn2_w2, g_final):
    bsz, t, d = x.shape
    tc = ctx.shape[1]
    assert w_mod.shape[0] == 1, "single-layer block"
    attn_tq = 1024
    assert bsz < 8 and d == D_MODEL
    assert t % max(TM_FFN, TM_INPROJ, TM_TAIL, AT_KEYS, attn_tq) == 0 and (bsz * tc) % TM_FFN == 0
    assert tc % (CHUNK * max(DN_CONV_BLOCKS, DN_SCAN_CHUNKS)) == 0 and t % (CHUNK * max(DN_CONV_BLOCKS, DN_SCAN_CHUNKS)) == 0
    assert (t + tc) % (CHUNK * DN_GROUP) == 0
    ctx_row = bsz

    cc = jnp.zeros((8, d), F32).at[:bsz].set(c).at[ctx_row].set(c_ctx)
    mod3 = _mod_call(cc, w_mod[0], b_mod[0][None]).reshape(8, N_MOD, d)

    lat_row = lambda tile: (lambda i: i // (t // tile))
    ctx_rowf = lambda i: ctx_row
    row = lambda v: v.reshape(1, -1)

    x1, c1 = _ffn_call(x.reshape(bsz * t, d), ctx.reshape(bsz * tc, d), mod3, lat_row(TM_FFN), ctx_row, row(g_ffn1[0]),
                       ffn1_w1[0], ffn1_w3[0], ffn1_w2[0], TM_FFN)

    w_parts = _split_w_in(w_in[0])
    no_b = jnp.zeros((1, 2 * DN_HEADS), F32)
    alog_row = _head_kind_lanes(no_b, dn_a_log[0].reshape(1, -1))
    dtb_row = _head_kind_lanes(no_b, dn_dt_bias[0].reshape(1, -1))
    cos, sin = _rope_tables(t)
    qn, kn = row(q_norm[0]), row(k_norm[0])
    qkv_l, z_l, bg_l, aq_l, ak_l, av_l = _inproj_call(
        x1, mod3, lat_row(TM_INPROJ), row(g_mix[0]), w_parts, alog_row, dtb_row, qn, kn, cos, sin, bsz, t, TM_INPROJ, True,
        "inproj_lat")
    qkv_c, _, bg_c, _, ak_c, av_c = _inproj_call(
        c1, mod3, ctx_rowf, row(g_mix[0]), w_parts, alog_row, dtb_row, qn, kn, cos[:tc], sin[:tc], bsz, tc, tc, False,
        "inproj_ctx")

    bg_all = jnp.concatenate([bg_c, bg_l], axis=2)
    bg_all = bg_all.reshape(bsz, DN_HEADS, DN_BG_ROWS, (tc + t) // CHUNK, CHUNK).transpose(0, 1, 3, 2, 4)
    conv_w = dn_conv[0].reshape(DN_CONV, 3 * DN_HEADS, LANES).transpose(1, 0, 2)
    dn_wq, dn_ak, dn_u, dn_eg = _dn_prep_call(qkv_l, qkv_c, conv_w, bg_all, _dn_masks())
    o_fwd, o_bwd = _dn_scan_call(dn_wq, dn_ak, dn_u, dn_eg, t)
    at_lat = _attn_call(aq_l, ak_l, av_l, ak_c, av_c, attn_tq)

    out = _tail_call(x1, o_fwd.reshape(bsz * t, DN_WIDTH), o_bwd.reshape(bsz * t, DN_WIDTH), z_l,
                     at_lat.reshape(bsz * t, AT_WIDTH), mod3, lat_row(TM_TAIL), row(dn_norm[0]), w_out[0].astype(BF16),
                     row(g_ffn2[0]), ffn2_w1[0], ffn2_w3[0], ffn2_w2[0],
                     row(g_final), TM_TAIL)
    return out.reshape(bsz, t, d)
```
